```python
import math
import functools
import numpy as np
import jax
import jax.numpy as jnp
from jax import lax

D_MODEL = 1024
BATCH = 4
SEQ = 4096
DEPTH = 2

GRID_W = 64
CTX_LEN = 256
EPS = 1e-6
F32 = jnp.float32

ATT_HEADS = 4
ATT_DH = 64
ATT_VD = 2 * ATT_DH
ATT_W = ATT_HEADS * ATT_VD
Q_BLOCK = 128
ROPE_BASE = 10000.0

DN_HEADS = 4
DN_DH = 64
DN_W = DN_HEADS * DN_DH
DN_CONV = 5
DN_CHUNK = 64

POOL_WINDOWS = (2, 4, 8, 16)
POOL_GD = 64
POOL_W = len(POOL_WINDOWS) * POOL_GD

N_BRANCH = 3
MIX_W = ATT_W + DN_W + POOL_W

IN_SPLITS = (ATT_HEADS * 2 * ATT_DH, ATT_HEADS * 2 * ATT_DH, ATT_W,
             DN_W, DN_W, DN_W, DN_W, 2 * DN_HEADS, 2 * DN_HEADS,
             POOL_W, N_BRANCH * D_MODEL)
IN_W = sum(IN_SPLITS)

FFN_DENSE = ((8 * D_MODEL // 3 + 255) // 256) * 256
N_EXPERTS = 8
TOP_K = 2
FFN_EXPERT = 7 * D_MODEL // 2
N_DENSE = (DEPTH + 1) // 2
N_MOE = DEPTH // 2

kernel_name = 'hybrid_diffattn_gdn_pool_moe_dit'


def _rmsnorm(x, g):
    xf = x.astype(F32)
    y = xf * lax.rsqrt(jnp.mean(xf * xf, axis=-1, keepdims=True) + EPS)
    return (y * g.astype(F32)).astype(x.dtype)


def _modulate(x, shift, scale):
    return x * (1 + scale) + shift


def _split(x, sizes):
    return jnp.split(x, np.cumsum(sizes)[:-1].tolist(), axis=-1)


def _l2norm(x):
    return x * lax.rsqrt(jnp.sum(x * x, axis=-1, keepdims=True) + EPS)


def _conv_centred(x, w):
    k = w.shape[0]
    return lax.conv_general_dilated(
        x, w[:, None, :].astype(x.dtype), window_strides=(1,),
        padding=[(k // 2, k // 2)], dimension_numbers=('NWC', 'WIO', 'NWC'),
        feature_group_count=x.shape[-1])


def _axial_rope_tables(rows):
    row = jnp.repeat(jnp.arange(rows), GRID_W).astype(F32)
    col = jnp.tile(jnp.arange(GRID_W), rows).astype(F32)
    n_freq = ATT_DH // 4
    inv = ROPE_BASE ** (-jnp.arange(n_freq, dtype=F32) / n_freq)
    ar = row[:, None] * inv
    ac = col[:, None] * inv
    return (jnp.cos(ar), jnp.sin(ar), jnp.cos(ac), jnp.sin(ac))


def _rotate(x, cos, sin):
    x1, x2 = jnp.split(x, 2, axis=-1)
    c = cos[:, None, None, :]
    s = sin[:, None, None, :]
    return jnp.concatenate([x1 * c - x2 * s, x2 * c + x1 * s], axis=-1)


def _apply_axial_rope(x, rope):
    cr, sr, cc, sc = rope
    xf = x.astype(F32)
    half = ATT_DH // 2
    out = jnp.concatenate([_rotate(xf[..., :half], cr, sr), _rotate(xf[..., half:], cc, sc)], axis=-1)
    return out.astype(x.dtype)


def _diff_softmax_attend(q, k, v, lam):
    s = jnp.einsum('bqhmd,bkhmd->bhmqk', q, k).astype(F32) * (ATT_DH ** -0.5)
    p = jax.nn.softmax(s, axis=-1)
    a = p[:, :, 0] - lam * p[:, :, 1]
    return jnp.einsum('bhqk,bkhe->bqhe', a.astype(v.dtype), v)


def _diff_attention(q_l, k_l, v_l, q_c, k_c, v_c, lam_params, subln_g, lambda_init, rope, ctx_out):
    B, S, _ = q_l.shape
    C = q_c.shape[1]
    ql = _apply_axial_rope(q_l.reshape(B, S, ATT_HEADS, 2, ATT_DH), rope)
    kl = _apply_axial_rope(k_l.reshape(B, S, ATT_HEADS, 2, ATT_DH), rope)
    vl = v_l.reshape(B, S, ATT_HEADS, ATT_VD)
    kc = k_c.reshape(B, C, ATT_HEADS, 2, ATT_DH)
    vc = v_c.reshape(B, C, ATT_HEADS, ATT_VD)
    lp = lam_params.astype(F32)
    lam = jnp.exp(jnp.sum(lp[0] * lp[1])) - jnp.exp(jnp.sum(lp[2] * lp[3])) + lambda_init
    k_all = jnp.concatenate([kc, kl], axis=1)
    v_all = jnp.concatenate([vc, vl], axis=1)
    nb = S // Q_BLOCK
    qb = jnp.moveaxis(ql.reshape(B, nb, Q_BLOCK, ATT_HEADS, 2, ATT_DH), 1, 0)
    ob = lax.map(lambda blk: _diff_softmax_attend(blk, k_all, v_all, lam), qb)
    o_l = jnp.moveaxis(ob, 0, 1).reshape(B, S, ATT_HEADS, ATT_VD)
    y_l = (_rmsnorm(o_l, subln_g) * (1 - lambda_init)).reshape(B, S, ATT_W)
    y_c = None
    if ctx_out:
        qc = q_c.reshape(B, C, ATT_HEADS, 2, ATT_DH)
        o_c = _diff_softmax_attend(qc, kc, vc, lam)
        y_c = (_rmsnorm(o_c, subln_g) * (1 - lambda_init)).reshape(B, C, ATT_W)
    return y_l, y_c


def _dn_inputs(q, k, v, a, b, conv_w, a_log, dt_bias):
    B, L, _ = q.shape
    qkv = jax.nn.silu(_conv_centred(jnp.concatenate([q, k, v], axis=-1), conv_w)).astype(F32)
    q, k, v = [t.reshape(B, L, DN_HEADS, DN_DH) for t in jnp.split(qkv, 3, axis=-1)]
    q = _l2norm(q) * (DN_DH ** -0.5)
    k = _l2norm(k)
    a = a.astype(F32).reshape(B, L, 2, DN_HEADS)
    b = b.astype(F32).reshape(B, L, 2, DN_HEADS)
    g = -jnp.exp(a_log.astype(F32)) * jax.nn.softplus(a + dt_bias.astype(F32))
    beta = jax.nn.sigmoid(b)
    return q, k, v, g, beta


def _chunk_gated_delta(q, k, v, beta, g, s0, with_output):
    B, L, H, Dk = q.shape
    Dv = v.shape[-1]
    n = L // DN_CHUNK
    cn = DN_CHUNK

    def blocks(t):
        t = jnp.moveaxis(t, 2, 1)
        t = t.reshape(B, H, n, cn, *t.shape[3:])
        return jnp.moveaxis(t, 2, 0)

    q, k, v, beta, g = (blocks(t) for t in (q, k, v, beta, g))
    G = jnp.cumsum(g, axis=-1)
    incl = jnp.tril(jnp.ones((cn, cn), dtype=bool))
    strict = jnp.tril(jnp.ones((cn, cn), dtype=bool), -1)
    decay = jnp.exp(jnp.where(incl, G[..., :, None] - G[..., None, :], -jnp.inf))
    kb = k * beta[..., None]
    m = jnp.where(strict, jnp.einsum('nbhid,nbhjd->nbhij', kb, k) * decay, 0.0)
    a = m + jnp.eye(cn, dtype=m.dtype)
    rhs = jnp.concatenate([v * beta[..., None], kb * jnp.exp(G)[..., None]], axis=-1)
    sol = lax.linalg.triangular_solve(a, rhs, left_side=True, lower=True, unit_diagonal=True)
    u, w = sol[..., :Dv], sol[..., Dv:]
    kt = k * jnp.exp(G[..., -1:] - G)[..., None]
    g_last = jnp.exp(G[..., -1])

    if with_output:
        qk = jnp.einsum('nbhid,nbhjd->nbhij', q, k) * decay
        qg = q * jnp.exp(G)[..., None]

        def step(s, xs):
            qg_i, qk_i, kt_i, u_i, w_i, gl_i = xs
            v_new = u_i - jnp.einsum('bhcd,bhde->bhce', w_i, s)
            o_i = jnp.einsum('bhcd,bhde->bhce', qg_i, s) + jnp.einsum('bhij,bhje->bhie', qk_i, v_new)
            s = s * gl_i[..., None, None] + jnp.einsum('bhcd,bhce->bhde', kt_i, v_new)
            return s, o_i

        s_fin, o = lax.scan(step, s0, (qg, qk, kt, u, w, g_last))
        o = jnp.moveaxis(o, 0, 2).reshape(B, H, L, Dv)
        return jnp.moveaxis(o, 1, 2), s_fin

    def step_state(s, xs):
        kt_i, u_i, w_i, gl_i = xs
        v_new = u_i - jnp.einsum('bhcd,bhde->bhce', w_i, s)
        return s * gl_i[..., None, None] + jnp.einsum('bhcd,bhce->bhde', kt_i, v_new), None

    s_fin, _ = lax.scan(step_state, s0, (kt, u, w, g_last))
    return None, s_fin


def _gated_deltanet(lat_parts, ctx_parts, conv_w, a_log, dt_bias, norm_g, ctx_out):
    ql, kl, vl, zl, al, bl = lat_parts
    qc, kc, vc, zc, ac, bc = ctx_parts
    lat = _dn_inputs(ql, kl, vl, al, bl, conv_w, a_log, dt_bias)
    ctx = _dn_inputs(qc, kc, vc, ac, bc, conv_w, a_log, dt_bias)
    B = ql.shape[0]
    s0 = jnp.zeros((B, DN_HEADS, DN_DH, DN_DH), F32)
    outs_l, outs_c = [], []
    for d in range(2):
        def orient(t):
            return jnp.flip(t, axis=1) if d == 1 else t
        c_in = [orient(t) for t in ctx[:3]] + [orient(ctx[4][:, :, d]), orient(ctx[3][:, :, d])]
        o_c, s_c = _chunk_gated_delta(*c_in, s0, ctx_out)
        l_in = [orient(t) for t in lat[:3]] + [orient(lat[4][:, :, d]), orient(lat[3][:, :, d])]
        o_l, _ = _chunk_gated_delta(*l_in, s_c, True)
        outs_l.append(orient(o_l))
        if ctx_out:
            outs_c.append(orient(o_c))

    def out_gate(o, z):
        Bz, L, _ = z.shape
        y = _rmsnorm(o, norm_g) * jax.nn.silu(z.astype(F32).reshape(Bz, L, DN_HEADS, DN_DH))
        return y.reshape(Bz, L, DN_W).astype(z.dtype)

    y_l = out_gate(outs_l[0] + outs_l[1], zl)
    y_c = out_gate(outs_c[0] + outs_c[1], zc) if ctx_out else None
    return y_l, y_c


def _multiscale_pool(u, pool_w, pool_scale):
    B, L, _ = u.shape
    uf = u.astype(F32)
    cs = jnp.concatenate([jnp.zeros((B, 1, POOL_W), F32), jnp.cumsum(uf, axis=1)], axis=1)
    t = jnp.arange(L)
    groups = []
    for gi, win in enumerate(POOL_WINDOWS):
        lo = jnp.clip(t - win // 2, 0, L)
        hi = jnp.clip(t - win // 2 + win, 0, L)
        sl = slice(gi * POOL_GD, (gi + 1) * POOL_GD)
        cs_g = cs[..., sl]
        mean = (cs_g[:, hi] - cs_g[:, lo]) / (hi - lo).astype(F32)[None, :, None]
        groups.append(mean - uf[..., sl])
    m = jnp.stack(groups, axis=2)
    y = jnp.einsum('blgi,gio->blgo', m, pool_w.astype(F32)).reshape(B, L, POOL_W)
    return (y * pool_scale.astype(F32)).astype(u.dtype)


def _token_mixer(u_lat, u_ctx, w_in, attn_lambda, attn_subln_g, lambda_init, dn_conv_w, dn_a_log,
                 dn_dt_bias, dn_norm_g, pool_w, pool_scale, w_branch, w_out, rope, ctx_out):
    pl = _split(u_lat @ w_in, IN_SPLITS)
    pc = _split(u_ctx @ w_in, IN_SPLITS)
    att_l, att_c = _diff_attention(pl[0], pl[1], pl[2], pc[0], pc[1], pc[2], attn_lambda,
                                   attn_subln_g, lambda_init, rope, ctx_out)
    dn_l, dn_c = _gated_deltanet(pl[3:9], pc[3:9], dn_conv_w, dn_a_log, dn_dt_bias, dn_norm_g, ctx_out)
    wb_att = w_branch[:ATT_W]
    wb_dn = w_branch[ATT_W:ATT_W + DN_W]
    wb_pool = w_branch[ATT_W + DN_W:]

    def merge(att, dn, pool, gates):
        ga, gd, gp = jnp.split(jax.nn.sigmoid(gates), N_BRANCH, axis=-1)
        return (ga * (att @ wb_att) + gd * (dn @ wb_dn) + gp * (pool @ wb_pool)) @ w_out

    y_lat = merge(att_l, dn_l, _multiscale_pool(pl[9], pool_w, pool_scale), pl[10])
    y_ctx = None
    if ctx_out:
        y_ctx = merge(att_c, dn_c, _multiscale_pool(pc[9], pool_w, pool_scale), pc[10])
    return y_lat, y_ctx


def _swiglu(u, w1, w3, w2):
    return (jax.nn.silu(u @ w1) * (u @ w3)) @ w2


def _moe_swiglu(u, router_w, w1, w3, w2):
    logits = (u @ router_w).astype(F32)
    top_v, top_i = lax.top_k(logits, TOP_K)
    top_w = jax.nn.softmax(top_v, axis=-1)
    comb = jnp.sum(jax.nn.one_hot(top_i, N_EXPERTS, dtype=F32) * top_w[..., None], axis=-2)
    out = jnp.zeros(u.shape, F32)
    for e in range(N_EXPERTS):
        out = out + comb[..., e:e + 1] * _swiglu(u, w1[e], w3[e], w2[e]).astype(F32)
    return out.astype(u.dtype)


def setup_inputs(seed: int = 0) -> dict:
    key = jax.random.key(seed)
    ks = iter(jax.random.split(key, 32))
    D = D_MODEL

    def nrm(shape, scale):
        return jax.random.normal(next(ks), shape, F32) * scale

    x = nrm((BATCH, SEQ, D), 1.0)
    c = nrm((BATCH, D), 1.0)
    ctx = nrm((BATCH, CTX_LEN, D), 1.0)
    c_ctx = nrm((D,), 1.0)
    ada_w = nrm((DEPTH, D, 6 * D), 0.5 * D ** -0.5)
    ada_b = nrm((DEPTH, 6 * D), 0.01)
    norm1_g = 1.0 + nrm((DEPTH, D), 0.05)
    norm2_g = 1.0 + nrm((DEPTH, D), 0.05)
    w_in = nrm((DEPTH, D, IN_W), D ** -0.5)
    attn_lambda = nrm((DEPTH, 4, ATT_DH), 0.1)
    attn_subln_g = 1.0 + nrm((DEPTH, ATT_VD), 0.05)
    dn_conv_w = nrm((DEPTH, DN_CONV, 3 * DN_W), DN_CONV ** -0.5)
    dn_a_log = jnp.log(jax.random.uniform(next(ks), (DEPTH, 2, DN_HEADS), F32, minval=1.0, maxval=16.0))
    dt = jnp.exp(jax.random.uniform(next(ks), (DEPTH, 2, DN_HEADS), F32,
                                    minval=math.log(1e-3), maxval=math.log(1e-1)))
    dn_dt_bias = dt + jnp.log(-jnp.expm1(-dt))
    dn_norm_g = 1.0 + nrm((DEPTH, DN_DH), 0.05)
    pool_w = nrm((DEPTH, len(POOL_WINDOWS), POOL_GD, POOL_GD), POOL_GD ** -0.5)
    pool_scale = 1.0 + nrm((DEPTH, POOL_W), 0.05)
    w_branch = jnp.concatenate([nrm((DEPTH, ATT_W, D), ATT_W ** -0.5),
                                nrm((DEPTH, DN_W, D), DN_W ** -0.5),
                                nrm((DEPTH, POOL_W, D), POOL_W ** -0.5)], axis=1)
    w_out = nrm((DEPTH, D, D), D ** -0.5)
    ffn_w1 = nrm((N_DENSE, D, FFN_DENSE), D ** -0.5)
    ffn_w3 = nrm((N_DENSE, D, FFN_DENSE), D ** -0.5)
    ffn_w2 = nrm((N_DENSE, FFN_DENSE, D), FFN_DENSE ** -0.5)
    router_w = nrm((N_MOE, D, N_EXPERTS), D ** -0.5)
    moe_w1 = nrm((N_MOE, N_EXPERTS, D, FFN_EXPERT), D ** -0.5)
    moe_w3 = nrm((N_MOE, N_EXPERTS, D, FFN_EXPERT), D ** -0.5)
    moe_w2 = nrm((N_MOE, N_EXPERTS, FFN_EXPERT, D), FFN_EXPERT ** -0.5)
    final_norm_g = 1.0 + nrm((D,), 0.05)
    return {'x': x, 'c': c, 'ctx': ctx, 'c_ctx': c_ctx, 'ada_w': ada_w, 'ada_b': ada_b,
            'norm1_g': norm1_g, 'norm2_g': norm2_g, 'w_in': w_in, 'attn_lambda': attn_lambda,
            'attn_subln_g': attn_subln_g, 'dn_conv_w': dn_conv_w, 'dn_a_log': dn_a_log,
            'dn_dt_bias': dn_dt_bias, 'dn_norm_g': dn_norm_g, 'pool_w': pool_w, 'pool_scale': pool_scale,
            'w_branch': w_branch, 'w_out': w_out, 'ffn_w1': ffn_w1, 'ffn_w3': ffn_w3, 'ffn_w2': ffn_w2,
            'router_w': router_w, 'moe_w1': moe_w1, 'moe_w3': moe_w3, 'moe_w2': moe_w2,
            'final_norm_g': final_norm_g}


def reference(x, c, ctx, c_ctx, ada_w, ada_b, norm1_g, norm2_g, w_in, attn_lambda, attn_subln_g,
              dn_conv_w, dn_a_log, dn_dt_bias, dn_norm_g, pool_w, pool_scale, w_branch, w_out,
              ffn_w1, ffn_w3, ffn_w2, router_w, moe_w1, moe_w3, moe_w2, final_norm_g):
    rows = x.shape[1] // GRID_W
    rope = _axial_rope_tables(rows)
    h_lat, h_ctx = x, ctx
    for l in range(DEPTH):
        ctx_out = l < DEPTH - 1
        lambda_init = 0.8 - 0.6 * math.exp(-0.3 * l)
        mod_lat = [m[:, None, :] for m in jnp.split(jax.nn.silu(c) @ ada_w[l] + ada_b[l], 6, axis=-1)]
        mod_ctx = jnp.split(jax.nn.silu(c_ctx) @ ada_w[l] + ada_b[l], 6, axis=-1)
        u_lat = _modulate(_rmsnorm(h_lat, norm1_g[l]), mod_lat[0], mod_lat[1])
        u_ctx = _modulate(_rmsnorm(h_ctx, norm1_g[l]), mod_ctx[0], mod_ctx[1])
        y_lat, y_ctx = _token_mixer(u_lat, u_ctx, w_in[l], attn_lambda[l], attn_subln_g[l], lambda_init,
                                    dn_conv_w[l], dn_a_log[l], dn_dt_bias[l], dn_norm_g[l], pool_w[l],
                                    pool_scale[l], w_branch[l], w_out[l], rope, ctx_out)
        h_lat = h_lat + mod_lat[2] * y_lat
        if ctx_out:
            h_ctx = h_ctx + mod_ctx[2] * y_ctx
        if l % 2 == 0:
            ffn = functools.partial(_swiglu, w1=ffn_w1[l // 2], w3=ffn_w3[l // 2], w2=ffn_w2[l // 2])
        else:
            ffn = functools.partial(_moe_swiglu, router_w=router_w[l // 2], w1=moe_w1[l // 2],
                                    w3=moe_w3[l // 2], w2=moe_w2[l // 2])
        h_lat = h_lat + mod_lat[5] * ffn(_modulate(_rmsnorm(h_lat, norm2_g[l]), mod_lat[3], mod_lat[4]))
        if ctx_out:
            h_ctx = h_ctx + mod_ctx[5] * ffn(_modulate(_rmsnorm(h_ctx, norm2_g[l]), mod_ctx[3], mod_ctx[4]))
    return _rmsnorm(h_lat, final_norm_g)
```

```python
import functools
import math

import numpy as np
import jax
import jax.numpy as jnp
from jax import lax
from jax.experimental import pallas as pl
from jax.experimental.pallas import tpu as pltpu

F32 = jnp.float32
BF16 = jnp.bfloat16

EPS = 1e-6
GRID_W = 64
ROPE_BASE = 10000.0
ATT_HEADS = 4
ATT_DH = 64
ATT_VD = 2 * ATT_DH
ATT_W = ATT_HEADS * ATT_VD
DN_HEADS = 4
DN_DH = 64
DN_W = DN_HEADS * DN_DH
DN_CONV = 5
DN_CHUNK = 64
DN_SUB = 16
POOL_WINDOWS = (2, 4, 8, 16)
POOL_GD = 64
POOL_W = len(POOL_WINDOWS) * POOL_GD
N_EXPERTS = 8
TOP_K = 2

ROW_TILE = 256
HALO = 8
LANES = 128
VMEM_LIMIT = 56 * 1024 * 1024


def _cparams(sem):
    return pltpu.CompilerParams(dimension_semantics=sem, vmem_limit_bytes=VMEM_LIMIT)


def _dot(a, b):
    return jnp.dot(a.astype(BF16), b.astype(BF16), preferred_element_type=F32)


def _dot_nt(a, b):
    return lax.dot_general(a.astype(BF16), b.astype(BF16), (((1,), (1,)), ((), ())),
                           preferred_element_type=F32)


def _bmm(a, b):
    return lax.dot_general(a.astype(BF16), b.astype(BF16), (((2,), (1,)), ((0,), (0,))),
                           preferred_element_type=F32)


def _bmm_nt(a, b):
    return lax.dot_general(a.astype(BF16), b.astype(BF16), (((2,), (2,)), ((0,), (0,))),
                           preferred_element_type=F32)


def _sigmoid(x):
    return 1.0 / (1.0 + jnp.exp(-x))


def _silu(x):
    return x * _sigmoid(x)


def _const_spec(shape):
    nd = len(shape)
    return pl.BlockSpec(shape, lambda *_: (0,) * nd)


def _ada_kernel(c_ref, w_ref, b_ref, o_ref):
    o_ref[...] = _dot(_silu(c_ref[...]), w_ref[...]) + b_ref[...]


def _ada_mod(c_rows, ada_w, ada_b):
    depth, d, six_d = ada_w.shape
    n = six_d // d
    rows = c_rows.shape[0]
    return pl.pallas_call(
        _ada_kernel,
        grid=(depth, n),
        in_specs=[pl.BlockSpec((rows, d), lambda l, j: (0, 0)),
                  pl.BlockSpec((None, d, d), lambda l, j: (l, 0, j)),
                  pl.BlockSpec((None, 1, d), lambda l, j: (l, 0, j))],
        out_specs=pl.BlockSpec((None, rows, d), lambda l, j: (l, 0, j)),
        out_shape=jax.ShapeDtypeStruct((depth, rows, six_d), F32),
        compiler_params=_cparams(("parallel", "parallel")),
        name="ada_mod",
    )(c_rows, ada_w, ada_b.reshape(depth, 1, six_d))


def _norm_mod(x, g, shift, scale):
    y = x * lax.rsqrt(jnp.mean(x * x, axis=-1, keepdims=True) + EPS) * g
    return y * (1.0 + scale) + shift


def _inproj_kernel(h_ref, mod_ref, g_ref, cos_ref, su_ref, sd_ref,
                   wq_ref, wk_ref, wv_ref, wdn_ref, wab_ref, wpool_ref, wg_ref,
                   q_ref, k_ref, v_ref, dn_ref, ab_ref, pool_ref, gate_ref):
    u = _norm_mod(h_ref[...], g_ref[...], mod_ref[0:1, :], mod_ref[1:2, :]).astype(BF16)
    reps = ATT_W // LANES
    cos = jnp.tile(cos_ref[...], (1, reps))
    s_up = jnp.tile(su_ref[...], (1, reps))
    s_dn = jnp.tile(sd_ref[...], (1, reps))
    quarter = ATT_DH // 4

    def rope(t):
        return (t * cos + pltpu.roll(t, ATT_W - quarter, 1) * s_up + pltpu.roll(t, quarter, 1) * s_dn)

    q = rope(jnp.dot(u, wq_ref[...], preferred_element_type=F32))
    q_ref[...] = (q * (ATT_DH ** -0.5)).astype(q_ref.dtype)
    k_ref[...] = rope(jnp.dot(u, wk_ref[...], preferred_element_type=F32)).astype(k_ref.dtype)
    v_ref[...] = jnp.dot(u, wv_ref[...], preferred_element_type=F32).astype(v_ref.dtype)
    dn_ref[...] = jnp.dot(u, wdn_ref[...], preferred_element_type=F32)
    ab_ref[...] = jnp.dot(u, wab_ref[...], preferred_element_type=F32)
    pool_ref[...] = jnp.dot(u, wpool_ref[...], preferred_element_type=F32)
    gate_ref[...] = _sigmoid(jnp.dot(u, wg_ref[...], preferred_element_type=F32)).astype(gate_ref.dtype)


def _in_projection(h, mod, norm_g, rope_tabs, weights, n_batch, tiles_per_batch):
    rows, d = h.shape
    n_tiles = rows // ROW_TILE
    tpb = tiles_per_batch
    widths = [w.shape[1] for w in weights]
    out_dtypes = [BF16, BF16, BF16, F32, F32, F32, BF16]

    def mod_idx(i):
        return (jnp.where(i % tpb == 0, n_batch, i // tpb), 0, 0)

    in_specs = [pl.BlockSpec((ROW_TILE, d), lambda i: (i, 0)),
                pl.BlockSpec((None, 2, d), mod_idx),
                _const_spec((1, d))]
    in_specs += [pl.BlockSpec((ROW_TILE, LANES), lambda i: (i % tpb, 0)) for _ in range(3)]
    in_specs += [pl.BlockSpec(w.shape, lambda i: (0, 0), pipeline_mode=pl.Buffered(1)) for w in weights]
    out_specs = [pl.BlockSpec((ROW_TILE, w), lambda i: (i, 0)) for w in widths]
    out_shape = [jax.ShapeDtypeStruct((rows, w), dt) for w, dt in zip(widths, out_dtypes)]
    return pl.pallas_call(
        _inproj_kernel,
        grid=(n_tiles,),
        in_specs=in_specs,
        out_specs=out_specs,
        out_shape=out_shape,
        compiler_params=_cparams(("parallel",)),
        name="in_projection",
    )(h, mod, norm_g.reshape(1, d), *rope_tabs, *weights)


def _rope_tables(n_ctx, seq):
    rows = seq // GRID_W
    row = jnp.repeat(jnp.arange(rows), GRID_W).astype(F32)
    col = jnp.tile(jnp.arange(GRID_W), rows).astype(F32)
    n_freq = ATT_DH // 4
    inv = ROPE_BASE ** (-jnp.arange(n_freq, dtype=F32) / n_freq)
    ar = row[:, None] * inv
    ac = col[:, None] * inv
    cr, sr, cc, sc = jnp.cos(ar), jnp.sin(ar), jnp.cos(ac), jnp.sin(ac)
    z = jnp.zeros_like(sr)
    cos = jnp.concatenate([cr, cr, cc, cc], axis=-1)
    s_up = jnp.concatenate([-sr, z, -sc, z], axis=-1)
    s_dn = jnp.concatenate([z, sr, z, sc], axis=-1)

    def full(t, ctx_val):
        t = jnp.tile(t, (1, LANES // ATT_DH))
        return jnp.concatenate([jnp.full((n_ctx, LANES), ctx_val, F32), t], axis=0)

    return full(cos, 1.0), full(s_up, 0.0), full(s_dn, 0.0)


def _attn_body(lam_ref, g_ref, q_ref, k, v, o_ref, lambda_init):
    q = q_ref[...]
    lane = lax.broadcasted_iota(jnp.int32, q.shape, 1)
    outs = []
    for m in range(2):
        keep = (lane < ATT_DH) if m == 0 else (lane >= ATT_DH)
        qm = jnp.where(keep, q, jnp.zeros_like(q))
        s = _dot_nt(qm, k)
        e = jnp.exp(s - jnp.max(s, axis=-1, keepdims=True))
        den = jnp.sum(e, axis=-1, keepdims=True)
        outs.append(jnp.dot(e.astype(BF16), v, preferred_element_type=F32) / den)
    lp = lam_ref[...]
    lam = (jnp.exp(jnp.sum(lp[0:1] * lp[1:2], keepdims=True))
           - jnp.exp(jnp.sum(lp[2:3] * lp[3:4], keepdims=True)) + lambda_init)
    o = outs[0] - lam * outs[1]
    y = o * lax.rsqrt(jnp.mean(o * o, axis=-1, keepdims=True) + EPS) * g_ref[...]
    o_ref[...] = (y * (1.0 - lambda_init)).astype(o_ref.dtype)


def _attn_kernel(lam_ref, g_ref, q_ref, k_ref, v_ref, o_ref, *, lambda_init, n_ctx, first_tile):
    j = pl.program_id(2) + first_tile

    @pl.when(j == 0)
    def _():
        _attn_body(lam_ref, g_ref, q_ref, k_ref[0:n_ctx, :], v_ref[0:n_ctx, :], o_ref, lambda_init)

    @pl.when(j > 0)
    def _():
        _attn_body(lam_ref, g_ref, q_ref, k_ref[...], v_ref[...], o_ref, lambda_init)


def _diff_attention(q, k, v, lam_params, subln_g, lambda_init, n_ctx, first_tile):
    n_batch, t, _ = q.shape
    n_q = t // ROW_TILE - first_tile
    kern = functools.partial(_attn_kernel, lambda_init=lambda_init, n_ctx=n_ctx, first_tile=first_tile)
    return pl.pallas_call(
        kern,
        grid=(n_batch, ATT_HEADS, n_q),
        in_specs=[_const_spec(lam_params.shape),
                  _const_spec((1, ATT_VD)),
                  pl.BlockSpec((None, ROW_TILE, ATT_VD), lambda b, h, j: (b, j + first_tile, h)),
                  pl.BlockSpec((None, t, ATT_VD), lambda b, h, j: (b, 0, h)),
                  pl.BlockSpec((None, t, ATT_VD), lambda b, h, j: (b, 0, h))],
        out_specs=pl.BlockSpec((None, ROW_TILE, ATT_VD), lambda b, h, j: (b, j + first_tile, h)),
        out_shape=jax.ShapeDtypeStruct((n_batch, t, ATT_W), BF16),
        compiler_params=_cparams(("parallel", "parallel", "arbitrary")),
        name="diff_attention",
    )(lam_params, subln_g.reshape(1, ATT_VD), q, k, v)


def _halo_specs(width, tiles_per_batch):
    per = ROW_TILE // HALO
    n_blocks = None

    def prev_idx(b, j):
        return (b, jnp.maximum(j * per - 1, 0), 0)

    def next_idx(b, j):
        return (b, jnp.minimum((j + 1) * per, tiles_per_batch * per - 1), 0)

    del n_blocks
    return (pl.BlockSpec((None, HALO, width), prev_idx), pl.BlockSpec((None, HALO, width), next_idx))


def _with_halo(prev_ref, cur, next_ref, width, tiles_per_batch, j):
    left_ok = jnp.where(j > 1, 1.0, 0.0)
    right_ok = jnp.where(jnp.logical_and(j > 0, j < tiles_per_batch - 1), 1.0, 0.0)
    return jnp.concatenate([prev_ref[:, 0:width] * left_ok, cur, next_ref[:, 0:width] * right_ok], axis=0)


def _dn_prep_kernel(dn_ref, prev_ref, next_ref, ab_ref, cw_ref, alog_ref, dtb_ref,
                    u_ref, w_ref, qg_ref, qk_ref, kt_ref, gl_ref, *, tiles_per_batch):
    j = pl.program_id(1)
    width = 3 * DN_W
    xe = _with_halo(prev_ref, dn_ref[:, 0:width], next_ref, width, tiles_per_batch, j)
    half = DN_CONV // 2
    acc = None
    for tap in range(DN_CONV):
        lo = HALO - half + tap
        term = xe[lo:lo + ROW_TILE, :] * cw_ref[tap:tap + 1, :]
        acc = term if acc is None else acc + term
    qkv = _silu(acc)

    def l2n(x):
        return x * lax.rsqrt(jnp.sum(x * x, axis=-1, keepdims=True) + EPS)

    heads = range(DN_HEADS)
    qn = [l2n(qkv[:, h * DN_DH:(h + 1) * DN_DH]) * (DN_DH ** -0.5) for h in heads]
    kn = [l2n(qkv[:, DN_W + h * DN_DH:DN_W + (h + 1) * DN_DH]) for h in heads]
    vv = [qkv[:, 2 * DN_W + h * DN_DH:2 * DN_W + (h + 1) * DN_DH] for h in heads]
    kn_t = jnp.concatenate(kn, axis=-1).T

    ab = ab_ref[...]
    x = ab + dtb_ref[...]
    g = -jnp.exp(alog_ref[...]) * (jnp.maximum(x, 0.0) + jnp.log(1.0 + jnp.exp(-jnp.abs(x))))
    n_dh = 2 * DN_HEADS
    beta = _sigmoid(pltpu.roll(ab, LANES - n_dh, 1))
    r = lax.broadcasted_iota(jnp.int32, (ROW_TILE, ROW_TILE), 0)
    c = lax.broadcasted_iota(jnp.int32, (ROW_TILE, ROW_TILE), 1)
    same_chunk = (r >> 6) == (c >> 6)
    tri_f = jnp.where(jnp.logical_and(same_chunk, c <= r), 1.0, 0.0)
    tri_b = jnp.where(jnp.logical_and(same_chunk, c >= r), 1.0, 0.0)
    hp = lax.Precision.HIGHEST
    lane = lax.broadcasted_iota(jnp.int32, g.shape, 1)
    gcum = jnp.where(lane < DN_HEADS,
                     jnp.dot(tri_f, g, precision=hp, preferred_element_type=F32),
                     jnp.dot(tri_b, g, precision=hp, preferred_element_type=F32))
    gcum_t = gcum.T

    ri = lax.broadcasted_iota(jnp.int32, (DN_CHUNK, DN_CHUNK), 0)
    ci = lax.broadcasted_iota(jnp.int32, (DN_CHUNK, DN_CHUNK), 1)
    eye = jnp.where(ri == ci, 1.0, 0.0)
    same_sub = (ri >> 4) == (ci >> 4)
    n_chunks = ROW_TILE // DN_CHUNK

    for ch in range(n_chunks):
        rows = slice(ch * DN_CHUNK, (ch + 1) * DN_CHUNK)
        gram = [_dot_nt(kn[h][rows], kn[h][rows]) for h in heads]
        qkm = [_dot_nt(qn[h][rows], kn[h][rows]) for h in heads]
        m_list, rhs_list = [], []
        for d in range(2):
            incl = (ci <= ri) if d == 0 else (ci >= ri)
            strict = (ci < ri) if d == 0 else (ci > ri)
            last = DN_CHUNK - 1 if d == 0 else 0
            for h in heads:
                idx = d * DN_HEADS + h
                g_col = gcum[rows, idx:idx + 1]
                g_row = gcum_t[idx:idx + 1, rows]
                b_col = beta[rows, idx:idx + 1]
                decay = jnp.where(incl, jnp.exp(jnp.minimum(g_col - g_row, 0.0)), 0.0)
                eg = jnp.exp(g_col)
                m_list.append(jnp.where(strict, gram[h] * decay, 0.0) * b_col)
                rhs_list.append(jnp.concatenate([vv[h][rows] * b_col, kn[h][rows] * (b_col * eg)], axis=-1))
                g_last = g_row[:, last:last + 1]
                qg_ref[idx, rows, :] = (qn[h][rows] * eg).astype(qg_ref.dtype)
                qk_ref[idx, rows, :] = (qkm[h] * decay).astype(qk_ref.dtype)
                kt_ref[idx, ch] = (kn_t[h * DN_DH:(h + 1) * DN_DH, rows]
                                   * jnp.exp(g_last - g_row)).astype(kt_ref.dtype)
                gl_ref[idx, ch * HALO:(ch + 1) * HALO, :] = jnp.broadcast_to(jnp.exp(g_last), (HALO, DN_DH))
        m = jnp.stack(m_list)
        rhs = jnp.stack(rhs_list)
        n_diag = jnp.where(same_sub, m, 0.0)
        m_off = m - n_diag
        xp = -n_diag
        dinv = eye + xp
        for _ in range(3):
            xp = _bmm(xp, xp)
            dinv = dinv + _bmm(dinv, xp)
        f = _bmm(dinv, m_off)
        f2 = _bmm(f, f)
        y = _bmm(dinv, rhs)
        y = y - _bmm(f, y)
        sol = y + _bmm(f2, y)
        for idx in range(n_dh):
            u_ref[idx, rows, :] = sol[idx, :, 0:DN_DH]
            w_ref[idx, rows, :] = sol[idx, :, DN_DH:2 * DN_DH].astype(w_ref.dtype)


def _dn_prep(dn, ab, conv_w, a_log, dt_bias, tiles_per_batch):
    n_batch, t, dn_cols = dn.shape
    n_dh = 2 * DN_HEADS
    pad = jnp.zeros((1, LANES - n_dh), F32)
    alog_row = jnp.concatenate([a_log.reshape(1, n_dh).astype(F32), pad], axis=-1)
    dtb_row = jnp.concatenate([dt_bias.reshape(1, n_dh).astype(F32), pad], axis=-1)
    prev_spec, next_spec = _halo_specs(dn_cols, tiles_per_batch)
    tok = lambda b, j: (b, 0, j, 0)
    out_specs = [pl.BlockSpec((None, n_dh, ROW_TILE, DN_DH), tok)] * 4
    out_specs += [pl.BlockSpec((None, n_dh, ROW_TILE // DN_CHUNK, DN_DH, DN_CHUNK), lambda b, j: (b, 0, j, 0, 0)),
                  pl.BlockSpec((None, n_dh, HALO * ROW_TILE // DN_CHUNK, DN_DH), tok)]
    tok_shape = (n_batch, n_dh, t, DN_DH)
    out_shape = [jax.ShapeDtypeStruct(tok_shape, F32),
                 jax.ShapeDtypeStruct(tok_shape, BF16),
                 jax.ShapeDtypeStruct(tok_shape, BF16),
                 jax.ShapeDtypeStruct(tok_shape, BF16),
                 jax.ShapeDtypeStruct((n_batch, n_dh, t // DN_CHUNK, DN_DH, DN_CHUNK), BF16),
                 jax.ShapeDtypeStruct((n_batch, n_dh, HALO * t // DN_CHUNK, DN_DH), F32)]
    return pl.pallas_call(
        functools.partial(_dn_prep_kernel, tiles_per_batch=tiles_per_batch),
        grid=(n_batch, tiles_per_batch),
        in_specs=[pl.BlockSpec((None, ROW_TILE, dn_cols), lambda b, j: (b, j, 0)),
                  prev_spec, next_spec,
                  pl.BlockSpec((None, ROW_TILE, LANES), lambda b, j: (b, j, 0)),
                  _const_spec(conv_w.shape), _const_spec((1, LANES)), _const_spec((1, LANES))],
        out_specs=out_specs,
        out_shape=out_shape,
        compiler_params=_cparams(("parallel", "parallel")),
        name="dn_prep",
    )(dn, dn, dn, ab, conv_w, alog_row, dtb_row)


def _dn_scan_kernel(*refs):
    ins, (of_ref, ob_ref, s_ref) = refs[:12], refs[12:]

    @pl.when(pl.program_id(1) == 0)
    def _():
        s_ref[...] = jnp.zeros_like(s_ref)

    for d, o_ref in enumerate((of_ref, ob_ref)):
        u_ref, w_ref, qg_ref, qk_ref, kt_ref, gl_ref = ins[6 * d:6 * d + 6]
        s = s_ref[d]
        v_new = u_ref[...] - _bmm(w_ref[...], s)
        o_ref[...] = _bmm(qg_ref[...], s) + _bmm(qk_ref[...], v_new)
        s_ref[d] = s * gl_ref[:, 0:1, :] + _bmm(kt_ref[...], v_new)


def _dn_scan(prep, n_ctx):
    u, w, qg, qk, kt, gl = prep
    n_batch, _, t, _ = u.shape
    n_steps = t // DN_CHUNK
    ctx_chunks = n_ctx // DN_CHUNK

    def chunk_of(d, i):
        if d == 0:
            return i
        return jnp.where(i < ctx_chunks, ctx_chunks - 1 - i, n_steps - 1 + ctx_chunks - i)

    in_specs = []
    for d in range(2):
        tok = functools.partial(lambda b, i, d: (b, d, chunk_of(d, i), 0), d=d)
        in_specs += [pl.BlockSpec((None, DN_HEADS, DN_CHUNK, DN_DH), tok)] * 4
        in_specs += [pl.BlockSpec((None, DN_HEADS, None, DN_DH, DN_CHUNK),
                                  functools.partial(lambda b, i, d: (b, d, chunk_of(d, i), 0, 0), d=d)),
                     pl.BlockSpec((None, DN_HEADS, HALO, DN_DH), tok)]
    out_specs = [pl.BlockSpec((None, DN_HEADS, DN_CHUNK, DN_DH),
                              functools.partial(lambda b, i, d: (b, 0, chunk_of(d, i), 0), d=d))
                 for d in range(2)]
    out_shape = [jax.ShapeDtypeStruct((n_batch, DN_HEADS, t, DN_DH), F32)] * 2
    return pl.pallas_call(
        _dn_scan_kernel,
        grid=(n_batch, n_steps),
        in_specs=in_specs,
        out_specs=out_specs,
        out_shape=out_shape,
        scratch_shapes=[pltpu.VMEM((2, DN_HEADS, DN_DH, DN_DH), F32)],
        compiler_params=_cparams(("parallel", "arbitrary")),
        name="dn_scan",
    )(u, w, qg, qk, kt, gl, u, w, qg, qk, kt, gl)


def _merge_kernel(h_ref, mod_ref, att_ref, of_ref, ob_ref, z_ref, dng_ref,
                  pool_ref, pprev_ref, pnext_ref, pw_ref, ps_ref, gate_ref,
                  wba_ref, wbd_ref, wbp_ref, wo_ref, o_ref, *, tiles_per_batch, first_tile, seq, n_ctx):
    j = pl.program_id(1) + first_tile
    d = h_ref.shape[-1]
    dn_heads = []
    for h in range(DN_HEADS):
        o = of_ref[h] + ob_ref[h]
        dn_heads.append(o * lax.rsqrt(jnp.mean(o * o, axis=-1, keepdims=True) + EPS) * dng_ref[...])
    dn = jnp.concatenate(dn_heads, axis=-1) * _silu(z_ref[...])
    cur = pool_ref[...]
    xe = _with_halo(pprev_ref, cur, pnext_ref, POOL_W, tiles_per_batch, j)
    length = jnp.where(j == 0, n_ctx, seq)
    t0 = jnp.where(j == 0, 0, (j - 1) * ROW_TILE)
    tpos = t0 + lax.broadcasted_iota(jnp.int32, (ROW_TILE, POOL_W), 0)
    lane = lax.broadcasted_iota(jnp.int32, (ROW_TILE, POOL_W), 1)
    n_ext = ROW_TILE + 2 * HALO
    sums = {1: xe}
    win = 1
    while win < max(POOL_WINDOWS):
        prev = sums[win]
        rows = prev.shape[0] - win
        sums[2 * win] = prev[0:rows, :] + prev[win:win + rows, :]
        win *= 2
    del n_ext
    mean = jnp.zeros((ROW_TILE, POOL_W), F32)
    for gi, win in enumerate(POOL_WINDOWS):
        start = HALO - win // 2
        wsum = sums[win][start:start + ROW_TILE, :]
        lo = jnp.clip(tpos - win // 2, 0, length)
        hi = jnp.clip(tpos - win // 2 + win, 0, length)
        mean = jnp.where((lane >> 6) == gi, wsum / (hi - lo).astype(F32), mean)
    pooled = _dot(mean - cur, pw_ref[...]) * ps_ref[...]
    ga = gate_ref[:, 0:d].astype(F32)
    gd = gate_ref[:, d:2 * d].astype(F32)
    gp = gate_ref[:, 2 * d:3 * d].astype(F32)
    mix = (ga * jnp.dot(att_ref[...], wba_ref[...], preferred_element_type=F32)
           + gd * _dot(dn, wbd_ref[...]) + gp * _dot(pooled, wbp_ref[...]))
    y = _dot(mix, wo_ref[...])
    o_ref[...] = h_ref[...] + mod_ref[0:1, :] * y


def _merge(h, mod_gate, att, o_f, o_b, dn, dn_norm_g, pool, pool_w_bd, pool_scale, gates,
           wb_att, wb_dn, wb_pool, w_out, first_tile, n_ctx):
    n_batch, t, d = h.shape
    tpb = t // ROW_TILE
    n_rows = tpb - first_tile
    seq = t - n_ctx
    ft = first_tile
    prev_spec, next_spec = _halo_specs(POOL_W, tpb)
    shift = lambda f: (lambda b, j: f(b, j + ft))
    prev_spec = pl.BlockSpec(prev_spec.block_shape, shift(prev_spec.index_map))
    next_spec = pl.BlockSpec(next_spec.block_shape, shift(next_spec.index_map))
    row = lambda b, j: (b, j + ft, 0)

    def mod_idx(b, j):
        return (jnp.where(j + ft == 0, n_batch, b), 0, 0)

    weights = [wb_att, wb_dn, wb_pool, w_out]
    kern = functools.partial(_merge_kernel, tiles_per_batch=tpb, first_tile=ft, seq=seq, n_ctx=n_ctx)
    return pl.pallas_call(
        kern,
        grid=(n_batch, n_rows),
        in_specs=[pl.BlockSpec((None, ROW_TILE, d), row),
                  pl.BlockSpec((None, 1, d), mod_idx),
                  pl.BlockSpec((None, ROW_TILE, ATT_W), row),
                  pl.BlockSpec((None, DN_HEADS, ROW_TILE, DN_DH), lambda b, j: (b, 0, j + ft, 0)),
                  pl.BlockSpec((None, DN_HEADS, ROW_TILE, DN_DH), lambda b, j: (b, 0, j + ft, 0)),
                  pl.BlockSpec((None, ROW_TILE, DN_W), lambda b, j: (b, j + ft, 3)),
                  _const_spec((1, DN_DH)),
                  pl.BlockSpec((None, ROW_TILE, POOL_W), row),
                  prev_spec, next_spec,
                  _const_spec(pool_w_bd.shape), _const_spec((1, POOL_W)),
                  pl.BlockSpec((None, ROW_TILE, 3 * d), row)]
                 + [_const_spec(w.shape) for w in weights],
        out_specs=pl.BlockSpec((None, ROW_TILE, d), row),
        out_shape=jax.ShapeDtypeStruct((n_batch, t, d), F32),
        input_output_aliases={0: 0},
        compiler_params=_cparams(("parallel", "parallel")),
        name="merge",
    )(h, mod_gate, att, o_f, o_b, dn, dn_norm_g.reshape(1, DN_DH), pool, pool, pool,
      pool_w_bd, pool_scale.reshape(1, POOL_W), gates, *weights)


def _ffn_kernel(*refs, n_experts, n_fchunks, final_norm):
    if n_experts > 1:
        h_ref, mod_ref, g_ref, fg_ref, rw_ref, w1_ref, w3_ref, w2_ref, o_ref, u_ref, acc_ref, comb_ref = refs
    else:
        h_ref, mod_ref, g_ref, fg_ref, w1_ref, w3_ref, w2_ref, o_ref, u_ref, acc_ref = refs
    step = pl.program_id(2)
    e = step // n_fchunks

    @pl.when(step == 0)
    def _():
        u = _norm_mod(h_ref[...], g_ref[...], mod_ref[0:1, :], mod_ref[1:2, :])
        u_ref[...] = u.astype(BF16)
        acc_ref[...] = jnp.zeros_like(acc_ref)
        if n_experts > 1:
            logits = _dot(u, rw_ref[...])
            lane = lax.broadcasted_iota(jnp.int32, logits.shape, 1).astype(F32)
            neg = jnp.float32(-jnp.inf)
            lg = jnp.where(lane < n_experts, logits, neg)
            v1 = jnp.max(lg, axis=-1, keepdims=True)
            i1 = jnp.min(jnp.where(lg == v1, lane, float(LANES)), axis=-1, keepdims=True)
            lg2 = jnp.where(lane == i1, neg, lg)
            v2 = jnp.max(lg2, axis=-1, keepdims=True)
            i2 = jnp.min(jnp.where(lg2 == v2, lane, float(LANES)), axis=-1, keepdims=True)
            e2 = jnp.exp(v2 - v1)
            den = 1.0 + e2
            comb_ref[...] = jnp.where(lane == i1, 1.0 / den, 0.0) + jnp.where(lane == i2, e2 / den, 0.0)

    u = u_ref[...]
    hid = _silu(jnp.dot(u, w1_ref[...], preferred_element_type=F32)) * jnp.dot(
        u, w3_ref[...], preferred_element_type=F32)
    y = _dot(hid, w2_ref[...])
    if n_experts > 1:
        lane = lax.broadcasted_iota(jnp.int32, comb_ref.shape, 1)
        y = y * jnp.sum(jnp.where(lane == e, comb_ref[...], 0.0), axis=-1, keepdims=True)
    acc_ref[...] += y

    @pl.when(step == n_experts * n_fchunks - 1)
    def _():
        out = h_ref[...] + mod_ref[2:3, :] * acc_ref[...]
        if final_norm:
            out = out * lax.rsqrt(jnp.mean(out * out, axis=-1, keepdims=True) + EPS) * fg_ref[...]
        o_ref[...] = out


def _ffn(h, mod, norm_g, final_g, router_w, w1, w3, w2, first_tile, final_norm):
    n_batch, t, d = h.shape
    n_exp, _, f = w1.shape
    tpb = t // ROW_TILE
    ft = first_tile
    n_rows = tpb - ft
    row = lambda b, j, e: (b, j + ft, 0)

    def mod_idx(b, j, e):
        return (jnp.where(j + ft == 0, n_batch, b), 0, 0)

    in_specs = [pl.BlockSpec((None, ROW_TILE, d), row),
                pl.BlockSpec((None, 3, d), mod_idx),
                _const_spec((1, d)), _const_spec((1, d))]
    args = [h, mod, norm_g.reshape(1, d), final_g.reshape(1, d)]
    scratch = [pltpu.VMEM((ROW_TILE, d), BF16), pltpu.VMEM((ROW_TILE, d), F32)]
    if n_exp > 1:
        in_specs.append(_const_spec(router_w.shape))
        args.append(router_w)
        scratch.append(pltpu.VMEM((ROW_TILE, LANES), F32))
    nfc = 1 if n_exp == 1 else 2
    fc = f // nfc
    assert fc * nfc == f and fc % LANES == 0
    in_specs += [pl.BlockSpec((None, d, fc), lambda b, j, s: (s // nfc, 0, s % nfc)),
                 pl.BlockSpec((None, d, fc), lambda b, j, s: (s // nfc, 0, s % nfc)),
                 pl.BlockSpec((None, fc, d), lambda b, j, s: (s // nfc, s % nfc, 0))]
    args += [w1, w3, w2]
    out_t = n_rows * ROW_TILE if final_norm else t
    out_row = (lambda b, j, e: (b, j, 0)) if final_norm else row
    kw = {} if final_norm else {"input_output_aliases": {0: 0}}
    return pl.pallas_call(
        functools.partial(_ffn_kernel, n_experts=n_exp, n_fchunks=nfc, final_norm=final_norm),
        grid=(n_batch, n_rows, n_exp * nfc),
        in_specs=in_specs,
        out_specs=pl.BlockSpec((None, ROW_TILE, d), out_row),
        out_shape=jax.ShapeDtypeStruct((n_batch, out_t, d), F32),
        scratch_shapes=scratch,
        compiler_params=_cparams(("parallel", "parallel", "arbitrary")),
        name="ffn_moe" if n_exp > 1 else "ffn_dense",
        **kw,
    )(*args)


def _block_diag(w):
    g, n, _ = w.shape
    out = jnp.zeros((g * n, g * n), w.dtype)
    for i in range(g):
        out = out.at[i * n:(i + 1) * n, i * n:(i + 1) * n].set(w[i])
    return out


def kernel(x, c, ctx, c_ctx, ada_w, ada_b, norm1_g, norm2_g, w_in, attn_lambda, attn_subln_g, dn_conv_w,
           dn_a_log, dn_dt_bias, dn_norm_g, pool_w, pool_scale, w_branch, w_out, ffn_w1, ffn_w3, ffn_w2,
           router_w, moe_w1, moe_w3, moe_w2, final_norm_g):
    n_batch, seq, d = x.shape
    n_ctx = ctx.shape[1]
    depth = ada_w.shape[0]
    assert n_ctx == ROW_TILE and seq % ROW_TILE == 0 and seq % GRID_W == 0 and d == 1024
    t = n_ctx + seq
    tpb = t // ROW_TILE

    h = jnp.concatenate([ctx, x], axis=1)
    c_rows = jnp.concatenate([c, c_ctx[None, :], jnp.zeros((8 - n_batch - 1, d), F32)], axis=0)
    mods = _ada_mod(c_rows, ada_w, ada_b).reshape(depth, 8, 6, d)
    rope_tabs = _rope_tables(n_ctx, seq)

    sizes = (ATT_W, ATT_W, ATT_W, DN_W, DN_W, DN_W, DN_W, 2 * DN_HEADS, 2 * DN_HEADS, POOL_W, 3 * d)
    offs = np.concatenate([[0], np.cumsum(sizes)]).tolist()

    for l in range(depth):
        last = l == depth - 1
        first_tile = 1 if last else 0
        lambda_init = 0.8 - 0.6 * math.exp(-0.3 * l)
        wl = w_in[l].astype(BF16)
        seg = lambda a, b: wl[:, offs[a]:offs[b]]
        w_ab = jnp.concatenate([seg(7, 9), jnp.zeros((d, LANES - 4 * DN_HEADS), BF16)], axis=1)
        weights = [seg(0, 1), seg(1, 2), seg(2, 3), seg(3, 7), w_ab, seg(9, 10), seg(10, 11)]
        mod = mods[l, :n_batch + 1]
        q, k, v, dn, ab, pool, gates = _in_projection(
            h.reshape(n_batch * t, d), mod[:, 0:2], norm1_g[l], rope_tabs, weights, n_batch, tpb)
        q, k, v = (a.reshape(n_batch, t, ATT_W) for a in (q, k, v))
        att = _diff_attention(q, k, v, attn_lambda[l], attn_subln_g[l], lambda_init, n_ctx, first_tile)
        dn = dn.reshape(n_batch, t, 4 * DN_W)
        prep = _dn_prep(dn, ab.reshape(n_batch, t, LANES), dn_conv_w[l], dn_a_log[l], dn_dt_bias[l], tpb)
        o_f, o_b = _dn_scan(prep, n_ctx)
        wb = w_branch[l].astype(BF16)
        h = _merge(h, mod[:, 2:3], att, o_f, o_b, dn, dn_norm_g[l], pool.reshape(n_batch, t, POOL_W),
                   _block_diag(pool_w[l]).astype(BF16), pool_scale[l], gates.reshape(n_batch, t, 3 * d),
                   wb[:ATT_W], wb[ATT_W:ATT_W + DN_W], wb[ATT_W + DN_W:], w_out[l].astype(BF16),
                   first_tile, n_ctx)
        if l % 2 == 0:
            i = l // 2
            rw = None
            w1, w3, w2 = ffn_w1[i][None], ffn_w3[i][None], ffn_w2[i][None]
        else:
            i = l // 2
            rw = jnp.concatenate([router_w[i], jnp.zeros((d, LANES - N_EXPERTS), F32)], axis=1).astype(BF16)
            w1, w3, w2 = moe_w1[i], moe_w3[i], moe_w2[i]
        h = _ffn(h, mod[:, 3:6], norm2_g[l], final_norm_g, rw, w1.astype(BF16), w3.astype(BF16),
                 w2.astype(BF16), first_tile, last)
    return h
```

```python
import functools
import math

import numpy as np
import jax
import jax.numpy as jnp
from jax import lax
from jax.experimental import pallas as pl
from jax.experimental.pallas import tpu as pltpu

F32 = jnp.float32
BF16 = jnp.bfloat16

EPS = 1e-6
GRID_W = 64
ROPE_BASE = 10000.0
ATT_HEADS = 4
ATT_DH = 64
ATT_VD = 2 * ATT_DH
ATT_W = ATT_HEADS * ATT_VD
DN_HEADS = 4
DN_DH = 64
DN_W = DN_HEADS * DN_DH
DN_CONV = 5
DN_CHUNK = 64
DN_SUB = 16
POOL_WINDOWS = (2, 4, 8, 16)
POOL_GD = 64
POOL_W = len(POOL_WINDOWS) * POOL_GD
N_EXPERTS = 8
TOP_K = 2

ROW_TILE = 256
HALO = 8
LANES = 128
VMEM_LIMIT = 56 * 1024 * 1024


def _cparams(sem):
    return pltpu.CompilerParams(dimension_semantics=sem, vmem_limit_bytes=VMEM_LIMIT)


def _dot(a, b):
    return jnp.dot(a.astype(BF16), b.astype(BF16), preferred_element_type=F32)


def _dot_nt(a, b):
    return lax.dot_general(a.astype(BF16), b.astype(BF16), (((1,), (1,)), ((), ())),
                           preferred_element_type=F32)


def _bmm(a, b):
    return lax.dot_general(a.astype(BF16), b.astype(BF16), (((2,), (1,)), ((0,), (0,))),
                           preferred_element_type=F32)


def _bmm_nt(a, b):
    return lax.dot_general(a.astype(BF16), b.astype(BF16), (((2,), (2,)), ((0,), (0,))),
                           preferred_element_type=F32)


def _sigmoid(x):
    return 1.0 / (1.0 + jnp.exp(-x))


def _silu(x):
    return x * _sigmoid(x)


def _const_spec(shape):
    nd = len(shape)
    return pl.BlockSpec(shape, lambda *_: (0,) * nd)


def _ada_kernel(c_ref, w_ref, b_ref, o_ref):
    o_ref[...] = _dot(_silu(c_ref[...]), w_ref[...]) + b_ref[...]


def _ada_mod(c_rows, ada_w, ada_b):
    depth, d, six_d = ada_w.shape
    n = six_d // d
    rows = c_rows.shape[0]
    return pl.pallas_call(
        _ada_kernel,
        grid=(depth, n),
        in_specs=[pl.BlockSpec((rows, d), lambda l, j: (0, 0)),
                  pl.BlockSpec((None, d, d), lambda l, j: (l, 0, j)),
                  pl.BlockSpec((None, 1, d), lambda l, j: (l, 0, j))],
        out_specs=pl.BlockSpec((None, rows, d), lambda l, j: (l, 0, j)),
        out_shape=jax.ShapeDtypeStruct((depth, rows, six_d), F32),
        compiler_params=_cparams(("parallel", "parallel")),
        name="ada_mod",
    )(c_rows, ada_w, ada_b.reshape(depth, 1, six_d))


def _norm_mod(x, g, shift, scale):
    y = x * lax.rsqrt(jnp.mean(x * x, axis=-1, keepdims=True) + EPS) * g
    return y * (1.0 + scale) + shift


def _inproj_kernel(h_ref, mod_ref, g_ref, cos_ref, su_ref, sd_ref,
                   wq_ref, wk_ref, wv_ref, wdn_ref, wab_ref, wpool_ref, wg_ref,
                   q_ref, k_ref, v_ref, dn_ref, ab_ref, pool_ref, gate_ref):
    u = _norm_mod(h_ref[...], g_ref[...], mod_ref[0:1, :], mod_ref[1:2, :]).astype(BF16)
    reps = ATT_W // LANES
    cos = jnp.tile(cos_ref[...], (1, reps))
    s_up = jnp.tile(su_ref[...], (1, reps))
    s_dn = jnp.tile(sd_ref[...], (1, reps))
    quarter = ATT_DH // 4

    def rope(t):
        return (t * cos + pltpu.roll(t, ATT_W - quarter, 1) * s_up + pltpu.roll(t, quarter, 1) * s_dn)

    q = rope(jnp.dot(u, wq_ref[...], preferred_element_type=F32))
    q_ref[...] = (q * (ATT_DH ** -0.5)).astype(q_ref.dtype)
    k_ref[...] = rope(jnp.dot(u, wk_ref[...], preferred_element_type=F32)).astype(k_ref.dtype)
    v_ref[...] = jnp.dot(u, wv_ref[...], preferred_element_type=F32).astype(v_ref.dtype)
    dn_ref[...] = jnp.dot(u, wdn_ref[...], preferred_element_type=F32)
    ab_ref[...] = jnp.dot(u, wab_ref[...], preferred_element_type=F32)
    pool_ref[...] = jnp.dot(u, wpool_ref[...], preferred_element_type=F32)
    gate_ref[...] = _sigmoid(jnp.dot(u, wg_ref[...], preferred_element_type=F32)).astype(gate_ref.dtype)


def _in_projection(h, mod, norm_g, rope_tabs, weights, n_batch, tiles_per_batch):
    rows, d = h.shape
    n_tiles = rows // ROW_TILE
    tpb = tiles_per_batch
    widths = [w.shape[1] for w in weights]
    out_dtypes = [BF16, BF16, BF16, F32, F32, F32, BF16]

    def mod_idx(i):
        return (jnp.where(i % tpb == 0, n_batch, i // tpb), 0, 0)

    in_specs = [pl.BlockSpec((ROW_TILE, d), lambda i: (i, 0)),
                pl.BlockSpec((None, 2, d), mod_idx),
                _const_spec((1, d))]
    in_specs += [pl.BlockSpec((ROW_TILE, LANES), lambda i: (i % tpb, 0)) for _ in range(3)]
    in_specs += [pl.BlockSpec(w.shape, lambda i: (0, 0), pipeline_mode=pl.Buffered(1)) for w in weights]
    out_specs = [pl.BlockSpec((ROW_TILE, w), lambda i: (i, 0)) for w in widths]
    out_shape = [jax.ShapeDtypeStruct((rows, w), dt) for w, dt in zip(widths, out_dtypes)]
    return pl.pallas_call(
        _inproj_kernel,
        grid=(n_tiles,),
        in_specs=in_specs,
        out_specs=out_specs,
        out_shape=out_shape,
        compiler_params=_cparams(("parallel",)),
        name="in_projection",
    )(h, mod, norm_g.reshape(1, d), *rope_tabs, *weights)


def _rope_tables(n_ctx, seq):
    rows = seq // GRID_W
    row = jnp.repeat(jnp.arange(rows), GRID_W).astype(F32)
    col = jnp.tile(jnp.arange(GRID_W), rows).astype(F32)
    n_freq = ATT_DH // 4
    inv = ROPE_BASE ** (-jnp.arange(n_freq, dtype=F32) / n_freq)
    ar = row[:, None] * inv
    ac = col[:, None] * inv
    cr, sr, cc, sc = jnp.cos(ar), jnp.sin(ar), jnp.cos(ac), jnp.sin(ac)
    z = jnp.zeros_like(sr)
    cos = jnp.concatenate([cr, cr, cc, cc], axis=-1)
    s_up = jnp.concatenate([-sr, z, -sc, z], axis=-1)
    s_dn = jnp.concatenate([z, sr, z, sc], axis=-1)

    def full(t, ctx_val):
        t = jnp.tile(t, (1, LANES // ATT_DH))
        return jnp.concatenate([jnp.full((n_ctx, LANES), ctx_val, F32), t], axis=0)

    return full(cos, 1.0), full(s_up, 0.0), full(s_dn, 0.0)


def _attn_body(lam_ref, g_ref, q_ref, k, v, o_ref, lambda_init):
    q = q_ref[...]
    lane = lax.broadcasted_iota(jnp.int32, q.shape, 1)
    outs = []
    for m in range(2):
        keep = (lane < ATT_DH) if m == 0 else (lane >= ATT_DH)
        qm = jnp.where(keep, q, jnp.zeros_like(q))
        s = _dot_nt(qm, k)
        e = jnp.exp(s - jnp.max(s, axis=-1, keepdims=True))
        den = jnp.sum(e, axis=-1, keepdims=True)
        outs.append(jnp.dot(e.astype(BF16), v, preferred_element_type=F32) / den)
    lp = lam_ref[...]
    lam = (jnp.exp(jnp.sum(lp[0:1] * lp[1:2], keepdims=True))
           - jnp.exp(jnp.sum(lp[2:3] * lp[3:4], keepdims=True)) + lambda_init)
    o = outs[0] - lam * outs[1]
    y = o * lax.rsqrt(jnp.mean(o * o, axis=-1, keepdims=True) + EPS) * g_ref[...]
    o_ref[...] = (y * (1.0 - lambda_init)).astype(o_ref.dtype)


def _attn_kernel(lam_ref, g_ref, q_ref, k_ref, v_ref, o_ref, *, lambda_init, n_ctx, first_tile):
    j = pl.program_id(2) + first_tile

    @pl.when(j == 0)
    def _():
        _attn_body(lam_ref, g_ref, q_ref, k_ref[0:n_ctx, :], v_ref[0:n_ctx, :], o_ref, lambda_init)

    @pl.when(j > 0)
    def _():
        _attn_body(lam_ref, g_ref, q_ref, k_ref[...], v_ref[...], o_ref, lambda_init)


def _diff_attention(q, k, v, lam_params, subln_g, lambda_init, n_ctx, first_tile):
    n_batch, t, _ = q.shape
    n_q = t // ROW_TILE - first_tile
    kern = functools.partial(_attn_kernel, lambda_init=lambda_init, n_ctx=n_ctx, first_tile=first_tile)
    return pl.pallas_call(
        kern,
        grid=(n_batch, ATT_HEADS, n_q),
        in_specs=[_const_spec(lam_params.shape),
                  _const_spec((1, ATT_VD)),
                  pl.BlockSpec((None, ROW_TILE, ATT_VD), lambda b, h, j: (b, j + first_tile, h)),
                  pl.BlockSpec((None, t, ATT_VD), lambda b, h, j: (b, 0, h)),
                  pl.BlockSpec((None, t, ATT_VD), lambda b, h, j: (b, 0, h))],
        out_specs=pl.BlockSpec((None, ROW_TILE, ATT_VD), lambda b, h, j: (b, j + first_tile, h)),
        out_shape=jax.ShapeDtypeStruct((n_batch, t, ATT_W), BF16),
        compiler_params=_cparams(("parallel", "parallel", "arbitrary")),
        name="diff_attention",
    )(lam_params, subln_g.reshape(1, ATT_VD), q, k, v)


def _halo_specs(width, tiles_per_batch):
    per = ROW_TILE // HALO
    n_blocks = None

    def prev_idx(b, j):
        return (b, jnp.maximum(j * per - 1, 0), 0)

    def next_idx(b, j):
        return (b, jnp.minimum((j + 1) * per, tiles_per_batch * per - 1), 0)

    del n_blocks
    return (pl.BlockSpec((None, HALO, width), prev_idx), pl.BlockSpec((None, HALO, width), next_idx))


def _with_halo(prev_ref, cur, next_ref, width, tiles_per_batch, j):
    left_ok = jnp.where(j > 1, 1.0, 0.0)
    right_ok = jnp.where(jnp.logical_and(j > 0, j < tiles_per_batch - 1), 1.0, 0.0)
    return jnp.concatenate([prev_ref[:, 0:width] * left_ok, cur, next_ref[:, 0:width] * right_ok], axis=0)


def _dn_prep_kernel(dn_ref, prev_ref, next_ref, ab_ref, cw_ref, alog_ref, dtb_ref,
                    u_ref, w_ref, qg_ref, qk_ref, kt_ref, gl_ref, *, tiles_per_batch):
    j = pl.program_id(1)
    width = 3 * DN_W
    xe = _with_halo(prev_ref, dn_ref[:, 0:width], next_ref, width, tiles_per_batch, j)
    half = DN_CONV // 2
    acc = None
    for tap in range(DN_CONV):
        lo = HALO - half + tap
        term = xe[lo:lo + ROW_TILE, :] * cw_ref[tap:tap + 1, :]
        acc = term if acc is None else acc + term
    qkv = _silu(acc)

    def l2n(x):
        return x * lax.rsqrt(jnp.sum(x * x, axis=-1, keepdims=True) + EPS)

    heads = range(DN_HEADS)
    qn = [l2n(qkv[:, h * DN_DH:(h + 1) * DN_DH]) * (DN_DH ** -0.5) for h in heads]
    kn = [l2n(qkv[:, DN_W + h * DN_DH:DN_W + (h + 1) * DN_DH]) for h in heads]
    vv = [qkv[:, 2 * DN_W + h * DN_DH:2 * DN_W + (h + 1) * DN_DH] for h in heads]
    kn_t = jnp.concatenate(kn, axis=-1).T

    ab = ab_ref[...]
    x = ab + dtb_ref[...]
    g = -jnp.exp(alog_ref[...]) * (jnp.maximum(x, 0.0) + jnp.log(1.0 + jnp.exp(-jnp.abs(x))))
    n_dh = 2 * DN_HEADS
    beta = _sigmoid(pltpu.roll(ab, LANES - n_dh, 1))
    r = lax.broadcasted_iota(jnp.int32, (ROW_TILE, ROW_TILE), 0)
    c = lax.broadcasted_iota(jnp.int32, (ROW_TILE, ROW_TILE), 1)
    same_chunk = (r >> 6) == (c >> 6)
    tri_f = jnp.where(jnp.logical_and(same_chunk, c <= r), 1.0, 0.0)
    tri_b = jnp.where(jnp.logical_and(same_chunk, c >= r), 1.0, 0.0)
    hp = lax.Precision.HIGHEST
    lane = lax.broadcasted_iota(jnp.int32, g.shape, 1)
    gcum = jnp.where(lane < DN_HEADS,
                     jnp.dot(tri_f, g, precision=hp, preferred_element_type=F32),
                     jnp.dot(tri_b, g, precision=hp, preferred_element_type=F32))
    gcum_t = gcum.T

    ri = lax.broadcasted_iota(jnp.int32, (DN_CHUNK, DN_CHUNK), 0)
    ci = lax.broadcasted_iota(jnp.int32, (DN_CHUNK, DN_CHUNK), 1)
    eye = jnp.where(ri == ci, 1.0, 0.0)
    same_sub = (ri >> 4) == (ci >> 4)
    n_chunks = ROW_TILE // DN_CHUNK

    for ch in range(n_chunks):
        rows = slice(ch * DN_CHUNK, (ch + 1) * DN_CHUNK)
        gram = [_dot_nt(kn[h][rows], kn[h][rows]) for h in heads]
        qkm = [_dot_nt(qn[h][rows], kn[h][rows]) for h in heads]
        m_list, rhs_list = [], []
        for d in range(2):
            incl = (ci <= ri) if d == 0 else (ci >= ri)
            strict = (ci < ri) if d == 0 else (ci > ri)
            last = DN_CHUNK - 1 if d == 0 else 0
            for h in heads:
                idx = d * DN_HEADS + h
                g_col = gcum[rows, idx:idx + 1]
                g_row = gcum_t[idx:idx + 1, rows]
                b_col = beta[rows, idx:idx + 1]
                decay = jnp.where(incl, jnp.exp(jnp.minimum(g_col - g_row, 0.0)), 0.0)
                eg = jnp.exp(g_col)
                m_list.append(jnp.where(strict, gram[h] * decay, 0.0) * b_col)
                rhs_list.append(jnp.concatenate([vv[h][rows] * b_col, kn[h][rows] * (b_col * eg)], axis=-1))
                g_last = g_row[:, last:last + 1]
                qg_ref[idx, rows, :] = (qn[h][rows] * eg).astype(qg_ref.dtype)
                qk_ref[idx, rows, :] = (qkm[h] * decay).astype(qk_ref.dtype)
                kt_ref[idx, ch] = (kn_t[h * DN_DH:(h + 1) * DN_DH, rows]
                                   * jnp.exp(g_last - g_row)).astype(kt_ref.dtype)
                gl_ref[idx, ch * HALO:(ch + 1) * HALO, :] = jnp.broadcast_to(jnp.exp(g_last), (HALO, DN_DH))
        m = jnp.stack(m_list)
        rhs = jnp.stack(rhs_list)
        n_diag = jnp.where(same_sub, m, 0.0)
        m_off = m - n_diag
        xp = -n_diag
        dinv = eye + xp
        for _ in range(3):
            xp = _bmm(xp, xp)
            dinv = dinv + _bmm(dinv, xp)
        f = _bmm(dinv, m_off)
        f2 = _bmm(f, f)
        y = _bmm(dinv, rhs)
        y = y - _bmm(f, y)
        sol = y + _bmm(f2, y)
        for idx in range(n_dh):
            u_ref[idx, rows, :] = sol[idx, :, 0:DN_DH]
            w_ref[idx, rows, :] = sol[idx, :, DN_DH:2 * DN_DH].astype(w_ref.dtype)


def _dn_prep(dn, ab, conv_w, a_log, dt_bias, tiles_per_batch):
    n_batch, t, dn_cols = dn.shape
    n_dh = 2 * DN_HEADS
    pad = jnp.zeros((1, LANES - n_dh), F32)
    alog_row = jnp.concatenate([a_log.reshape(1, n_dh).astype(F32), pad], axis=-1)
    dtb_row = jnp.concatenate([dt_bias.reshape(1, n_dh).astype(F32), pad], axis=-1)
    prev_spec, next_spec = _halo_specs(dn_cols, tiles_per_batch)
    tok = lambda b, j: (b, 0, j, 0)
    out_specs = [pl.BlockSpec((None, n_dh, ROW_TILE, DN_DH), tok)] * 4
    out_specs += [pl.BlockSpec((None, n_dh, ROW_TILE // DN_CHUNK, DN_DH, DN_CHUNK), lambda b, j: (b, 0, j, 0, 0)),
                  pl.BlockSpec((None, n_dh, HALO * ROW_TILE // DN_CHUNK, DN_DH), tok)]
    tok_shape = (n_batch, n_dh, t, DN_DH)
    out_shape = [jax.ShapeDtypeStruct(tok_shape, F32),
                 jax.ShapeDtypeStruct(tok_shape, BF16),
                 jax.ShapeDtypeStruct(tok_shape, BF16),
                 jax.ShapeDtypeStruct(tok_shape, BF16),
                 jax.ShapeDtypeStruct((n_batch, n_dh, t // DN_CHUNK, DN_DH, DN_CHUNK), BF16),
                 jax.ShapeDtypeStruct((n_batch, n_dh, HALO * t // DN_CHUNK, DN_DH), F32)]
    return pl.pallas_call(
        functools.partial(_dn_prep_kernel, tiles_per_batch=tiles_per_batch),
        grid=(n_batch, tiles_per_batch),
        in_specs=[pl.BlockSpec((None, ROW_TILE, dn_cols), lambda b, j: (b, j, 0)),
                  prev_spec, next_spec,
                  pl.BlockSpec((None, ROW_TILE, LANES), lambda b, j: (b, j, 0)),
                  _const_spec(conv_w.shape), _const_spec((1, LANES)), _const_spec((1, LANES))],
        out_specs=out_specs,
        out_shape=out_shape,
        compiler_params=_cparams(("parallel", "parallel")),
        name="dn_prep",
    )(dn, dn, dn, ab, conv_w, alog_row, dtb_row)


def _dn_scan_kernel(*refs):
    ins, (of_ref, ob_ref, s_ref) = refs[:12], refs[12:]

    @pl.when(pl.program_id(1) == 0)
    def _():
        s_ref[...] = jnp.zeros_like(s_ref)

    for d, o_ref in enumerate((of_ref, ob_ref)):
        u_ref, w_ref, qg_ref, qk_ref, kt_ref, gl_ref = ins[6 * d:6 * d + 6]
        s = s_ref[d]
        v_new = u_ref[...] - _bmm(w_ref[...], s)
        o_ref[...] = _bmm(qg_ref[...], s) + _bmm(qk_ref[...], v_new)
        s_ref[d] = s * gl_ref[:, 0:1, :] + _bmm(kt_ref[...], v_new)


def _dn_scan(prep, n_ctx):
    u, w, qg, qk, kt, gl = prep
    n_batch, _, t, _ = u.shape
    n_steps = t // DN_CHUNK
    ctx_chunks = n_ctx // DN_CHUNK

    def chunk_of(d, i):
        if d == 0:
            return i
        return jnp.where(i < ctx_chunks, ctx_chunks - 1 - i, n_steps - 1 + ctx_chunks - i)

    in_specs = []
    for d in range(2):
        tok = functools.partial(lambda b, i, d: (b, d, chunk_of(d, i), 0), d=d)
        in_specs += [pl.BlockSpec((None, DN_HEADS, DN_CHUNK, DN_DH), tok)] * 4
        in_specs += [pl.BlockSpec((None, DN_HEADS, None, DN_DH, DN_CHUNK),
                                  functools.partial(lambda b, i, d: (b, d, chunk_of(d, i), 0, 0), d=d)),
                     pl.BlockSpec((None, DN_HEADS, HALO, DN_DH), tok)]
    out_specs = [pl.BlockSpec((None, DN_HEADS, DN_CHUNK, DN_DH),
                              functools.partial(lambda b, i, d: (b, 0, chunk_of(d, i), 0), d=d))
                 for d in range(2)]
    out_shape = [jax.ShapeDtypeStruct((n_batch, DN_HEADS, t, DN_DH), F32)] * 2
    return pl.pallas_call(
        _dn_scan_kernel,
        grid=(n_batch, n_steps),
        in_specs=in_specs,
        out_specs=out_specs,
        out_shape=out_shape,
        scratch_shapes=[pltpu.VMEM((2, DN_HEADS, DN_DH, DN_DH), F32)],
        compiler_params=_cparams(("parallel", "arbitrary")),
        name="dn_scan",
    )(u, w, qg, qk, kt, gl, u, w, qg, qk, kt, gl)


def _merge_kernel(h_ref, mod_ref, att_ref, of_ref, ob_ref, z_ref, dng_ref,
                  pool_ref, pprev_ref, pnext_ref, pw_ref, ps_ref, gate_ref,
                  wba_ref, wbd_ref, wbp_ref, wo_ref, o_ref, *, tiles_per_batch, first_tile, seq, n_ctx):
    j = pl.program_id(1) + first_tile
    d = h_ref.shape[-1]
    dn_heads = []
    for h in range(DN_HEADS):
        o = of_ref[h] + ob_ref[h]
        dn_heads.append(o * lax.rsqrt(jnp.mean(o * o, axis=-1, keepdims=True) + EPS) * dng_ref[...])
    dn = jnp.concatenate(dn_heads, axis=-1) * _silu(z_ref[...])
    cur = pool_ref[...]
    xe = _with_halo(pprev_ref, cur, pnext_ref, POOL_W, tiles_per_batch, j)
    length = jnp.where(j == 0, n_ctx, seq)
    t0 = jnp.where(j == 0, 0, (j - 1) * ROW_TILE)
    tpos = t0 + lax.broadcasted_iota(jnp.int32, (ROW_TILE, POOL_W), 0)
    lane = lax.broadcasted_iota(jnp.int32, (ROW_TILE, POOL_W), 1)
    n_ext = ROW_TILE + 2 * HALO
    sums = {1: xe}
    win = 1
    while win < max(POOL_WINDOWS):
        prev = sums[win]
        rows = prev.shape[0] - win
        sums[2 * win] = prev[0:rows, :] + prev[win:win + rows, :]
        win *= 2
    del n_ext
    mean = jnp.zeros((ROW_TILE, POOL_W), F32)
    for gi, win in enumerate(POOL_WINDOWS):
        start = HALO - win // 2
        wsum = sums[win][start:start + ROW_TILE, :]
        lo = jnp.clip(tpos - win // 2, 0, length)
        hi = jnp.clip(tpos - win // 2 + win, 0, length)
        mean = jnp.where((lane >> 6) == gi, wsum / (hi - lo).astype(F32), mean)
    pooled = _dot(mean - cur, pw_ref[...]) * ps_ref[...]
    ga = gate_ref[:, 0:d].astype(F32)
    gd = gate_ref[:, d:2 * d].astype(F32)
    gp = gate_ref[:, 2 * d:3 * d].astype(F32)
    mix = (ga * jnp.dot(att_ref[...], wba_ref[...], preferred_element_type=F32)
           + gd * _dot(dn, wbd_ref[...]) + gp * _dot(pooled, wbp_ref[...]))
    y = _dot(mix, wo_ref[...])
    o_ref[...] = h_ref[...] + mod_ref[0:1, :] * y


def _merge(h, mod_gate, att, o_f, o_b, dn, dn_norm_g, pool, pool_w_bd, pool_scale, gates,
           wb_att, wb_dn, wb_pool, w_out, first_tile, n_ctx):
    n_batch, t, d = h.shape
    tpb = t // ROW_TILE
    n_rows = tpb - first_tile
    seq = t - n_ctx
    ft = first_tile
    prev_spec, next_spec = _halo_specs(POOL_W, tpb)
    shift = lambda f: (lambda b, j: f(b, j + ft))
    prev_spec = pl.BlockSpec(prev_spec.block_shape, shift(prev_spec.index_map))
    next_spec = pl.BlockSpec(next_spec.block_shape, shift(next_spec.index_map))
    row = lambda b, j: (b, j + ft, 0)

    def mod_idx(b, j):
        return (jnp.where(j + ft == 0, n_batch, b), 0, 0)

    weights = [wb_att, wb_dn, wb_pool, w_out]
    kern = functools.partial(_merge_kernel, tiles_per_batch=tpb, first_tile=ft, seq=seq, n_ctx=n_ctx)
    return pl.pallas_call(
        kern,
        grid=(n_batch, n_rows),
        in_specs=[pl.BlockSpec((None, ROW_TILE, d), row),
                  pl.BlockSpec((None, 1, d), mod_idx),
                  pl.BlockSpec((None, ROW_TILE, ATT_W), row),
                  pl.BlockSpec((None, DN_HEADS, ROW_TILE, DN_DH), lambda b, j: (b, 0, j + ft, 0)),
                  pl.BlockSpec((None, DN_HEADS, ROW_TILE, DN_DH), lambda b, j: (b, 0, j + ft, 0)),
                  pl.BlockSpec((None, ROW_TILE, DN_W), lambda b, j: (b, j + ft, 3)),
                  _const_spec((1, DN_DH)),
                  pl.BlockSpec((None, ROW_TILE, POOL_W), row),
                  prev_spec, next_spec,
                  _const_spec(pool_w_bd.shape), _const_spec((1, POOL_W)),
                  pl.BlockSpec((None, ROW_TILE, 3 * d), row)]
                 + [_const_spec(w.shape) for w in weights],
        out_specs=pl.BlockSpec((None, ROW_TILE, d), lambda b, j: (b, j, 0)),
        out_shape=jax.ShapeDtypeStruct((n_batch, n_rows * ROW_TILE, d), F32),
        compiler_params=_cparams(("parallel", "parallel")),
        name="merge",
    )(h, mod_gate, att, o_f, o_b, dn, dn_norm_g.reshape(1, DN_DH), pool, pool, pool,
      pool_w_bd, pool_scale.reshape(1, POOL_W), gates, *weights)


def _ffn_kernel(*refs, n_experts, n_fchunks, final_norm):
    if n_experts > 1:
        h_ref, mod_ref, g_ref, fg_ref, rw_ref, w1_ref, w3_ref, w2_ref, o_ref, u_ref, acc_ref, comb_ref = refs
    else:
        h_ref, mod_ref, g_ref, fg_ref, w1_ref, w3_ref, w2_ref, o_ref, u_ref, acc_ref = refs
    step = pl.program_id(2)
    e = step // n_fchunks

    @pl.when(step == 0)
    def _():
        u = _norm_mod(h_ref[...], g_ref[...], mod_ref[0:1, :], mod_ref[1:2, :])
        u_ref[...] = u.astype(BF16)
        acc_ref[...] = jnp.zeros_like(acc_ref)
        if n_experts > 1:
            logits = _dot(u, rw_ref[...])
            lane = lax.broadcasted_iota(jnp.int32, logits.shape, 1).astype(F32)
            neg = jnp.float32(-jnp.inf)
            lg = jnp.where(lane < n_experts, logits, neg)
            v1 = jnp.max(lg, axis=-1, keepdims=True)
            i1 = jnp.min(jnp.where(lg == v1, lane, float(LANES)), axis=-1, keepdims=True)
            lg2 = jnp.where(lane == i1, neg, lg)
            v2 = jnp.max(lg2, axis=-1, keepdims=True)
            i2 = jnp.min(jnp.where(lg2 == v2, lane, float(LANES)), axis=-1, keepdims=True)
            e2 = jnp.exp(v2 - v1)
            den = 1.0 + e2
            comb_ref[...] = jnp.where(lane == i1, 1.0 / den, 0.0) + jnp.where(lane == i2, e2 / den, 0.0)

    u = u_ref[...]
    hid = _silu(jnp.dot(u, w1_ref[...], preferred_element_type=F32)) * jnp.dot(
        u, w3_ref[...], preferred_element_type=F32)
    y = _dot(hid, w2_ref[...])
    if n_experts > 1:
        lane = lax.broadcasted_iota(jnp.int32, comb_ref.shape, 1)
        y = y * jnp.sum(jnp.where(lane == e, comb_ref[...], 0.0), axis=-1, keepdims=True)
    acc_ref[...] += y

    @pl.when(step == n_experts * n_fchunks - 1)
    def _():
        out = h_ref[...] + mod_ref[2:3, :] * acc_ref[...]
        if final_norm:
            out = out * lax.rsqrt(jnp.mean(out * out, axis=-1, keepdims=True) + EPS) * fg_ref[...]
        o_ref[...] = out


def _ffn(h, mod, norm_g, final_g, router_w, w1, w3, w2, has_ctx, final_norm):
    n_batch, t, d = h.shape
    n_exp, _, f = w1.shape
    n_rows = t // ROW_TILE
    row = lambda b, j, e: (b, j, 0)

    def mod_idx(b, j, e):
        return (jnp.where(jnp.logical_and(has_ctx, j == 0), n_batch, b), 0, 0)

    in_specs = [pl.BlockSpec((None, ROW_TILE, d), row),
                pl.BlockSpec((None, 3, d), mod_idx),
                _const_spec((1, d)), _const_spec((1, d))]
    args = [h, mod, norm_g.reshape(1, d), final_g.reshape(1, d)]
    scratch = [pltpu.VMEM((ROW_TILE, d), BF16), pltpu.VMEM((ROW_TILE, d), F32)]
    if n_exp > 1:
        in_specs.append(_const_spec(router_w.shape))
        args.append(router_w)
        scratch.append(pltpu.VMEM((ROW_TILE, LANES), F32))
    nfc = 1 if n_exp == 1 else 2
    fc = f // nfc
    assert fc * nfc == f and fc % LANES == 0
    in_specs += [pl.BlockSpec((None, d, fc), lambda b, j, s: (s // nfc, 0, s % nfc)),
                 pl.BlockSpec((None, d, fc), lambda b, j, s: (s // nfc, 0, s % nfc)),
                 pl.BlockSpec((None, fc, d), lambda b, j, s: (s // nfc, s % nfc, 0))]
    args += [w1, w3, w2]
    return pl.pallas_call(
        functools.partial(_ffn_kernel, n_experts=n_exp, n_fchunks=nfc, final_norm=final_norm),
        grid=(n_batch, n_rows, n_exp * nfc),
        in_specs=in_specs,
        out_specs=pl.BlockSpec((None, ROW_TILE, d), row),
        out_shape=jax.ShapeDtypeStruct((n_batch, t, d), F32),
        scratch_shapes=scratch,
        input_output_aliases={0: 0},
        compiler_params=_cparams(("parallel", "parallel", "arbitrary")),
        name="ffn_all_experts" if n_exp > 1 else "ffn_dense",
    )(*args)


MOE_BLOCK = 2048
MOE_SUB = 512
MOE_TILE = 256
MOE_FCHUNKS = 4


def _top2_routing(logits, n_experts):
    lane = lax.broadcasted_iota(jnp.int32, logits.shape, 1).astype(F32)
    neg = jnp.float32(-jnp.inf)
    lg = jnp.where(lane < n_experts, logits, neg)
    v1 = jnp.max(lg, axis=-1, keepdims=True)
    i1 = jnp.min(jnp.where(lg == v1, lane, float(LANES)), axis=-1, keepdims=True)
    lg2 = jnp.where(lane == i1, neg, lg)
    v2 = jnp.max(lg2, axis=-1, keepdims=True)
    i2 = jnp.min(jnp.where(lg2 == v2, lane, float(LANES)), axis=-1, keepdims=True)
    e2 = jnp.exp(v2 - v1)
    den = 1.0 + e2
    first = lane == i1
    second = lane == i2
    sel = jnp.where(jnp.logical_or(first, second), 1.0, 0.0)
    comb = jnp.where(first, 1.0 / den, 0.0) + jnp.where(second, e2 / den, 0.0)
    return sel, comb


def _moe_kernel(h_ref, mod_ref, g_ref, fg_ref, rw_ref, w1_ref, w3_ref, w2_ref, o_ref,
                u_ref, x_ref, y_ref, rank_ref, rank_t_ref, comb_ref, cnt_ref, *, n_experts, final_norm):
    e = pl.program_id(2)
    f = pl.program_id(3)
    n_sub = MOE_BLOCK // MOE_SUB

    @pl.when(jnp.logical_and(e == 0, f == 0))
    def _route():
        r = lax.broadcasted_iota(jnp.int32, (ROW_TILE, ROW_TILE), 0)
        c = lax.broadcasted_iota(jnp.int32, (ROW_TILE, ROW_TILE), 1)
        before = jnp.where(c < r, 1.0, 0.0).astype(BF16)
        count = jnp.zeros((1, LANES), F32)
        for t in range(MOE_BLOCK // ROW_TILE):
            rows = slice(t * ROW_TILE, (t + 1) * ROW_TILE)
            u = _norm_mod(h_ref[rows, :], g_ref[...], mod_ref[0:1, :], mod_ref[1:2, :])
            u_ref[rows, :] = u.astype(BF16)
            sel, comb = _top2_routing(_dot(u, rw_ref[...]), n_experts)
            rank = jnp.dot(before, sel.astype(BF16), preferred_element_type=F32) + count
            rank_ref[rows, :] = jnp.where(sel > 0.0, rank, -1.0)
            comb_ref[rows, :] = comb
            count = count + jnp.sum(sel, axis=0, keepdims=True)
        cnt_ref[...] = jnp.broadcast_to(count, cnt_ref.shape)
        rank_t_ref[...] = rank_ref[...].T
        o_ref[...] = jnp.zeros_like(o_ref)

    lane = lax.broadcasted_iota(jnp.int32, (1, LANES), 1)
    n_rows = jnp.sum(jnp.where(lane == e, cnt_ref[0:1, :], 0.0)).astype(jnp.int32)
    n_tiles = (n_rows + MOE_TILE - 1) // MOE_TILE
    rank_row = rank_t_ref[pl.ds(e, 1), :]

    def sub_hits(row0, s):
        rk = rank_row[:, s * MOE_SUB:(s + 1) * MOE_SUB]
        lo = row0.astype(F32)
        inside = jnp.logical_and(rk >= lo, rk < lo + MOE_TILE)
        return rk, jnp.max(jnp.where(inside, 1.0, 0.0)) > 0.0

    @pl.when(f == 0)
    def _compact():
        def body(i, carry):
            row0 = pl.multiple_of(i * MOE_TILE, MOE_TILE)
            ids = (row0 + lax.broadcasted_iota(jnp.int32, (MOE_TILE, 1), 0)).astype(F32)
            x_ref[pl.ds(row0, MOE_TILE), :] = jnp.zeros((MOE_TILE, x_ref.shape[1]), BF16)
            for s in range(n_sub):
                rk, hit = sub_hits(row0, s)

                @pl.when(hit)
                def _():
                    pick = jnp.where(rk == ids, 1.0, 0.0).astype(BF16)
                    got = jnp.dot(pick, u_ref[s * MOE_SUB:(s + 1) * MOE_SUB, :], preferred_element_type=F32)
                    x_ref[pl.ds(row0, MOE_TILE), :] += got.astype(BF16)
            return carry

        lax.fori_loop(0, n_tiles, body, 0)

    def expert_body(i, carry):
        row0 = pl.multiple_of(i * MOE_TILE, MOE_TILE)
        x = x_ref[pl.ds(row0, MOE_TILE), :]
        hid = _silu(jnp.dot(x, w1_ref[...], preferred_element_type=F32)) * jnp.dot(
            x, w3_ref[...], preferred_element_type=F32)
        y = _dot(hid, w2_ref[...])

        @pl.when(f == 0)
        def _():
            y_ref[pl.ds(row0, MOE_TILE), :] = y

        @pl.when(f > 0)
        def _():
            y_ref[pl.ds(row0, MOE_TILE), :] += y

        return carry

    lax.fori_loop(0, n_tiles, expert_body, 0)

    @pl.when(f == MOE_FCHUNKS - 1)
    def _expand():
        lanes = lax.broadcasted_iota(jnp.int32, (MOE_SUB, LANES), 1)

        def body(i, carry):
            row0 = pl.multiple_of(i * MOE_TILE, MOE_TILE)
            yt = y_ref[pl.ds(row0, MOE_TILE), :].astype(BF16)
            ids = (row0 + lax.broadcasted_iota(jnp.int32, (1, MOE_TILE), 1)).astype(F32)
            for s in range(n_sub):
                _, hit = sub_hits(row0, s)

                @pl.when(hit)
                def _():
                    rows = slice(s * MOE_SUB, (s + 1) * MOE_SUB)
                    rk_col = jnp.sum(jnp.where(lanes == e, rank_ref[rows, :], 0.0), axis=1, keepdims=True)
                    w_col = jnp.sum(jnp.where(lanes == e, comb_ref[rows, :], 0.0), axis=1, keepdims=True)
                    put = jnp.where(rk_col == ids, 1.0, 0.0).astype(BF16)
                    o_ref[rows, :] += w_col * jnp.dot(put, yt, preferred_element_type=F32)
            return carry

        lax.fori_loop(0, n_tiles, body, 0)

    @pl.when(jnp.logical_and(e == n_experts - 1, f == MOE_FCHUNKS - 1))
    def _finish():
        for t in range(MOE_BLOCK // ROW_TILE):
            rows = slice(t * ROW_TILE, (t + 1) * ROW_TILE)
            out = h_ref[rows, :] + mod_ref[2:3, :] * o_ref[rows, :]
            if final_norm:
                out = out * lax.rsqrt(jnp.mean(out * out, axis=-1, keepdims=True) + EPS) * fg_ref[...]
            o_ref[rows, :] = out


def _moe(h, mod, norm_g, final_g, router_w, w1, w3, w2, final_norm):
    n_batch, s, d = h.shape
    n_exp, _, f = w1.shape
    assert s % MOE_BLOCK == 0 and f % (MOE_FCHUNKS * LANES) == 0 and n_exp <= 8
    fc = f // MOE_FCHUNKS
    blk = lambda b, j, e, c: (b, j, 0)
    single = pl.Buffered(1)
    return pl.pallas_call(
        functools.partial(_moe_kernel, n_experts=n_exp, final_norm=final_norm),
        grid=(n_batch, s // MOE_BLOCK, n_exp, MOE_FCHUNKS),
        in_specs=[pl.BlockSpec((None, MOE_BLOCK, d), blk, pipeline_mode=single),
                  pl.BlockSpec((None, 3, d), lambda b, j, e, c: (b, 0, 0)),
                  _const_spec((1, d)), _const_spec((1, d)), _const_spec(router_w.shape),
                  pl.BlockSpec((None, d, fc), lambda b, j, e, c: (e, 0, c)),
                  pl.BlockSpec((None, d, fc), lambda b, j, e, c: (e, 0, c)),
                  pl.BlockSpec((None, fc, d), lambda b, j, e, c: (e, c, 0))],
        out_specs=pl.BlockSpec((None, MOE_BLOCK, d), blk, pipeline_mode=single),
        out_shape=jax.ShapeDtypeStruct((n_batch, s, d), F32),
        scratch_shapes=[pltpu.VMEM((MOE_BLOCK, d), BF16),
                        pltpu.VMEM((MOE_BLOCK, d), BF16),
                        pltpu.VMEM((MOE_BLOCK, d), F32),
                        pltpu.VMEM((MOE_BLOCK, LANES), F32),
                        pltpu.VMEM((LANES, MOE_BLOCK), F32),
                        pltpu.VMEM((MOE_BLOCK, LANES), F32),
                        pltpu.VMEM((HALO, LANES), F32)],
        compiler_params=_cparams(("parallel", "parallel", "arbitrary", "arbitrary")),
        name="moe_top2",
    )(h, mod, norm_g.reshape(1, d), final_g.reshape(1, d), router_w, w1, w3, w2)


def _block_diag(w):
    g, n, _ = w.shape
    out = jnp.zeros((g * n, g * n), w.dtype)
    for i in range(g):
        out = out.at[i * n:(i + 1) * n, i * n:(i + 1) * n].set(w[i])
    return out


def kernel(x, c, ctx, c_ctx, ada_w, ada_b, norm1_g, norm2_g, w_in, attn_lambda, attn_subln_g, dn_conv_w,
           dn_a_log, dn_dt_bias, dn_norm_g, pool_w, pool_scale, w_branch, w_out, ffn_w1, ffn_w3, ffn_w2,
           router_w, moe_w1, moe_w3, moe_w2, final_norm_g):
    n_batch, seq, d = x.shape
    n_ctx = ctx.shape[1]
    depth = ada_w.shape[0]
    assert n_ctx == ROW_TILE and seq % ROW_TILE == 0 and seq % GRID_W == 0 and d == 1024
    t = n_ctx + seq
    tpb = t // ROW_TILE

    h = jnp.concatenate([ctx, x], axis=1)
    c_rows = jnp.concatenate([c, c_ctx[None, :], jnp.zeros((8 - n_batch - 1, d), F32)], axis=0)
    mods = _ada_mod(c_rows, ada_w, ada_b).reshape(depth, 8, 6, d)
    rope_tabs = _rope_tables(n_ctx, seq)

    sizes = (ATT_W, ATT_W, ATT_W, DN_W, DN_W, DN_W, DN_W, 2 * DN_HEADS, 2 * DN_HEADS, POOL_W, 3 * d)
    offs = np.concatenate([[0], np.cumsum(sizes)]).tolist()

    for l in range(depth):
        last = l == depth - 1
        first_tile = 1 if last else 0
        lambda_init = 0.8 - 0.6 * math.exp(-0.3 * l)
        wl = w_in[l].astype(BF16)
        seg = lambda a, b: wl[:, offs[a]:offs[b]]
        w_ab = jnp.concatenate([seg(7, 9), jnp.zeros((d, LANES - 4 * DN_HEADS), BF16)], axis=1)
        weights = [seg(0, 1), seg(1, 2), seg(2, 3), seg(3, 7), w_ab, seg(9, 10), seg(10, 11)]
        mod = mods[l, :n_batch + 1]
        q, k, v, dn, ab, pool, gates = _in_projection(
            h.reshape(n_batch * t, d), mod[:, 0:2], norm1_g[l], rope_tabs, weights, n_batch, tpb)
        q, k, v = (a.reshape(n_batch, t, ATT_W) for a in (q, k, v))
        att = _diff_attention(q, k, v, attn_lambda[l], attn_subln_g[l], lambda_init, n_ctx, first_tile)
        dn = dn.reshape(n_batch, t, 4 * DN_W)
        prep = _dn_prep(dn, ab.reshape(n_batch, t, LANES), dn_conv_w[l], dn_a_log[l], dn_dt_bias[l], tpb)
        o_f, o_b = _dn_scan(prep, n_ctx)
        wb = w_branch[l].astype(BF16)
        h = _merge(h, mod[:, 2:3], att, o_f, o_b, dn, dn_norm_g[l], pool.reshape(n_batch, t, POOL_W),
                   _block_diag(pool_w[l]).astype(BF16), pool_scale[l], gates.reshape(n_batch, t, 3 * d),
                   wb[:ATT_W], wb[ATT_W:ATT_W + DN_W], wb[ATT_W + DN_W:], w_out[l].astype(BF16),
                   first_tile, n_ctx)
        i = l // 2
        if l % 2 == 0:
            h = _ffn(h, mod[:, 3:6], norm2_g[l], final_norm_g, None, ffn_w1[i][None].astype(BF16),
                     ffn_w3[i][None].astype(BF16), ffn_w2[i][None].astype(BF16), not last, last)
        else:
            rw = jnp.concatenate([router_w[i], jnp.zeros((d, LANES - N_EXPERTS), F32)], axis=1).astype(BF16)
            w1, w3, w2 = moe_w1[i].astype(BF16), moe_w3[i].astype(BF16), moe_w2[i].astype(BF16)
            if last:
                h = _moe(h, mod[:n_batch, 3:6], norm2_g[l], final_norm_g, rw, w1, w3, w2, True)
            else:
                h = _ffn(h, mod[:, 3:6], norm2_g[l], final_norm_g, rw, w1, w3, w2, True, False)
    return h
```

```python
import functools
import math

import numpy as np
import jax
import jax.numpy as jnp
from jax import lax
from jax.experimental import pallas as pl
from jax.experimental.pallas import tpu as pltpu

F32 = jnp.float32
BF16 = jnp.bfloat16

EPS = 1e-6
GRID_W = 64
ROPE_BASE = 10000.0
ATT_HEADS = 4
ATT_DH = 64
ATT_VD = 2 * ATT_DH
ATT_W = ATT_HEADS * ATT_VD
DN_HEADS = 4
DN_DH = 64
DN_W = DN_HEADS * DN_DH
DN_CONV = 5
DN_CHUNK = 64
DN_SUB = 16
POOL_WINDOWS = (2, 4, 8, 16)
POOL_GD = 64
POOL_W = len(POOL_WINDOWS) * POOL_GD
N_EXPERTS = 8
TOP_K = 2

ROW_TILE = 256
HALO = 8
LANES = 128
VMEM_LIMIT = 56 * 1024 * 1024


def _cparams(sem):
    return pltpu.CompilerParams(dimension_semantics=sem, vmem_limit_bytes=VMEM_LIMIT)


def _dot(a, b):
    return jnp.dot(a.astype(BF16), b.astype(BF16), preferred_element_type=F32)


def _dot_nt(a, b):
    return lax.dot_general(a.astype(BF16), b.astype(BF16), (((1,), (1,)), ((), ())),
                           preferred_element_type=F32)


def _bmm(a, b):
    return lax.dot_general(a.astype(BF16), b.astype(BF16), (((2,), (1,)), ((0,), (0,))),
                           preferred_element_type=F32)


def _bmm_nt(a, b):
    return lax.dot_general(a.astype(BF16), b.astype(BF16), (((2,), (2,)), ((0,), (0,))),
                           preferred_element_type=F32)


def _sigmoid(x):
    return 1.0 / (1.0 + jnp.exp(-x))


def _silu(x):
    return x * _sigmoid(x)


def _const_spec(shape):
    nd = len(shape)
    return pl.BlockSpec(shape, lambda *_: (0,) * nd)


def _ada_kernel(c_ref, w_ref, b_ref, o_ref):
    o_ref[...] = _dot(_silu(c_ref[...]), w_ref[...]) + b_ref[...]


def _ada_mod(c_rows, ada_w, ada_b):
    depth, d, six_d = ada_w.shape
    n = six_d // d
    rows = c_rows.shape[0]
    return pl.pallas_call(
        _ada_kernel,
        grid=(depth, n),
        in_specs=[pl.BlockSpec((rows, d), lambda l, j: (0, 0)),
                  pl.BlockSpec((None, d, d), lambda l, j: (l, 0, j)),
                  pl.BlockSpec((None, 1, d), lambda l, j: (l, 0, j))],
        out_specs=pl.BlockSpec((None, rows, d), lambda l, j: (l, 0, j)),
        out_shape=jax.ShapeDtypeStruct((depth, rows, six_d), F32),
        compiler_params=_cparams(("parallel", "parallel")),
        name="ada_mod",
    )(c_rows, ada_w, ada_b.reshape(depth, 1, six_d))


def _norm_mod(x, g, shift, scale):
    y = x * lax.rsqrt(jnp.mean(x * x, axis=-1, keepdims=True) + EPS) * g
    return y * (1.0 + scale) + shift


def _inproj_kernel(h_ref, mod_ref, g_ref, cos_ref, su_ref, sd_ref,
                   wq_ref, wk_ref, wv_ref, wdn_ref, wab_ref, wpool_ref, wg_ref,
                   q_ref, k_ref, v_ref, dn_ref, ab_ref, pool_ref, gate_ref):
    u = _norm_mod(h_ref[...], g_ref[...], mod_ref[0:1, :], mod_ref[1:2, :]).astype(BF16)
    reps = ATT_W // LANES
    cos = jnp.tile(cos_ref[...], (1, reps))
    s_up = jnp.tile(su_ref[...], (1, reps))
    s_dn = jnp.tile(sd_ref[...], (1, reps))
    quarter = ATT_DH // 4

    def rope(t):
        return (t * cos + pltpu.roll(t, ATT_W - quarter, 1) * s_up + pltpu.roll(t, quarter, 1) * s_dn)

    q = rope(jnp.dot(u, wq_ref[...], preferred_element_type=F32))
    q_ref[...] = (q * (ATT_DH ** -0.5 * LOG2E)).astype(q_ref.dtype)
    k_ref[...] = rope(jnp.dot(u, wk_ref[...], preferred_element_type=F32)).astype(k_ref.dtype)
    v_ref[...] = jnp.dot(u, wv_ref[...], preferred_element_type=F32).astype(v_ref.dtype)
    dn_ref[...] = jnp.dot(u, wdn_ref[...], preferred_element_type=F32)
    ab_ref[...] = jnp.dot(u, wab_ref[...], preferred_element_type=F32)
    pool_ref[...] = jnp.dot(u, wpool_ref[...], preferred_element_type=F32)
    gate_ref[...] = _sigmoid(jnp.dot(u, wg_ref[...], preferred_element_type=F32)).astype(gate_ref.dtype)


def _in_projection(h, mod, norm_g, rope_tabs, weights, n_batch, tiles_per_batch):
    rows, d = h.shape
    n_tiles = rows // ROW_TILE
    tpb = tiles_per_batch
    widths = [w.shape[1] for w in weights]
    out_dtypes = [BF16, BF16, BF16, F32, F32, F32, BF16]

    def mod_idx(i):
        return (jnp.where(i % tpb == 0, n_batch, i // tpb), 0, 0)

    in_specs = [pl.BlockSpec((ROW_TILE, d), lambda i: (i, 0)),
                pl.BlockSpec((None, 2, d), mod_idx),
                _const_spec((1, d))]
    in_specs += [pl.BlockSpec((ROW_TILE, LANES), lambda i: (i % tpb, 0)) for _ in range(3)]
    in_specs += [pl.BlockSpec(w.shape, lambda i: (0, 0), pipeline_mode=pl.Buffered(1)) for w in weights]
    out_specs = [pl.BlockSpec((ROW_TILE, w), lambda i: (i, 0)) for w in widths]
    out_shape = [jax.ShapeDtypeStruct((rows, w), dt) for w, dt in zip(widths, out_dtypes)]
    return pl.pallas_call(
        _inproj_kernel,
        grid=(n_tiles,),
        in_specs=in_specs,
        out_specs=out_specs,
        out_shape=out_shape,
        compiler_params=_cparams(("parallel",)),
        name="in_projection",
    )(h, mod, norm_g.reshape(1, d), *rope_tabs, *weights)


def _rope_tables(n_ctx, seq):
    rows = seq // GRID_W
    row = jnp.repeat(jnp.arange(rows), GRID_W).astype(F32)
    col = jnp.tile(jnp.arange(GRID_W), rows).astype(F32)
    n_freq = ATT_DH // 4
    inv = ROPE_BASE ** (-jnp.arange(n_freq, dtype=F32) / n_freq)
    ar = row[:, None] * inv
    ac = col[:, None] * inv
    cr, sr, cc, sc = jnp.cos(ar), jnp.sin(ar), jnp.cos(ac), jnp.sin(ac)
    z = jnp.zeros_like(sr)
    cos = jnp.concatenate([cr, cr, cc, cc], axis=-1)
    s_up = jnp.concatenate([-sr, z, -sc, z], axis=-1)
    s_dn = jnp.concatenate([z, sr, z, sc], axis=-1)

    def full(t, ctx_val):
        t = jnp.tile(t, (1, LANES // ATT_DH))
        return jnp.concatenate([jnp.full((n_ctx, LANES), ctx_val, F32), t], axis=0)

    return full(cos, 1.0), full(s_up, 0.0), full(s_dn, 0.0)


ATT_KEY_TILE = 512
LOG2E = math.log2(math.e)


ATT_STREAMS = 2


class _AttnStream:
    def __init__(self, q, e_ref, k_ref, v_ref):
        lane = lax.broadcasted_iota(jnp.int32, q.shape, 1)
        zero = jnp.zeros_like(q)
        self.qm = (jnp.where(lane < ATT_DH, q, zero), jnp.where(lane >= ATT_DH, q, zero))
        self.e_ref, self.k_ref, self.v_ref = e_ref, k_ref, v_ref
        tq = q.shape[0]
        self.m_part = [jnp.full((tq, LANES), -jnp.inf, F32)] * 2
        self.l_part = [jnp.zeros((tq, LANES), F32)] * 2
        self.acc = jnp.zeros((tq, ATT_VD), F32)

    @staticmethod
    def _lane_fold(x, op):
        out = x[:, 0:LANES]
        for c in range(1, x.shape[1] // LANES):
            out = op(out, x[:, c * LANES:(c + 1) * LANES])
        return out

    def scores(self, k0, w):
        kt = self.k_ref[k0:k0 + w, :]
        for m in range(2):
            s = _dot_nt(self.qm[m], kt)
            self.e_ref[m, :, k0:k0 + w] = s
            self.m_part[m] = jnp.maximum(self.m_part[m], self._lane_fold(s, jnp.maximum))

    def end_scores(self):
        self.mx = [jnp.max(p, axis=-1, keepdims=True) for p in self.m_part]

    def exps(self, k0, w):
        for m in range(2):
            e = jnp.exp2(self.e_ref[m, :, k0:k0 + w] - self.mx[m])
            self.e_ref[m, :, k0:k0 + w] = e
            self.l_part[m] = self.l_part[m] + self._lane_fold(e, jnp.add)

    def end_exps(self, lam):
        self.inv0 = 1.0 / jnp.sum(self.l_part[0], axis=-1, keepdims=True)
        self.inv1 = lam / jnp.sum(self.l_part[1], axis=-1, keepdims=True)

    def values(self, k0, w):
        a = self.e_ref[0, :, k0:k0 + w] * self.inv0 - self.e_ref[1, :, k0:k0 + w] * self.inv1
        self.acc = self.acc + jnp.dot(a.astype(BF16), self.v_ref[k0:k0 + w, :], preferred_element_type=F32)

    def result(self, g, lambda_init):
        y = self.acc * lax.rsqrt(jnp.mean(self.acc * self.acc, axis=-1, keepdims=True) + EPS) * g
        return y * (1.0 - lambda_init)


def _attn_run(streams, tiles, lam, g, lambda_init, outs):
    passes = ("scores", "exps", "values")
    for step in range(len(passes) + len(streams) - 1):
        active = [(st, passes[step - i]) for i, st in enumerate(streams) if 0 <= step - i < len(passes)]
        for k0, w in tiles:
            for st, name in active:
                getattr(st, name)(k0, w)
        for st, name in active:
            if name == "scores":
                st.end_scores()
            elif name == "exps":
                st.end_exps(lam)
    for st, (o_ref, rows) in zip(streams, outs):
        o_ref[rows, :] = st.result(g, lambda_init).astype(o_ref.dtype)


def _attn_kernel(*refs, lambda_init, n_ctx, ctx_out):
    if ctx_out:
        lam_ref, g_ref, q_ref, qc_ref, k_ref, v_ref, o_ref, oc_ref, e_ref = refs
    else:
        lam_ref, g_ref, q_ref, k_ref, v_ref, o_ref, e_ref = refs
    n_keys = k_ref.shape[0]
    ctx_tiles = ((0, n_ctx),)
    all_tiles = ctx_tiles + tuple((k0, ATT_KEY_TILE) for k0 in range(n_ctx, n_keys, ATT_KEY_TILE))
    lp = lam_ref[...]
    lam = (jnp.exp(jnp.sum(lp[0:1] * lp[1:2], keepdims=True))
           - jnp.exp(jnp.sum(lp[2:3] * lp[3:4], keepdims=True)) + lambda_init)
    g = g_ref[...]

    if ctx_out:
        @pl.when(pl.program_id(2) == 0)
        def _():
            st = _AttnStream(qc_ref[...], e_ref.at[0], k_ref, v_ref)
            _attn_run([st], ctx_tiles, lam, g, lambda_init, [(oc_ref, slice(None))])

    streams, outs = [], []
    for i in range(ATT_STREAMS):
        rows = slice(i * ROW_TILE, (i + 1) * ROW_TILE)
        streams.append(_AttnStream(q_ref[0, rows, :], e_ref.at[i], k_ref, v_ref))
        outs.append((o_ref, rows))
    _attn_run(streams, all_tiles, lam, g, lambda_init, outs)


def _diff_attention(q, k, v, lam_params, subln_g, lambda_init, n_ctx, ctx_out):
    n_batch, t, _ = q.shape
    seq = t - n_ctx
    tq = ATT_STREAMS * ROW_TILE
    assert seq % ATT_KEY_TILE == 0 and seq % tq == 0
    kv_spec = pl.BlockSpec((None, t, ATT_VD), lambda b, h, j: (b, 0, h))
    ctx_spec = pl.BlockSpec((None, n_ctx, ATT_VD), lambda b, h, j: (b, 0, h))
    in_specs = [_const_spec(lam_params.shape), _const_spec((1, ATT_VD)),
                pl.BlockSpec((pl.Element(1), pl.Element(tq), pl.Element(ATT_VD)),
                             lambda b, h, j: (b, pl.multiple_of(n_ctx + j * tq, ROW_TILE),
                                              pl.multiple_of(h * ATT_VD, LANES)))]
    out_specs = [pl.BlockSpec((None, tq, ATT_VD), lambda b, h, j: (b, j, h))]
    out_shape = [jax.ShapeDtypeStruct((n_batch, seq, ATT_W), BF16)]
    args = [lam_params, subln_g.reshape(1, ATT_VD), q]
    if ctx_out:
        in_specs.append(ctx_spec)
        args.append(q)
        out_specs.append(ctx_spec)
        out_shape.append(jax.ShapeDtypeStruct((n_batch, n_ctx, ATT_W), BF16))
    kern = functools.partial(_attn_kernel, lambda_init=lambda_init, n_ctx=n_ctx, ctx_out=ctx_out)
    outs = pl.pallas_call(
        kern,
        grid=(n_batch, ATT_HEADS, seq // tq),
        in_specs=in_specs + [kv_spec, kv_spec],
        out_specs=out_specs,
        out_shape=out_shape,
        scratch_shapes=[pltpu.VMEM((ATT_STREAMS, 2, ROW_TILE, t), F32)],
        compiler_params=_cparams(("parallel", "parallel", "arbitrary")),
        name="diff_attention",
    )(*args, k, v)
    return outs[0], (outs[1] if ctx_out else None)


def _halo_specs(width, tiles_per_batch):
    per = ROW_TILE // HALO
    n_blocks = None

    def prev_idx(b, j):
        return (b, jnp.maximum(j * per - 1, 0), 0)

    def next_idx(b, j):
        return (b, jnp.minimum((j + 1) * per, tiles_per_batch * per - 1), 0)

    del n_blocks
    return (pl.BlockSpec((None, HALO, width), prev_idx), pl.BlockSpec((None, HALO, width), next_idx))


def _with_halo(prev_ref, cur, next_ref, width, tiles_per_batch, j):
    left_ok = jnp.where(j > 1, 1.0, 0.0)
    right_ok = jnp.where(jnp.logical_and(j > 0, j < tiles_per_batch - 1), 1.0, 0.0)
    return jnp.concatenate([prev_ref[:, 0:width] * left_ok, cur, next_ref[:, 0:width] * right_ok], axis=0)


def _dn_prep_kernel(dn_ref, prev_ref, next_ref, ab_ref, cw_ref, alog_ref, dtb_ref,
                    u_ref, w_ref, qg_ref, qk_ref, kt_ref, gl_ref, *, tiles_per_batch):
    j = pl.program_id(1)
    width = 3 * DN_W
    xe = _with_halo(prev_ref, dn_ref[:, 0:width], next_ref, width, tiles_per_batch, j)
    half = DN_CONV // 2
    acc = None
    for tap in range(DN_CONV):
        lo = HALO - half + tap
        term = xe[lo:lo + ROW_TILE, :] * cw_ref[tap:tap + 1, :]
        acc = term if acc is None else acc + term
    qkv = _silu(acc)

    def l2n(x):
        return x * lax.rsqrt(jnp.sum(x * x, axis=-1, keepdims=True) + EPS)

    heads = range(DN_HEADS)
    qn = [l2n(qkv[:, h * DN_DH:(h + 1) * DN_DH]) * (DN_DH ** -0.5) for h in heads]
    kn = [l2n(qkv[:, DN_W + h * DN_DH:DN_W + (h + 1) * DN_DH]) for h in heads]
    vv = [qkv[:, 2 * DN_W + h * DN_DH:2 * DN_W + (h + 1) * DN_DH] for h in heads]
    kn_t = jnp.concatenate(kn, axis=-1).T

    ab = ab_ref[...]
    x = ab + dtb_ref[...]
    g = -jnp.exp(alog_ref[...]) * (jnp.maximum(x, 0.0) + jnp.log(1.0 + jnp.exp(-jnp.abs(x))))
    n_dh = 2 * DN_HEADS
    beta = _sigmoid(pltpu.roll(ab, LANES - n_dh, 1))
    r = lax.broadcasted_iota(jnp.int32, (ROW_TILE, ROW_TILE), 0)
    c = lax.broadcasted_iota(jnp.int32, (ROW_TILE, ROW_TILE), 1)
    same_chunk = (r >> 6) == (c >> 6)
    tri_f = jnp.where(jnp.logical_and(same_chunk, c <= r), 1.0, 0.0)
    tri_b = jnp.where(jnp.logical_and(same_chunk, c >= r), 1.0, 0.0)
    hp = lax.Precision.HIGHEST
    lane = lax.broadcasted_iota(jnp.int32, g.shape, 1)
    gcum = jnp.where(lane < DN_HEADS,
                     jnp.dot(tri_f, g, precision=hp, preferred_element_type=F32),
                     jnp.dot(tri_b, g, precision=hp, preferred_element_type=F32))
    gcum_t = gcum.T

    ri = lax.broadcasted_iota(jnp.int32, (DN_CHUNK, DN_CHUNK), 0)
    ci = lax.broadcasted_iota(jnp.int32, (DN_CHUNK, DN_CHUNK), 1)
    eye = jnp.where(ri == ci, 1.0, 0.0)
    same_sub = (ri >> 4) == (ci >> 4)
    n_chunks = ROW_TILE // DN_CHUNK

    for ch in range(n_chunks):
        rows = slice(ch * DN_CHUNK, (ch + 1) * DN_CHUNK)
        gram = [_dot_nt(kn[h][rows], kn[h][rows]) for h in heads]
        qkm = [_dot_nt(qn[h][rows], kn[h][rows]) for h in heads]
        m_list, rhs_list = [], []
        for d in range(2):
            incl = (ci <= ri) if d == 0 else (ci >= ri)
            strict = (ci < ri) if d == 0 else (ci > ri)
            last = DN_CHUNK - 1 if d == 0 else 0
            for h in heads:
                idx = d * DN_HEADS + h
                g_col = gcum[rows, idx:idx + 1]
                g_row = gcum_t[idx:idx + 1, rows]
                b_col = beta[rows, idx:idx + 1]
                decay = jnp.where(incl, jnp.exp(jnp.minimum(g_col - g_row, 0.0)), 0.0)
                eg = jnp.exp(g_col)
                m_list.append(jnp.where(strict, gram[h] * decay, 0.0) * b_col)
                rhs_list.append(jnp.concatenate([vv[h][rows] * b_col, kn[h][rows] * (b_col * eg)], axis=-1))
                g_last = g_row[:, last:last + 1]
                qg_ref[idx, rows, :] = (qn[h][rows] * eg).astype(qg_ref.dtype)
                qk_ref[idx, rows, :] = (qkm[h] * decay).astype(qk_ref.dtype)
                kt_ref[idx, ch] = (kn_t[h * DN_DH:(h + 1) * DN_DH, rows]
                                   * jnp.exp(g_last - g_row)).astype(kt_ref.dtype)
                gl_ref[idx, ch * HALO:(ch + 1) * HALO, :] = jnp.broadcast_to(jnp.exp(g_last), (HALO, DN_DH))
        m = jnp.stack(m_list)
        rhs = jnp.stack(rhs_list)
        n_diag = jnp.where(same_sub, m, 0.0)
        m_off = m - n_diag
        xp = -n_diag
        dinv = eye + xp
        for _ in range(3):
            xp = _bmm(xp, xp)
            dinv = dinv + _bmm(dinv, xp)
        f = _bmm(dinv, m_off)
        f2 = _bmm(f, f)
        y = _bmm(dinv, rhs)
        y = y - _bmm(f, y)
        sol = y + _bmm(f2, y)
        for idx in range(n_dh):
            u_ref[idx, rows, :] = sol[idx, :, 0:DN_DH]
            w_ref[idx, rows, :] = sol[idx, :, DN_DH:2 * DN_DH].astype(w_ref.dtype)


def _dn_prep(dn, ab, conv_w, a_log, dt_bias, tiles_per_batch):
    n_batch, t, dn_cols = dn.shape
    n_dh = 2 * DN_HEADS
    pad = jnp.zeros((1, LANES - n_dh), F32)
    alog_row = jnp.concatenate([a_log.reshape(1, n_dh).astype(F32), pad], axis=-1)
    dtb_row = jnp.concatenate([dt_bias.reshape(1, n_dh).astype(F32), pad], axis=-1)
    prev_spec, next_spec = _halo_specs(dn_cols, tiles_per_batch)
    tok = lambda b, j: (b, 0, j, 0)
    out_specs = [pl.BlockSpec((None, n_dh, ROW_TILE, DN_DH), tok)] * 4
    out_specs += [pl.BlockSpec((None, n_dh, ROW_TILE // DN_CHUNK, DN_DH, DN_CHUNK), lambda b, j: (b, 0, j, 0, 0)),
                  pl.BlockSpec((None, n_dh, HALO * ROW_TILE // DN_CHUNK, DN_DH), tok)]
    tok_shape = (n_batch, n_dh, t, DN_DH)
    out_shape = [jax.ShapeDtypeStruct(tok_shape, F32),
                 jax.ShapeDtypeStruct(tok_shape, BF16),
                 jax.ShapeDtypeStruct(tok_shape, BF16),
                 jax.ShapeDtypeStruct(tok_shape, BF16),
                 jax.ShapeDtypeStruct((n_batch, n_dh, t // DN_CHUNK, DN_DH, DN_CHUNK), BF16),
                 jax.ShapeDtypeStruct((n_batch, n_dh, HALO * t // DN_CHUNK, DN_DH), F32)]
    return pl.pallas_call(
        functools.partial(_dn_prep_kernel, tiles_per_batch=tiles_per_batch),
        grid=(n_batch, tiles_per_batch),
        in_specs=[pl.BlockSpec((None, ROW_TILE, dn_cols), lambda b, j: (b, j, 0)),
                  prev_spec, next_spec,
                  pl.BlockSpec((None, ROW_TILE, LANES), lambda b, j: (b, j, 0)),
                  _const_spec(conv_w.shape), _const_spec((1, LANES)), _const_spec((1, LANES))],
        out_specs=out_specs,
        out_shape=out_shape,
        compiler_params=_cparams(("parallel", "parallel")),
        name="dn_prep",
    )(dn, dn, dn, ab, conv_w, alog_row, dtb_row)


def _dn_scan_kernel(*refs):
    ins, (of_ref, ob_ref, s_ref) = refs[:12], refs[12:]

    @pl.when(pl.program_id(0) == 0)
    def _():
        s_ref[...] = jnp.zeros_like(s_ref)

    for d, o_ref in enumerate((of_ref, ob_ref)):
        u_ref, w_ref, qg_ref, qk_ref, kt_ref, gl_ref = ins[6 * d:6 * d + 6]
        flat = lambda r: r[...].reshape((-1,) + r.shape[2:])
        s = s_ref[d]
        v_new = flat(u_ref) - _bmm(flat(w_ref), s)
        o = _bmm(flat(qg_ref), s) + _bmm(flat(qk_ref), v_new)
        o_ref[...] = o.reshape(o_ref.shape)
        s_ref[d] = s * flat(gl_ref)[:, 0:1, :] + _bmm(flat(kt_ref), v_new)


def _dn_scan(prep, n_ctx):
    u, w, qg, qk, kt, gl = prep
    n_batch, _, t, _ = u.shape
    n_steps = t // DN_CHUNK
    ctx_chunks = n_ctx // DN_CHUNK

    def chunk_of(d, i):
        if d == 0:
            return i
        return jnp.where(i < ctx_chunks, ctx_chunks - 1 - i, n_steps - 1 + ctx_chunks - i)

    in_specs = []
    for d in range(2):
        tok = functools.partial(lambda i, d: (0, d, chunk_of(d, i), 0), d=d)
        in_specs += [pl.BlockSpec((n_batch, DN_HEADS, DN_CHUNK, DN_DH), tok)] * 4
        in_specs += [pl.BlockSpec((n_batch, DN_HEADS, None, DN_DH, DN_CHUNK),
                                  functools.partial(lambda i, d: (0, d, chunk_of(d, i), 0, 0), d=d)),
                     pl.BlockSpec((n_batch, DN_HEADS, HALO, DN_DH), tok)]
    out_specs = [pl.BlockSpec((n_batch, DN_HEADS, DN_CHUNK, DN_DH),
                              functools.partial(lambda i, d: (0, 0, chunk_of(d, i), 0), d=d))
                 for d in range(2)]
    out_shape = [jax.ShapeDtypeStruct((n_batch, DN_HEADS, t, DN_DH), F32)] * 2
    return pl.pallas_call(
        _dn_scan_kernel,
        grid=(n_steps,),
        in_specs=in_specs,
        out_specs=out_specs,
        out_shape=out_shape,
        scratch_shapes=[pltpu.VMEM((2, n_batch * DN_HEADS, DN_DH, DN_DH), F32)],
        compiler_params=_cparams(("arbitrary",)),
        name="dn_scan",
    )(u, w, qg, qk, kt, gl, u, w, qg, qk, kt, gl)


def _merge_kernel(h_ref, mod_ref, att_ref, attc_ref, of_ref, ob_ref, z_ref, dng_ref,
                  pool_ref, pprev_ref, pnext_ref, pw_ref, ps_ref, gate_ref,
                  wba_ref, wbd_ref, wbp_ref, wo_ref, o_ref, *, tiles_per_batch, first_tile, seq, n_ctx):
    j = pl.program_id(1) + first_tile
    d = h_ref.shape[-1]
    dn_heads = []
    for h in range(DN_HEADS):
        o = of_ref[h] + ob_ref[h]
        dn_heads.append(o * lax.rsqrt(jnp.mean(o * o, axis=-1, keepdims=True) + EPS) * dng_ref[...])
    dn = jnp.concatenate(dn_heads, axis=-1) * _silu(z_ref[...])
    cur = pool_ref[...]
    xe = _with_halo(pprev_ref, cur, pnext_ref, POOL_W, tiles_per_batch, j)
    length = jnp.where(j == 0, n_ctx, seq)
    t0 = jnp.where(j == 0, 0, (j - 1) * ROW_TILE)
    tpos = t0 + lax.broadcasted_iota(jnp.int32, (ROW_TILE, POOL_W), 0)
    lane = lax.broadcasted_iota(jnp.int32, (ROW_TILE, POOL_W), 1)
    n_ext = ROW_TILE + 2 * HALO
    sums = {1: xe}
    win = 1
    while win < max(POOL_WINDOWS):
        prev = sums[win]
        rows = prev.shape[0] - win
        sums[2 * win] = prev[0:rows, :] + prev[win:win + rows, :]
        win *= 2
    del n_ext
    mean = jnp.zeros((ROW_TILE, POOL_W), F32)
    for gi, win in enumerate(POOL_WINDOWS):
        start = HALO - win // 2
        wsum = sums[win][start:start + ROW_TILE, :]
        lo = jnp.clip(tpos - win // 2, 0, length)
        hi = jnp.clip(tpos - win // 2 + win, 0, length)
        mean = jnp.where((lane >> 6) == gi, wsum / (hi - lo).astype(F32), mean)
    pooled = _dot(mean - cur, pw_ref[...]) * ps_ref[...]
    ga = gate_ref[:, 0:d].astype(F32)
    gd = gate_ref[:, d:2 * d].astype(F32)
    gp = gate_ref[:, 2 * d:3 * d].astype(F32)
    att = jnp.where(j == 0, attc_ref[...], att_ref[...])
    mix = (ga * jnp.dot(att, wba_ref[...], preferred_element_type=F32)
           + gd * _dot(dn, wbd_ref[...]) + gp * _dot(pooled, wbp_ref[...]))
    y = _dot(mix, wo_ref[...])
    o_ref[...] = h_ref[...] + mod_ref[0:1, :] * y


def _merge(h, mod_gate, att, att_ctx, o_f, o_b, dn, dn_norm_g, pool, pool_w_bd, pool_scale, gates,
           wb_att, wb_dn, wb_pool, w_out, first_tile, n_ctx):
    n_batch, t, d = h.shape
    tpb = t // ROW_TILE
    n_rows = tpb - first_tile
    seq = t - n_ctx
    ft = first_tile
    prev_spec, next_spec = _halo_specs(POOL_W, tpb)
    shift = lambda f: (lambda b, j: f(b, j + ft))
    prev_spec = pl.BlockSpec(prev_spec.block_shape, shift(prev_spec.index_map))
    next_spec = pl.BlockSpec(next_spec.block_shape, shift(next_spec.index_map))
    row = lambda b, j: (b, j + ft, 0)

    def mod_idx(b, j):
        return (jnp.where(j + ft == 0, n_batch, b), 0, 0)

    weights = [wb_att, wb_dn, wb_pool, w_out]
    kern = functools.partial(_merge_kernel, tiles_per_batch=tpb, first_tile=ft, seq=seq, n_ctx=n_ctx)
    return pl.pallas_call(
        kern,
        grid=(n_batch, n_rows),
        in_specs=[pl.BlockSpec((None, ROW_TILE, d), row),
                  pl.BlockSpec((None, 1, d), mod_idx),
                  pl.BlockSpec((None, ROW_TILE, ATT_W), lambda b, j: (b, jnp.maximum(j + ft - 1, 0), 0)),
                  pl.BlockSpec((None, ROW_TILE, ATT_W), lambda b, j: (b, 0, 0)),
                  pl.BlockSpec((None, DN_HEADS, ROW_TILE, DN_DH), lambda b, j: (b, 0, j + ft, 0)),
                  pl.BlockSpec((None, DN_HEADS, ROW_TILE, DN_DH), lambda b, j: (b, 0, j + ft, 0)),
                  pl.BlockSpec((None, ROW_TILE, DN_W), lambda b, j: (b, j + ft, 3)),
                  _const_spec((1, DN_DH)),
                  pl.BlockSpec((None, ROW_TILE, POOL_W), row),
                  prev_spec, next_spec,
                  _const_spec(pool_w_bd.shape), _const_spec((1, POOL_W)),
                  pl.BlockSpec((None, ROW_TILE, 3 * d), row)]
                 + [_const_spec(w.shape) for w in weights],
        out_specs=pl.BlockSpec((None, ROW_TILE, d), lambda b, j: (b, j, 0)),
        out_shape=jax.ShapeDtypeStruct((n_batch, n_rows * ROW_TILE, d), F32),
        compiler_params=_cparams(("parallel", "parallel")),
        name="merge",
    )(h, mod_gate, att, att if att_ctx is None else att_ctx, o_f, o_b, dn, dn_norm_g.reshape(1, DN_DH),
      pool, pool, pool,
      pool_w_bd, pool_scale.reshape(1, POOL_W), gates, *weights)


def _ffn_kernel(*refs, n_experts, n_fchunks, final_norm):
    if n_experts > 1:
        h_ref, mod_ref, g_ref, fg_ref, rw_ref, w1_ref, w3_ref, w2_ref, o_ref, u_ref, acc_ref, comb_ref = refs
    else:
        h_ref, mod_ref, g_ref, fg_ref, w1_ref, w3_ref, w2_ref, o_ref, u_ref, acc_ref = refs
    step = pl.program_id(2)
    e = step // n_fchunks

    @pl.when(step == 0)
    def _():
        u = _norm_mod(h_ref[...], g_ref[...], mod_ref[0:1, :], mod_ref[1:2, :])
        u_ref[...] = u.astype(BF16)
        acc_ref[...] = jnp.zeros_like(acc_ref)
        if n_experts > 1:
            logits = _dot(u, rw_ref[...])
            lane = lax.broadcasted_iota(jnp.int32, logits.shape, 1).astype(F32)
            neg = jnp.float32(-jnp.inf)
            lg = jnp.where(lane < n_experts, logits, neg)
            v1 = jnp.max(lg, axis=-1, keepdims=True)
            i1 = jnp.min(jnp.where(lg == v1, lane, float(LANES)), axis=-1, keepdims=True)
            lg2 = jnp.where(lane == i1, neg, lg)
            v2 = jnp.max(lg2, axis=-1, keepdims=True)
            i2 = jnp.min(jnp.where(lg2 == v2, lane, float(LANES)), axis=-1, keepdims=True)
            e2 = jnp.exp(v2 - v1)
            den = 1.0 + e2
            comb_ref[...] = jnp.where(lane == i1, 1.0 / den, 0.0) + jnp.where(lane == i2, e2 / den, 0.0)

    u = u_ref[...]
    hid = _silu(jnp.dot(u, w1_ref[...], preferred_element_type=F32)) * jnp.dot(
        u, w3_ref[...], preferred_element_type=F32)
    y = _dot(hid, w2_ref[...])
    if n_experts > 1:
        lane = lax.broadcasted_iota(jnp.int32, comb_ref.shape, 1)
        y = y * jnp.sum(jnp.where(lane == e, comb_ref[...], 0.0), axis=-1, keepdims=True)
    acc_ref[...] += y

    @pl.when(step == n_experts * n_fchunks - 1)
    def _():
        out = h_ref[...] + mod_ref[2:3, :] * acc_ref[...]
        if final_norm:
            out = out * lax.rsqrt(jnp.mean(out * out, axis=-1, keepdims=True) + EPS) * fg_ref[...]
        o_ref[...] = out


def _ffn(h, mod, norm_g, final_g, router_w, w1, w3, w2, has_ctx, final_norm):
    n_batch, t, d = h.shape
    n_exp, _, f = w1.shape
    n_rows = t // ROW_TILE
    row = lambda b, j, e: (b, j, 0)

    def mod_idx(b, j, e):
        return (jnp.where(jnp.logical_and(has_ctx, j == 0), n_batch, b), 0, 0)

    in_specs = [pl.BlockSpec((None, ROW_TILE, d), row),
                pl.BlockSpec((None, 3, d), mod_idx),
                _const_spec((1, d)), _const_spec((1, d))]
    args = [h, mod, norm_g.reshape(1, d), final_g.reshape(1, d)]
    scratch = [pltpu.VMEM((ROW_TILE, d), BF16), pltpu.VMEM((ROW_TILE, d), F32)]
    if n_exp > 1:
        in_specs.append(_const_spec(router_w.shape))
        args.append(router_w)
        scratch.append(pltpu.VMEM((ROW_TILE, LANES), F32))
    nfc = 1 if n_exp == 1 else 2
    fc = f // nfc
    assert fc * nfc == f and fc % LANES == 0
    in_specs += [pl.BlockSpec((None, d, fc), lambda b, j, s: (s // nfc, 0, s % nfc)),
                 pl.BlockSpec((None, d, fc), lambda b, j, s: (s // nfc, 0, s % nfc)),
                 pl.BlockSpec((None, fc, d), lambda b, j, s: (s // nfc, s % nfc, 0))]
    args += [w1, w3, w2]
    return pl.pallas_call(
        functools.partial(_ffn_kernel, n_experts=n_exp, n_fchunks=nfc, final_norm=final_norm),
        grid=(n_batch, n_rows, n_exp * nfc),
        in_specs=in_specs,
        out_specs=pl.BlockSpec((None, ROW_TILE, d), row),
        out_shape=jax.ShapeDtypeStruct((n_batch, t, d), F32),
        scratch_shapes=scratch,
        input_output_aliases={0: 0},
        compiler_params=_cparams(("parallel", "parallel", "arbitrary")),
        name="ffn_all_experts" if n_exp > 1 else "ffn_dense",
    )(*args)


MOE_BLOCK = 2048
MOE_SUB = 512
MOE_TILE = 256
MOE_FCHUNKS = 4


def _top2_routing(logits, n_experts):
    lane = lax.broadcasted_iota(jnp.int32, logits.shape, 1).astype(F32)
    neg = jnp.float32(-jnp.inf)
    lg = jnp.where(lane < n_experts, logits, neg)
    v1 = jnp.max(lg, axis=-1, keepdims=True)
    i1 = jnp.min(jnp.where(lg == v1, lane, float(LANES)), axis=-1, keepdims=True)
    lg2 = jnp.where(lane == i1, neg, lg)
    v2 = jnp.max(lg2, axis=-1, keepdims=True)
    i2 = jnp.min(jnp.where(lg2 == v2, lane, float(LANES)), axis=-1, keepdims=True)
    e2 = jnp.exp(v2 - v1)
    den = 1.0 + e2
    first = lane == i1
    second = lane == i2
    sel = jnp.where(jnp.logical_or(first, second), 1.0, 0.0)
    comb = jnp.where(first, 1.0 / den, 0.0) + jnp.where(second, e2 / den, 0.0)
    return sel, comb


def _moe_kernel(h_ref, mod_ref, g_ref, fg_ref, rw_ref, w1_ref, w3_ref, w2_ref, o_ref,
                u_ref, x_ref, y_ref, rank_ref, rank_t_ref, comb_ref, cnt_ref, *, n_experts, final_norm):
    e = pl.program_id(2)
    f = pl.program_id(3)
    n_sub = MOE_BLOCK // MOE_SUB

    @pl.when(jnp.logical_and(e == 0, f == 0))
    def _route():
        r = lax.broadcasted_iota(jnp.int32, (ROW_TILE, ROW_TILE), 0)
        c = lax.broadcasted_iota(jnp.int32, (ROW_TILE, ROW_TILE), 1)
        before = jnp.where(c < r, 1.0, 0.0).astype(BF16)
        count = jnp.zeros((1, LANES), F32)
        for t in range(MOE_BLOCK // ROW_TILE):
            rows = slice(t * ROW_TILE, (t + 1) * ROW_TILE)
            u = _norm_mod(h_ref[rows, :], g_ref[...], mod_ref[0:1, :], mod_ref[1:2, :])
            u_ref[rows, :] = u.astype(BF16)
            sel, comb = _top2_routing(_dot(u, rw_ref[...]), n_experts)
            rank = jnp.dot(before, sel.astype(BF16), preferred_element_type=F32) + count
            rank_ref[rows, :] = jnp.where(sel > 0.0, rank, -1.0)
            comb_ref[rows, :] = comb
            count = count + jnp.sum(sel, axis=0, keepdims=True)
        cnt_ref[...] = jnp.broadcast_to(count, cnt_ref.shape)
        rank_t_ref[...] = rank_ref[...].T
        o_ref[...] = jnp.zeros_like(o_ref)

    lane = lax.broadcasted_iota(jnp.int32, (1, LANES), 1)
    n_rows = jnp.sum(jnp.where(lane == e, cnt_ref[0:1, :], 0.0)).astype(jnp.int32)
    n_tiles = (n_rows + MOE_TILE - 1) // MOE_TILE
    rank_row = rank_t_ref[pl.ds(e, 1), :]

    def sub_hits(row0, s):
        rk = rank_row[:, s * MOE_SUB:(s + 1) * MOE_SUB]
        lo = row0.astype(F32)
        inside = jnp.logical_and(rk >= lo, rk < lo + MOE_TILE)
        return rk, jnp.max(jnp.where(inside, 1.0, 0.0)) > 0.0

    @pl.when(f == 0)
    def _compact():
        def body(i, carry):
            row0 = pl.multiple_of(i * MOE_TILE, MOE_TILE)
            ids = (row0 + lax.broadcasted_iota(jnp.int32, (MOE_TILE, 1), 0)).astype(F32)
            x_ref[pl.ds(row0, MOE_TILE), :] = jnp.zeros((MOE_TILE, x_ref.shape[1]), BF16)
            for s in range(n_sub):
                rk, hit = sub_hits(row0, s)

                @pl.when(hit)
                def _():
                    pick = jnp.where(rk == ids, 1.0, 0.0).astype(BF16)
                    got = jnp.dot(pick, u_ref[s * MOE_SUB:(s + 1) * MOE_SUB, :], preferred_element_type=F32)
                    x_ref[pl.ds(row0, MOE_TILE), :] += got.astype(BF16)
            return carry

        lax.fori_loop(0, n_tiles, body, 0)

    def expert_body(i, carry):
        row0 = pl.multiple_of(i * MOE_TILE, MOE_TILE)
        x = x_ref[pl.ds(row0, MOE_TILE), :]
        hid = _silu(jnp.dot(x, w1_ref[...], preferred_element_type=F32)) * jnp.dot(
            x, w3_ref[...], preferred_element_type=F32)
        y = _dot(hid, w2_ref[...])

        @pl.when(f == 0)
        def _():
            y_ref[pl.ds(row0, MOE_TILE), :] = y

        @pl.when(f > 0)
        def _():
            y_ref[pl.ds(row0, MOE_TILE), :] += y

        return carry

    lax.fori_loop(0, n_tiles, expert_body, 0)

    @pl.when(f == MOE_FCHUNKS - 1)
    def _expand():
        lanes = lax.broadcasted_iota(jnp.int32, (MOE_SUB, LANES), 1)

        def body(i, carry):
            row0 = pl.multiple_of(i * MOE_TILE, MOE_TILE)
            yt = y_ref[pl.ds(row0, MOE_TILE), :].astype(BF16)
            ids = (row0 + lax.broadcasted_iota(jnp.int32, (1, MOE_TILE), 1)).astype(F32)
            for s in range(n_sub):
                _, hit = sub_hits(row0, s)

                @pl.when(hit)
                def _():
                    rows = slice(s * MOE_SUB, (s + 1) * MOE_SUB)
                    rk_col = jnp.sum(jnp.where(lanes == e, rank_ref[rows, :], 0.0), axis=1, keepdims=True)
                    w_col = jnp.sum(jnp.where(lanes == e, comb_ref[rows, :], 0.0), axis=1, keepdims=True)
                    put = jnp.where(rk_col == ids, 1.0, 0.0).astype(BF16)
                    o_ref[rows, :] += w_col * jnp.dot(put, yt, preferred_element_type=F32)
            return carry

        lax.fori_loop(0, n_tiles, body, 0)

    @pl.when(jnp.logical_and(e == n_experts - 1, f == MOE_FCHUNKS - 1))
    def _finish():
        for t in range(MOE_BLOCK // ROW_TILE):
            rows = slice(t * ROW_TILE, (t + 1) * ROW_TILE)
            out = h_ref[rows, :] + mod_ref[2:3, :] * o_ref[rows, :]
            if final_norm:
                out = out * lax.rsqrt(jnp.mean(out * out, axis=-1, keepdims=True) + EPS) * fg_ref[...]
            o_ref[rows, :] = out


def _moe(h, mod, norm_g, final_g, router_w, w1, w3, w2, final_norm):
    n_batch, s, d = h.shape
    n_exp, _, f = w1.shape
    assert s % MOE_BLOCK == 0 and f % (MOE_FCHUNKS * LANES) == 0 and n_exp <= 8
    fc = f // MOE_FCHUNKS
    blk = lambda b, j, e, c: (b, j, 0)
    single = pl.Buffered(1)
    return pl.pallas_call(
        functools.partial(_moe_kernel, n_experts=n_exp, final_norm=final_norm),
        grid=(n_batch, s // MOE_BLOCK, n_exp, MOE_FCHUNKS),
        in_specs=[pl.BlockSpec((None, MOE_BLOCK, d), blk, pipeline_mode=single),
                  pl.BlockSpec((None, 3, d), lambda b, j, e, c: (b, 0, 0)),
                  _const_spec((1, d)), _const_spec((1, d)), _const_spec(router_w.shape),
                  pl.BlockSpec((None, d, fc), lambda b, j, e, c: (e, 0, c)),
                  pl.BlockSpec((None, d, fc), lambda b, j, e, c: (e, 0, c)),
                  pl.BlockSpec((None, fc, d), lambda b, j, e, c: (e, c, 0))],
        out_specs=pl.BlockSpec((None, MOE_BLOCK, d), blk, pipeline_mode=single),
        out_shape=jax.ShapeDtypeStruct((n_batch, s, d), F32),
        scratch_shapes=[pltpu.VMEM((MOE_BLOCK, d), BF16),
                        pltpu.VMEM((MOE_BLOCK, d), BF16),
                        pltpu.VMEM((MOE_BLOCK, d), F32),
                        pltpu.VMEM((MOE_BLOCK, LANES), F32),
                        pltpu.VMEM((LANES, MOE_BLOCK), F32),
                        pltpu.VMEM((MOE_BLOCK, LANES), F32),
                        pltpu.VMEM((HALO, LANES), F32)],
        compiler_params=_cparams(("parallel", "parallel", "arbitrary", "arbitrary")),
        name="moe_top2",
    )(h, mod, norm_g.reshape(1, d), final_g.reshape(1, d), router_w, w1, w3, w2)


def _block_diag(w):
    g, n, _ = w.shape
    out = jnp.zeros((g * n, g * n), w.dtype)
    for i in range(g):
        out = out.at[i * n:(i + 1) * n, i * n:(i + 1) * n].set(w[i])
    return out


def kernel(x, c, ctx, c_ctx, ada_w, ada_b, norm1_g, norm2_g, w_in, attn_lambda, attn_subln_g, dn_conv_w,
           dn_a_log, dn_dt_bias, dn_norm_g, pool_w, pool_scale, w_branch, w_out, ffn_w1, ffn_w3, ffn_w2,
           router_w, moe_w1, moe_w3, moe_w2, final_norm_g):
    n_batch, seq, d = x.shape
    n_ctx = ctx.shape[1]
    depth = ada_w.shape[0]
    assert n_ctx == ROW_TILE and seq % ROW_TILE == 0 and seq % GRID_W == 0 and d == 1024
    t = n_ctx + seq
    tpb = t // ROW_TILE

    h = jnp.concatenate([ctx, x], axis=1)
    c_rows = jnp.concatenate([c, c_ctx[None, :], jnp.zeros((8 - n_batch - 1, d), F32)], axis=0)
    mods = _ada_mod(c_rows, ada_w, ada_b).reshape(depth, 8, 6, d)
    rope_tabs = _rope_tables(n_ctx, seq)

    sizes = (ATT_W, ATT_W, ATT_W, DN_W, DN_W, DN_W, DN_W, 2 * DN_HEADS, 2 * DN_HEADS, POOL_W, 3 * d)
    offs = np.concatenate([[0], np.cumsum(sizes)]).tolist()

    for l in range(depth):
        last = l == depth - 1
        first_tile = 1 if last else 0
        lambda_init = 0.8 - 0.6 * math.exp(-0.3 * l)
        wl = w_in[l].astype(BF16)
        seg = lambda a, b: wl[:, offs[a]:offs[b]]
        w_ab = jnp.concatenate([seg(7, 9), jnp.zeros((d, LANES - 4 * DN_HEADS), BF16)], axis=1)
        weights = [seg(0, 1), seg(1, 2), seg(2, 3), seg(3, 7), w_ab, seg(9, 10), seg(10, 11)]
        mod = mods[l, :n_batch + 1]
        q, k, v, dn, ab, pool, gates = _in_projection(
            h.reshape(n_batch * t, d), mod[:, 0:2], norm1_g[l], rope_tabs, weights, n_batch, tpb)
        q, k, v = (a.reshape(n_batch, t, ATT_W) for a in (q, k, v))
        att, att_ctx = _diff_attention(q, k, v, attn_lambda[l], attn_subln_g[l], lambda_init, n_ctx, not last)
        dn = dn.reshape(n_batch, t, 4 * DN_W)
        prep = _dn_prep(dn, ab.reshape(n_batch, t, LANES), dn_conv_w[l], dn_a_log[l], dn_dt_bias[l], tpb)
        o_f, o_b = _dn_scan(prep, n_ctx)
        wb = w_branch[l].astype(BF16)
        h = _merge(h, mod[:, 2:3], att, att_ctx, o_f, o_b, dn, dn_norm_g[l], pool.reshape(n_batch, t, POOL_W),
                   _block_diag(pool_w[l]).astype(BF16), pool_scale[l], gates.reshape(n_batch, t, 3 * d),
                   wb[:ATT_W], wb[ATT_W:ATT_W + DN_W], wb[ATT_W + DN_W:], w_out[l].astype(BF16),
                   first_tile, n_ctx)
        i = l // 2
        if l % 2 == 0:
            h = _ffn(h, mod[:, 3:6], norm2_g[l], final_norm_g, None, ffn_w1[i][None].astype(BF16),
                     ffn_w3[i][None].astype(BF16), ffn_w2[i][None].astype(BF16), not last, last)
        else:
            rw = jnp.concatenate([router_w[i], jnp.zeros((d, LANES - N_EXPERTS), F32)], axis=1).astype(BF16)
            w1, w3, w2 = moe_w1[i].astype(BF16), moe_w3[i].astype(BF16), moe_w2[i].astype(BF16)
            if last:
                h = _moe(h, mod[:n_batch, 3:6], norm2_g[l], final_norm_g, rw, w1, w3, w2, True)
            else:
                h = _ffn(h, mod[:, 3:6], norm2_g[l], final_norm_g, rw, w1, w3, w2, True, False)
    return h
```

```python
import functools
import math

import numpy as np
import jax
import jax.numpy as jnp
from jax import lax
from jax.experimental import pallas as pl
from jax.experimental.pallas import tpu as pltpu

F32 = jnp.float32
BF16 = jnp.bfloat16

EPS = 1e-6
GRID_W = 64
ROPE_BASE = 10000.0
ATT_HEADS = 4
ATT_DH = 64
ATT_VD = 2 * ATT_DH
ATT_W = ATT_HEADS * ATT_VD
DN_HEADS = 4
DN_DH = 64
DN_W = DN_HEADS * DN_DH
DN_CONV = 5
DN_CHUNK = 64
DN_SUB = 16
POOL_WINDOWS = (2, 4, 8, 16)
POOL_GD = 64
POOL_W = len(POOL_WINDOWS) * POOL_GD
N_EXPERTS = 8
TOP_K = 2

ROW_TILE = 256
HALO = 8
LANES = 128
VMEM_LIMIT = 56 * 1024 * 1024


def _cparams(sem):
    return pltpu.CompilerParams(dimension_semantics=sem, vmem_limit_bytes=VMEM_LIMIT)


def _dot(a, b):
    return jnp.dot(a.astype(BF16), b.astype(BF16), preferred_element_type=F32)


def _dot_nt(a, b):
    return lax.dot_general(a.astype(BF16), b.astype(BF16), (((1,), (1,)), ((), ())),
                           preferred_element_type=F32)


def _bmm(a, b):
    return lax.dot_general(a.astype(BF16), b.astype(BF16), (((2,), (1,)), ((0,), (0,))),
                           preferred_element_type=F32)


def _bmm_nt(a, b):
    return lax.dot_general(a.astype(BF16), b.astype(BF16), (((2,), (2,)), ((0,), (0,))),
                           preferred_element_type=F32)


def _sigmoid(x):
    return 1.0 / (1.0 + jnp.exp(-x))


def _silu(x):
    return x * _sigmoid(x)


def _const_spec(shape):
    nd = len(shape)
    return pl.BlockSpec(shape, lambda *_: (0,) * nd)


def _ada_kernel(c_ref, w_ref, b_ref, o_ref):
    o_ref[...] = _dot(_silu(c_ref[...]), w_ref[...]) + b_ref[...]


def _ada_mod(c_rows, ada_w, ada_b):
    depth, d, six_d = ada_w.shape
    n = six_d // d
    rows = c_rows.shape[0]
    return pl.pallas_call(
        _ada_kernel,
        grid=(depth, n),
        in_specs=[pl.BlockSpec((rows, d), lambda l, j: (0, 0)),
                  pl.BlockSpec((None, d, d), lambda l, j: (l, 0, j)),
                  pl.BlockSpec((None, 1, d), lambda l, j: (l, 0, j))],
        out_specs=pl.BlockSpec((None, rows, d), lambda l, j: (l, 0, j)),
        out_shape=jax.ShapeDtypeStruct((depth, rows, six_d), F32),
        compiler_params=_cparams(("parallel", "parallel")),
        name="ada_mod",
    )(c_rows, ada_w, ada_b.reshape(depth, 1, six_d))


def _norm_mod(x, g, shift, scale):
    y = x * lax.rsqrt(jnp.mean(x * x, axis=-1, keepdims=True) + EPS) * g
    return y * (1.0 + scale) + shift


def _inproj_kernel(h_ref, mod_ref, g_ref, cos_ref, su_ref, sd_ref,
                   wq_ref, wk_ref, wv_ref, wdn_ref, wab_ref, wpool_ref, wg_ref,
                   q_ref, k_ref, v_ref, dn_ref, ab_ref, pool_ref, gate_ref):
    u = _norm_mod(h_ref[...], g_ref[...], mod_ref[0:1, :], mod_ref[1:2, :]).astype(BF16)
    reps = ATT_W // LANES
    cos = jnp.tile(cos_ref[...], (1, reps))
    s_up = jnp.tile(su_ref[...], (1, reps))
    s_dn = jnp.tile(sd_ref[...], (1, reps))
    quarter = ATT_DH // 4

    def rope(t):
        return (t * cos + pltpu.roll(t, ATT_W - quarter, 1) * s_up + pltpu.roll(t, quarter, 1) * s_dn)

    q = rope(jnp.dot(u, wq_ref[...], preferred_element_type=F32))
    q_ref[...] = (q * (ATT_DH ** -0.5 * LOG2E)).astype(q_ref.dtype)
    k_ref[...] = rope(jnp.dot(u, wk_ref[...], preferred_element_type=F32)).astype(k_ref.dtype)
    v_ref[...] = jnp.dot(u, wv_ref[...], preferred_element_type=F32).astype(v_ref.dtype)
    dn_ref[...] = jnp.dot(u, wdn_ref[...], preferred_element_type=F32)
    ab_ref[...] = jnp.dot(u, wab_ref[...], preferred_element_type=F32)
    pool_ref[...] = jnp.dot(u, wpool_ref[...], preferred_element_type=F32)
    gate_ref[...] = _sigmoid(jnp.dot(u, wg_ref[...], preferred_element_type=F32)).astype(gate_ref.dtype)


def _in_projection(h, mod, norm_g, rope_tabs, weights, n_batch, tiles_per_batch):
    rows, d = h.shape
    n_tiles = rows // ROW_TILE
    tpb = tiles_per_batch
    widths = [w.shape[1] for w in weights]
    out_dtypes = [BF16, BF16, BF16, F32, F32, F32, BF16]

    def mod_idx(i):
        return (jnp.where(i % tpb == 0, n_batch, i // tpb), 0, 0)

    in_specs = [pl.BlockSpec((ROW_TILE, d), lambda i: (i, 0)),
                pl.BlockSpec((None, 2, d), mod_idx),
                _const_spec((1, d))]
    in_specs += [pl.BlockSpec((ROW_TILE, LANES), lambda i: (i % tpb, 0)) for _ in range(3)]
    in_specs += [pl.BlockSpec(w.shape, lambda i: (0, 0), pipeline_mode=pl.Buffered(1)) for w in weights]
    out_specs = [pl.BlockSpec((ROW_TILE, w), lambda i: (i, 0)) for w in widths]
    out_shape = [jax.ShapeDtypeStruct((rows, w), dt) for w, dt in zip(widths, out_dtypes)]
    return pl.pallas_call(
        _inproj_kernel,
        grid=(n_tiles,),
        in_specs=in_specs,
        out_specs=out_specs,
        out_shape=out_shape,
        compiler_params=_cparams(("parallel",)),
        name="in_projection",
    )(h, mod, norm_g.reshape(1, d), *rope_tabs, *weights)


def _rope_tables(n_ctx, seq):
    rows = seq // GRID_W
    row = jnp.repeat(jnp.arange(rows), GRID_W).astype(F32)
    col = jnp.tile(jnp.arange(GRID_W), rows).astype(F32)
    n_freq = ATT_DH // 4
    inv = ROPE_BASE ** (-jnp.arange(n_freq, dtype=F32) / n_freq)
    ar = row[:, None] * inv
    ac = col[:, None] * inv
    cr, sr, cc, sc = jnp.cos(ar), jnp.sin(ar), jnp.cos(ac), jnp.sin(ac)
    z = jnp.zeros_like(sr)
    cos = jnp.concatenate([cr, cr, cc, cc], axis=-1)
    s_up = jnp.concatenate([-sr, z, -sc, z], axis=-1)
    s_dn = jnp.concatenate([z, sr, z, sc], axis=-1)

    def full(t, ctx_val):
        t = jnp.tile(t, (1, LANES // ATT_DH))
        return jnp.concatenate([jnp.full((n_ctx, LANES), ctx_val, F32), t], axis=0)

    return full(cos, 1.0), full(s_up, 0.0), full(s_dn, 0.0)


ATT_KEY_TILE = 512
LOG2E = math.log2(math.e)


ATT_STREAMS = 2


class _AttnStream:
    def __init__(self, q, e_ref, k_ref, v_ref):
        lane = lax.broadcasted_iota(jnp.int32, q.shape, 1)
        zero = jnp.zeros_like(q)
        self.qm = (jnp.where(lane < ATT_DH, q, zero), jnp.where(lane >= ATT_DH, q, zero))
        self.e_ref, self.k_ref, self.v_ref = e_ref, k_ref, v_ref
        tq = q.shape[0]
        self.m_part = [jnp.full((tq, LANES), -jnp.inf, F32)] * 2
        self.l_part = [jnp.zeros((tq, LANES), F32)] * 2
        self.acc = jnp.zeros((tq, ATT_VD), F32)

    @staticmethod
    def _lane_fold(x, op):
        out = x[:, 0:LANES]
        for c in range(1, x.shape[1] // LANES):
            out = op(out, x[:, c * LANES:(c + 1) * LANES])
        return out

    def scores(self, k0, w):
        kt = self.k_ref[k0:k0 + w, :]
        for m in range(2):
            s = _dot_nt(self.qm[m], kt)
            self.e_ref[m, :, k0:k0 + w] = s
            self.m_part[m] = jnp.maximum(self.m_part[m], self._lane_fold(s, jnp.maximum))

    def end_scores(self):
        self.mx = [jnp.max(p, axis=-1, keepdims=True) for p in self.m_part]

    def exps(self, k0, w):
        for m in range(2):
            e = jnp.exp2(self.e_ref[m, :, k0:k0 + w] - self.mx[m])
            self.e_ref[m, :, k0:k0 + w] = e
            self.l_part[m] = self.l_part[m] + self._lane_fold(e, jnp.add)

    def end_exps(self, lam):
        self.inv0 = 1.0 / jnp.sum(self.l_part[0], axis=-1, keepdims=True)
        self.inv1 = lam / jnp.sum(self.l_part[1], axis=-1, keepdims=True)

    def values(self, k0, w):
        a = self.e_ref[0, :, k0:k0 + w] * self.inv0 - self.e_ref[1, :, k0:k0 + w] * self.inv1
        self.acc = self.acc + jnp.dot(a.astype(BF16), self.v_ref[k0:k0 + w, :], preferred_element_type=F32)

    def result(self, g, lambda_init):
        y = self.acc * lax.rsqrt(jnp.mean(self.acc * self.acc, axis=-1, keepdims=True) + EPS) * g
        return y * (1.0 - lambda_init)


def _attn_run(streams, tiles, lam, g, lambda_init, outs):
    passes = ("scores", "exps", "values")
    for step in range(len(passes) + len(streams) - 1):
        active = [(st, passes[step - i]) for i, st in enumerate(streams) if 0 <= step - i < len(passes)]
        for k0, w in tiles:
            for st, name in active:
                getattr(st, name)(k0, w)
        for st, name in active:
            if name == "scores":
                st.end_scores()
            elif name == "exps":
                st.end_exps(lam)
    for st, (o_ref, rows) in zip(streams, outs):
        o_ref[rows, :] = st.result(g, lambda_init).astype(o_ref.dtype)


def _attn_kernel(*refs, lambda_init, n_ctx, ctx_out):
    if ctx_out:
        lam_ref, g_ref, q_ref, qc_ref, k_ref, v_ref, o_ref, oc_ref, e_ref = refs
    else:
        lam_ref, g_ref, q_ref, k_ref, v_ref, o_ref, e_ref = refs
    n_keys = k_ref.shape[0]
    ctx_tiles = ((0, n_ctx),)
    all_tiles = ctx_tiles + tuple((k0, ATT_KEY_TILE) for k0 in range(n_ctx, n_keys, ATT_KEY_TILE))
    lp = lam_ref[...]
    lam = (jnp.exp(jnp.sum(lp[0:1] * lp[1:2], keepdims=True))
           - jnp.exp(jnp.sum(lp[2:3] * lp[3:4], keepdims=True)) + lambda_init)
    g = g_ref[...]

    if ctx_out:
        @pl.when(pl.program_id(2) == 0)
        def _():
            st = _AttnStream(qc_ref[...], e_ref.at[0], k_ref, v_ref)
            _attn_run([st], ctx_tiles, lam, g, lambda_init, [(oc_ref, slice(None))])

    streams, outs = [], []
    for i in range(ATT_STREAMS):
        rows = slice(i * ROW_TILE, (i + 1) * ROW_TILE)
        streams.append(_AttnStream(q_ref[0, rows, :], e_ref.at[i], k_ref, v_ref))
        outs.append((o_ref, rows))
    _attn_run(streams, all_tiles, lam, g, lambda_init, outs)


def _diff_attention(q, k, v, lam_params, subln_g, lambda_init, n_ctx, ctx_out):
    n_batch, t, _ = q.shape
    seq = t - n_ctx
    tq = ATT_STREAMS * ROW_TILE
    assert seq % ATT_KEY_TILE == 0 and seq % tq == 0
    kv_spec = pl.BlockSpec((None, t, ATT_VD), lambda b, h, j: (b, 0, h))
    ctx_spec = pl.BlockSpec((None, n_ctx, ATT_VD), lambda b, h, j: (b, 0, h))
    in_specs = [_const_spec(lam_params.shape), _const_spec((1, ATT_VD)),
                pl.BlockSpec((pl.Element(1), pl.Element(tq), pl.Element(ATT_VD)),
                             lambda b, h, j: (b, pl.multiple_of(n_ctx + j * tq, ROW_TILE),
                                              pl.multiple_of(h * ATT_VD, LANES)))]
    out_specs = [pl.BlockSpec((None, tq, ATT_VD), lambda b, h, j: (b, j, h))]
    out_shape = [jax.ShapeDtypeStruct((n_batch, seq, ATT_W), BF16)]
    args = [lam_params, subln_g.reshape(1, ATT_VD), q]
    if ctx_out:
        in_specs.append(ctx_spec)
        args.append(q)
        out_specs.append(ctx_spec)
        out_shape.append(jax.ShapeDtypeStruct((n_batch, n_ctx, ATT_W), BF16))
    kern = functools.partial(_attn_kernel, lambda_init=lambda_init, n_ctx=n_ctx, ctx_out=ctx_out)
    outs = pl.pallas_call(
        kern,
        grid=(n_batch, ATT_HEADS, seq // tq),
        in_specs=in_specs + [kv_spec, kv_spec],
        out_specs=out_specs,
        out_shape=out_shape,
        scratch_shapes=[pltpu.VMEM((ATT_STREAMS, 2, ROW_TILE, t), F32)],
        compiler_params=_cparams(("parallel", "parallel", "arbitrary")),
        name="diff_attention",
    )(*args, k, v)
    return outs[0], (outs[1] if ctx_out else None)


def _halo_specs(width, tiles_per_batch):
    per = ROW_TILE // HALO
    n_blocks = None

    def prev_idx(b, j):
        return (b, jnp.maximum(j * per - 1, 0), 0)

    def next_idx(b, j):
        return (b, jnp.minimum((j + 1) * per, tiles_per_batch * per - 1), 0)

    del n_blocks
    return (pl.BlockSpec((None, HALO, width), prev_idx), pl.BlockSpec((None, HALO, width), next_idx))


def _with_halo(prev_ref, cur, next_ref, width, tiles_per_batch, j):
    left_ok = jnp.where(j > 1, 1.0, 0.0)
    right_ok = jnp.where(jnp.logical_and(j > 0, j < tiles_per_batch - 1), 1.0, 0.0)
    return jnp.concatenate([prev_ref[:, 0:width] * left_ok, cur, next_ref[:, 0:width] * right_ok], axis=0)


def _dn_prep_kernel(dn_ref, prev_ref, next_ref, ab_ref, cw_ref, alog_ref, dtb_ref,
                    u_ref, w_ref, qg_ref, qk_ref, kt_ref, gl_ref, *, tiles_per_batch):
    j = pl.program_id(1)
    width = 3 * DN_W
    xe = _with_halo(prev_ref, dn_ref[:, 0:width], next_ref, width, tiles_per_batch, j)
    half = DN_CONV // 2
    acc = None
    for tap in range(DN_CONV):
        lo = HALO - half + tap
        term = xe[lo:lo + ROW_TILE, :] * cw_ref[tap:tap + 1, :]
        acc = term if acc is None else acc + term
    qkv = _silu(acc)

    def l2n(x):
        return x * lax.rsqrt(jnp.sum(x * x, axis=-1, keepdims=True) + EPS)

    heads = range(DN_HEADS)
    qn = [l2n(qkv[:, h * DN_DH:(h + 1) * DN_DH]) * (DN_DH ** -0.5) for h in heads]
    kn = [l2n(qkv[:, DN_W + h * DN_DH:DN_W + (h + 1) * DN_DH]) for h in heads]
    vv = [qkv[:, 2 * DN_W + h * DN_DH:2 * DN_W + (h + 1) * DN_DH] for h in heads]
    kn_t = jnp.concatenate(kn, axis=-1).T

    ab = ab_ref[...]
    x = ab + dtb_ref[...]
    g = -jnp.exp(alog_ref[...]) * (jnp.maximum(x, 0.0) + jnp.log(1.0 + jnp.exp(-jnp.abs(x))))
    n_dh = 2 * DN_HEADS
    beta = _sigmoid(pltpu.roll(ab, LANES - n_dh, 1))
    r = lax.broadcasted_iota(jnp.int32, (ROW_TILE, ROW_TILE), 0)
    c = lax.broadcasted_iota(jnp.int32, (ROW_TILE, ROW_TILE), 1)
    same_chunk = (r >> 6) == (c >> 6)
    tri_f = jnp.where(jnp.logical_and(same_chunk, c <= r), 1.0, 0.0)
    tri_b = jnp.where(jnp.logical_and(same_chunk, c >= r), 1.0, 0.0)
    hp = lax.Precision.HIGHEST
    lane = lax.broadcasted_iota(jnp.int32, g.shape, 1)
    gcum = jnp.where(lane < DN_HEADS,
                     jnp.dot(tri_f, g, precision=hp, preferred_element_type=F32),
                     jnp.dot(tri_b, g, precision=hp, preferred_element_type=F32))
    gcum_t = gcum.T

    ri = lax.broadcasted_iota(jnp.int32, (DN_CHUNK, DN_CHUNK), 0)
    ci = lax.broadcasted_iota(jnp.int32, (DN_CHUNK, DN_CHUNK), 1)
    eye = jnp.where(ri == ci, 1.0, 0.0)
    same_sub = (ri >> 4) == (ci >> 4)
    n_chunks = ROW_TILE // DN_CHUNK

    for ch in range(n_chunks):
        rows = slice(ch * DN_CHUNK, (ch + 1) * DN_CHUNK)
        gram = [_dot_nt(kn[h][rows], kn[h][rows]) for h in heads]
        qkm = [_dot_nt(qn[h][rows], kn[h][rows]) for h in heads]
        m_list, rhs_list = [], []
        for d in range(2):
            incl = (ci <= ri) if d == 0 else (ci >= ri)
            strict = (ci < ri) if d == 0 else (ci > ri)
            last = DN_CHUNK - 1 if d == 0 else 0
            for h in heads:
                idx = d * DN_HEADS + h
                g_col = gcum[rows, idx:idx + 1]
                g_row = gcum_t[idx:idx + 1, rows]
                b_col = beta[rows, idx:idx + 1]
                decay = jnp.where(incl, jnp.exp(jnp.minimum(g_col - g_row, 0.0)), 0.0)
                eg = jnp.exp(g_col)
                m_list.append(jnp.where(strict, gram[h] * decay, 0.0) * b_col)
                rhs_list.append(jnp.concatenate([vv[h][rows] * b_col, kn[h][rows] * (b_col * eg)], axis=-1))
                g_last = g_row[:, last:last + 1]
                qg_ref[idx, rows, :] = (qn[h][rows] * eg).astype(qg_ref.dtype)
                qk_ref[idx, rows, :] = (qkm[h] * decay).astype(qk_ref.dtype)
                kt_ref[idx, ch] = (kn_t[h * DN_DH:(h + 1) * DN_DH, rows]
                                   * jnp.exp(g_last - g_row)).astype(kt_ref.dtype)
                gl_ref[idx, ch * HALO:(ch + 1) * HALO, :] = jnp.broadcast_to(jnp.exp(g_last), (HALO, DN_DH))
        m = jnp.stack(m_list)
        rhs = jnp.stack(rhs_list)
        n_diag = jnp.where(same_sub, m, 0.0)
        m_off = m - n_diag
        xp = -n_diag
        dinv = eye + xp
        for _ in range(3):
            xp = _bmm(xp, xp)
            dinv = dinv + _bmm(dinv, xp)
        f = _bmm(dinv, m_off)
        f2 = _bmm(f, f)
        y = _bmm(dinv, rhs)
        y = y - _bmm(f, y)
        sol = y + _bmm(f2, y)
        for idx in range(n_dh):
            u_ref[idx, rows, :] = sol[idx, :, 0:DN_DH]
            w_ref[idx, rows, :] = sol[idx, :, DN_DH:2 * DN_DH].astype(w_ref.dtype)


def _dn_prep(dn, ab, conv_w, a_log, dt_bias, tiles_per_batch):
    n_batch, t, dn_cols = dn.shape
    n_dh = 2 * DN_HEADS
    pad = jnp.zeros((1, LANES - n_dh), F32)
    alog_row = jnp.concatenate([a_log.reshape(1, n_dh).astype(F32), pad], axis=-1)
    dtb_row = jnp.concatenate([dt_bias.reshape(1, n_dh).astype(F32), pad], axis=-1)
    prev_spec, next_spec = _halo_specs(dn_cols, tiles_per_batch)
    tok = lambda b, j: (b, 0, j, 0)
    out_specs = [pl.BlockSpec((None, n_dh, ROW_TILE, DN_DH), tok)] * 4
    out_specs += [pl.BlockSpec((None, n_dh, ROW_TILE // DN_CHUNK, DN_DH, DN_CHUNK), lambda b, j: (b, 0, j, 0, 0)),
                  pl.BlockSpec((None, n_dh, HALO * ROW_TILE // DN_CHUNK, DN_DH), tok)]
    tok_shape = (n_batch, n_dh, t, DN_DH)
    out_shape = [jax.ShapeDtypeStruct(tok_shape, F32),
                 jax.ShapeDtypeStruct(tok_shape, BF16),
                 jax.ShapeDtypeStruct(tok_shape, BF16),
                 jax.ShapeDtypeStruct(tok_shape, BF16),
                 jax.ShapeDtypeStruct((n_batch, n_dh, t // DN_CHUNK, DN_DH, DN_CHUNK), BF16),
                 jax.ShapeDtypeStruct((n_batch, n_dh, HALO * t // DN_CHUNK, DN_DH), F32)]
    return pl.pallas_call(
        functools.partial(_dn_prep_kernel, tiles_per_batch=tiles_per_batch),
        grid=(n_batch, tiles_per_batch),
        in_specs=[pl.BlockSpec((None, ROW_TILE, dn_cols), lambda b, j: (b, j, 0)),
                  prev_spec, next_spec,
                  pl.BlockSpec((None, ROW_TILE, LANES), lambda b, j: (b, j, 0)),
                  _const_spec(conv_w.shape), _const_spec((1, LANES)), _const_spec((1, LANES))],
        out_specs=out_specs,
        out_shape=out_shape,
        compiler_params=_cparams(("parallel", "parallel")),
        name="dn_prep",
    )(dn, dn, dn, ab, conv_w, alog_row, dtb_row)


def _dn_scan_kernel(*refs):
    ins, (of_ref, ob_ref, s_ref) = refs[:12], refs[12:]

    @pl.when(pl.program_id(0) == 0)
    def _():
        s_ref[...] = jnp.zeros_like(s_ref)

    for d, o_ref in enumerate((of_ref, ob_ref)):
        u_ref, w_ref, qg_ref, qk_ref, kt_ref, gl_ref = ins[6 * d:6 * d + 6]
        flat = lambda r: r[...].reshape((-1,) + r.shape[2:])
        s = s_ref[d]
        v_new = flat(u_ref) - _bmm(flat(w_ref), s)
        o = _bmm(flat(qg_ref), s) + _bmm(flat(qk_ref), v_new)
        o_ref[...] = o.reshape(o_ref.shape)
        s_ref[d] = s * flat(gl_ref)[:, 0:1, :] + _bmm(flat(kt_ref), v_new)


def _dn_scan(prep, n_ctx):
    u, w, qg, qk, kt, gl = prep
    n_batch, _, t, _ = u.shape
    n_steps = t // DN_CHUNK
    ctx_chunks = n_ctx // DN_CHUNK

    def chunk_of(d, i):
        if d == 0:
            return i
        return jnp.where(i < ctx_chunks, ctx_chunks - 1 - i, n_steps - 1 + ctx_chunks - i)

    in_specs = []
    for d in range(2):
        tok = functools.partial(lambda i, d: (0, d, chunk_of(d, i), 0), d=d)
        in_specs += [pl.BlockSpec((n_batch, DN_HEADS, DN_CHUNK, DN_DH), tok)] * 4
        in_specs += [pl.BlockSpec((n_batch, DN_HEADS, None, DN_DH, DN_CHUNK),
                                  functools.partial(lambda i, d: (0, d, chunk_of(d, i), 0, 0), d=d)),
                     pl.BlockSpec((n_batch, DN_HEADS, HALO, DN_DH), tok)]
    out_specs = [pl.BlockSpec((n_batch, DN_HEADS, DN_CHUNK, DN_DH),
                              functools.partial(lambda i, d: (0, 0, chunk_of(d, i), 0), d=d))
                 for d in range(2)]
    out_shape = [jax.ShapeDtypeStruct((n_batch, DN_HEADS, t, DN_DH), F32)] * 2
    return pl.pallas_call(
        _dn_scan_kernel,
        grid=(n_steps,),
        in_specs=in_specs,
        out_specs=out_specs,
        out_shape=out_shape,
        scratch_shapes=[pltpu.VMEM((2, n_batch * DN_HEADS, DN_DH, DN_DH), F32)],
        compiler_params=_cparams(("arbitrary",)),
        name="dn_scan",
    )(u, w, qg, qk, kt, gl, u, w, qg, qk, kt, gl)


def _merge_kernel(h_ref, mod_ref, att_ref, attc_ref, of_ref, ob_ref, z_ref, dng_ref,
                  pool_ref, pprev_ref, pnext_ref, pw_ref, ps_ref, gate_ref,
                  wba_ref, wbd_ref, wbp_ref, wo_ref, o_ref, *, tiles_per_batch, first_tile, seq, n_ctx):
    j = pl.program_id(1) + first_tile
    d = h_ref.shape[-1]
    dn_heads = []
    for h in range(DN_HEADS):
        o = of_ref[h] + ob_ref[h]
        dn_heads.append(o * lax.rsqrt(jnp.mean(o * o, axis=-1, keepdims=True) + EPS) * dng_ref[...])
    dn = jnp.concatenate(dn_heads, axis=-1) * _silu(z_ref[...])
    cur = pool_ref[...]
    xe = _with_halo(pprev_ref, cur, pnext_ref, POOL_W, tiles_per_batch, j)
    length = jnp.where(j == 0, n_ctx, seq)
    t0 = jnp.where(j == 0, 0, (j - 1) * ROW_TILE)
    tpos = t0 + lax.broadcasted_iota(jnp.int32, (ROW_TILE, POOL_W), 0)
    lane = lax.broadcasted_iota(jnp.int32, (ROW_TILE, POOL_W), 1)
    n_ext = ROW_TILE + 2 * HALO
    sums = {1: xe}
    win = 1
    while win < max(POOL_WINDOWS):
        prev = sums[win]
        rows = prev.shape[0] - win
        sums[2 * win] = prev[0:rows, :] + prev[win:win + rows, :]
        win *= 2
    del n_ext
    mean = jnp.zeros((ROW_TILE, POOL_W), F32)
    for gi, win in enumerate(POOL_WINDOWS):
        start = HALO - win // 2
        wsum = sums[win][start:start + ROW_TILE, :]
        lo = jnp.clip(tpos - win // 2, 0, length)
        hi = jnp.clip(tpos - win // 2 + win, 0, length)
        mean = jnp.where((lane >> 6) == gi, wsum / (hi - lo).astype(F32), mean)
    pooled = _dot(mean - cur, pw_ref[...]) * ps_ref[...]
    ga = gate_ref[:, 0:d].astype(F32)
    gd = gate_ref[:, d:2 * d].astype(F32)
    gp = gate_ref[:, 2 * d:3 * d].astype(F32)
    att = jnp.where(j == 0, attc_ref[...], att_ref[...])
    mix = (ga * jnp.dot(att, wba_ref[...], preferred_element_type=F32)
           + gd * _dot(dn, wbd_ref[...]) + gp * _dot(pooled, wbp_ref[...]))
    y = _dot(mix, wo_ref[...])
    o_ref[...] = h_ref[...] + mod_ref[0:1, :] * y


def _merge(h, mod_gate, att, att_ctx, o_f, o_b, dn, dn_norm_g, pool, pool_w_bd, pool_scale, gates,
           wb_att, wb_dn, wb_pool, w_out, first_tile, n_ctx):
    n_batch, t, d = h.shape
    tpb = t // ROW_TILE
    n_rows = tpb - first_tile
    seq = t - n_ctx
    ft = first_tile
    prev_spec, next_spec = _halo_specs(POOL_W, tpb)
    shift = lambda f: (lambda b, j: f(b, j + ft))
    prev_spec = pl.BlockSpec(prev_spec.block_shape, shift(prev_spec.index_map))
    next_spec = pl.BlockSpec(next_spec.block_shape, shift(next_spec.index_map))
    row = lambda b, j: (b, j + ft, 0)

    def mod_idx(b, j):
        return (jnp.where(j + ft == 0, n_batch, b), 0, 0)

    weights = [wb_att, wb_dn, wb_pool, w_out]
    kern = functools.partial(_merge_kernel, tiles_per_batch=tpb, first_tile=ft, seq=seq, n_ctx=n_ctx)
    return pl.pallas_call(
        kern,
        grid=(n_batch, n_rows),
        in_specs=[pl.BlockSpec((None, ROW_TILE, d), row),
                  pl.BlockSpec((None, 1, d), mod_idx),
                  pl.BlockSpec((None, ROW_TILE, ATT_W), lambda b, j: (b, jnp.maximum(j + ft - 1, 0), 0)),
                  pl.BlockSpec((None, ROW_TILE, ATT_W), lambda b, j: (b, 0, 0)),
                  pl.BlockSpec((None, DN_HEADS, ROW_TILE, DN_DH), lambda b, j: (b, 0, j + ft, 0)),
                  pl.BlockSpec((None, DN_HEADS, ROW_TILE, DN_DH), lambda b, j: (b, 0, j + ft, 0)),
                  pl.BlockSpec((None, ROW_TILE, DN_W), lambda b, j: (b, j + ft, 3)),
                  _const_spec((1, DN_DH)),
                  pl.BlockSpec((None, ROW_TILE, POOL_W), row),
                  prev_spec, next_spec,
                  _const_spec(pool_w_bd.shape), _const_spec((1, POOL_W)),
                  pl.BlockSpec((None, ROW_TILE, 3 * d), row)]
                 + [_const_spec(w.shape) for w in weights],
        out_specs=pl.BlockSpec((None, ROW_TILE, d), lambda b, j: (b, j, 0)),
        out_shape=jax.ShapeDtypeStruct((n_batch, n_rows * ROW_TILE, d), F32),
        compiler_params=_cparams(("parallel", "parallel")),
        name="merge",
    )(h, mod_gate, att, att if att_ctx is None else att_ctx, o_f, o_b, dn, dn_norm_g.reshape(1, DN_DH),
      pool, pool, pool,
      pool_w_bd, pool_scale.reshape(1, POOL_W), gates, *weights)


def _ffn_kernel(*refs, n_experts, n_fchunks, final_norm):
    if n_experts > 1:
        h_ref, mod_ref, g_ref, fg_ref, rw_ref, w1_ref, w3_ref, w2_ref, o_ref, u_ref, acc_ref, comb_ref = refs
    else:
        h_ref, mod_ref, g_ref, fg_ref, w1_ref, w3_ref, w2_ref, o_ref, u_ref, acc_ref = refs
    step = pl.program_id(2)
    e = step // n_fchunks

    @pl.when(step == 0)
    def _():
        u = _norm_mod(h_ref[...], g_ref[...], mod_ref[0:1, :], mod_ref[1:2, :])
        u_ref[...] = u.astype(BF16)
        acc_ref[...] = jnp.zeros_like(acc_ref)
        if n_experts > 1:
            logits = _dot(u, rw_ref[...])
            lane = lax.broadcasted_iota(jnp.int32, logits.shape, 1).astype(F32)
            neg = jnp.float32(-jnp.inf)
            lg = jnp.where(lane < n_experts, logits, neg)
            v1 = jnp.max(lg, axis=-1, keepdims=True)
            i1 = jnp.min(jnp.where(lg == v1, lane, float(LANES)), axis=-1, keepdims=True)
            lg2 = jnp.where(lane == i1, neg, lg)
            v2 = jnp.max(lg2, axis=-1, keepdims=True)
            i2 = jnp.min(jnp.where(lg2 == v2, lane, float(LANES)), axis=-1, keepdims=True)
            e2 = jnp.exp(v2 - v1)
            den = 1.0 + e2
            comb_ref[...] = jnp.where(lane == i1, 1.0 / den, 0.0) + jnp.where(lane == i2, e2 / den, 0.0)

    u = u_ref[...]
    hid = _silu(jnp.dot(u, w1_ref[...], preferred_element_type=F32)) * jnp.dot(
        u, w3_ref[...], preferred_element_type=F32)
    y = _dot(hid, w2_ref[...])
    if n_experts > 1:
        lane = lax.broadcasted_iota(jnp.int32, comb_ref.shape, 1)
        y = y * jnp.sum(jnp.where(lane == e, comb_ref[...], 0.0), axis=-1, keepdims=True)
    acc_ref[...] += y

    @pl.when(step == n_experts * n_fchunks - 1)
    def _():
        out = h_ref[...] + mod_ref[2:3, :] * acc_ref[...]
        if final_norm:
            out = out * lax.rsqrt(jnp.mean(out * out, axis=-1, keepdims=True) + EPS) * fg_ref[...]
        o_ref[...] = out


def _ffn(h, mod, norm_g, final_g, router_w, w1, w3, w2, has_ctx, final_norm):
    n_batch, t, d = h.shape
    n_exp, _, f = w1.shape
    n_rows = t // ROW_TILE
    row = lambda b, j, e: (b, j, 0)

    def mod_idx(b, j, e):
        return (jnp.where(jnp.logical_and(has_ctx, j == 0), n_batch, b), 0, 0)

    in_specs = [pl.BlockSpec((None, ROW_TILE, d), row),
                pl.BlockSpec((None, 3, d), mod_idx),
                _const_spec((1, d)), _const_spec((1, d))]
    args = [h, mod, norm_g.reshape(1, d), final_g.reshape(1, d)]
    scratch = [pltpu.VMEM((ROW_TILE, d), BF16), pltpu.VMEM((ROW_TILE, d), F32)]
    if n_exp > 1:
        in_specs.append(_const_spec(router_w.shape))
        args.append(router_w)
        scratch.append(pltpu.VMEM((ROW_TILE, LANES), F32))
    nfc = 1 if n_exp == 1 else 2
    fc = f // nfc
    assert fc * nfc == f and fc % LANES == 0
    in_specs += [pl.BlockSpec((None, d, fc), lambda b, j, s: (s // nfc, 0, s % nfc)),
                 pl.BlockSpec((None, d, fc), lambda b, j, s: (s // nfc, 0, s % nfc)),
                 pl.BlockSpec((None, fc, d), lambda b, j, s: (s // nfc, s % nfc, 0))]
    args += [w1, w3, w2]
    return pl.pallas_call(
        functools.partial(_ffn_kernel, n_experts=n_exp, n_fchunks=nfc, final_norm=final_norm),
        grid=(n_batch, n_rows, n_exp * nfc),
        in_specs=in_specs,
        out_specs=pl.BlockSpec((None, ROW_TILE, d), row),
        out_shape=jax.ShapeDtypeStruct((n_batch, t, d), F32),
        scratch_shapes=scratch,
        input_output_aliases={0: 0},
        compiler_params=_cparams(("parallel", "parallel", "arbitrary")),
        name="ffn_all_experts" if n_exp > 1 else "ffn_dense",
    )(*args)


MOE_BLOCK = 2048
MOE_SUB = 512
MOE_TILE = 576
MOE_ROW_ALIGN = 64
MOE_ROWS = -(-MOE_BLOCK // MOE_TILE) * MOE_TILE
MOE_FCHUNKS = 4


def _top2_routing(logits, n_experts):
    lane = lax.broadcasted_iota(jnp.int32, logits.shape, 1).astype(F32)
    neg = jnp.float32(-jnp.inf)
    lg = jnp.where(lane < n_experts, logits, neg)
    v1 = jnp.max(lg, axis=-1, keepdims=True)
    i1 = jnp.min(jnp.where(lg == v1, lane, float(LANES)), axis=-1, keepdims=True)
    lg2 = jnp.where(lane == i1, neg, lg)
    v2 = jnp.max(lg2, axis=-1, keepdims=True)
    i2 = jnp.min(jnp.where(lg2 == v2, lane, float(LANES)), axis=-1, keepdims=True)
    e2 = jnp.exp(v2 - v1)
    den = 1.0 + e2
    first = lane == i1
    second = lane == i2
    sel = jnp.where(jnp.logical_or(first, second), 1.0, 0.0)
    comb = jnp.where(first, 1.0 / den, 0.0) + jnp.where(second, e2 / den, 0.0)
    return sel, comb


def _moe_kernel(h_ref, mod_ref, g_ref, fg_ref, rw_ref, w13_ref, w2_ref, o_ref,
                u_ref, x_ref, y_ref, rank_ref, rank_t_ref, comb_ref, cnt_ref, *, n_experts, final_norm):
    e = pl.program_id(2)
    f = pl.program_id(3)
    n_sub = MOE_BLOCK // MOE_SUB

    @pl.when(jnp.logical_and(e == 0, f == 0))
    def _route():
        r = lax.broadcasted_iota(jnp.int32, (ROW_TILE, ROW_TILE), 0)
        c = lax.broadcasted_iota(jnp.int32, (ROW_TILE, ROW_TILE), 1)
        before = jnp.where(c < r, 1.0, 0.0).astype(BF16)
        count = jnp.zeros((1, LANES), F32)
        for t in range(MOE_BLOCK // ROW_TILE):
            rows = slice(t * ROW_TILE, (t + 1) * ROW_TILE)
            u = _norm_mod(h_ref[rows, :], g_ref[...], mod_ref[0:1, :], mod_ref[1:2, :])
            u_ref[rows, :] = u.astype(BF16)
            sel, comb = _top2_routing(_dot(u, rw_ref[...]), n_experts)
            rank = jnp.dot(before, sel.astype(BF16), preferred_element_type=F32) + count
            rank_ref[rows, :] = jnp.where(sel > 0.0, rank, -1.0)
            comb_ref[rows, :] = comb
            count = count + jnp.sum(sel, axis=0, keepdims=True)
        cnt_ref[...] = jnp.broadcast_to(count, cnt_ref.shape)
        rank_t_ref[...] = rank_ref[...].T
        o_ref[...] = jnp.zeros_like(o_ref)

    lane = lax.broadcasted_iota(jnp.int32, (1, LANES), 1)
    n_rows = jnp.sum(jnp.where(lane == e, cnt_ref[0:1, :], 0.0)).astype(jnp.int32)
    n_tiles = (n_rows + MOE_TILE - 1) // MOE_TILE
    rank_row = rank_t_ref[pl.ds(e, 1), :]

    def sub_hits(row0, s):
        rk = rank_row[:, s * MOE_SUB:(s + 1) * MOE_SUB]
        lo = row0.astype(F32)
        inside = jnp.logical_and(rk >= lo, rk < lo + MOE_TILE)
        return rk, jnp.max(jnp.where(inside, 1.0, 0.0)) > 0.0

    @pl.when(f == 0)
    def _compact():
        def body(i, carry):
            row0 = pl.multiple_of(i * MOE_TILE, MOE_ROW_ALIGN)
            ids = (row0 + lax.broadcasted_iota(jnp.int32, (MOE_TILE, 1), 0)).astype(F32)
            x_ref[pl.ds(row0, MOE_TILE), :] = jnp.zeros((MOE_TILE, x_ref.shape[1]), BF16)
            for s in range(n_sub):
                rk, hit = sub_hits(row0, s)

                @pl.when(hit)
                def _():
                    pick = jnp.where(rk == ids, 1.0, 0.0).astype(BF16)
                    got = jnp.dot(pick, u_ref[s * MOE_SUB:(s + 1) * MOE_SUB, :], preferred_element_type=F32)
                    x_ref[pl.ds(row0, MOE_TILE), :] += got.astype(BF16)
            return carry

        lax.fori_loop(0, n_tiles, body, 0)

    def expert_body(i, carry):
        row0 = pl.multiple_of(i * MOE_TILE, MOE_ROW_ALIGN)
        x = x_ref[pl.ds(row0, MOE_TILE), :]
        up = jnp.dot(x, w13_ref[...], preferred_element_type=F32)
        fc = up.shape[1] // 2
        y = _dot(_silu(up[:, 0:fc]) * up[:, fc:2 * fc], w2_ref[...])

        @pl.when(f == 0)
        def _():
            y_ref[pl.ds(row0, MOE_TILE), :] = y

        @pl.when(f > 0)
        def _():
            y_ref[pl.ds(row0, MOE_TILE), :] += y

        return carry

    lax.fori_loop(0, n_tiles, expert_body, 0)

    @pl.when(f == MOE_FCHUNKS - 1)
    def _expand():
        lanes = lax.broadcasted_iota(jnp.int32, (MOE_SUB, LANES), 1)

        def body(i, carry):
            row0 = pl.multiple_of(i * MOE_TILE, MOE_ROW_ALIGN)
            yt = y_ref[pl.ds(row0, MOE_TILE), :].astype(BF16)
            ids = (row0 + lax.broadcasted_iota(jnp.int32, (1, MOE_TILE), 1)).astype(F32)
            for s in range(n_sub):
                _, hit = sub_hits(row0, s)

                @pl.when(hit)
                def _():
                    rows = slice(s * MOE_SUB, (s + 1) * MOE_SUB)
                    rk_col = jnp.sum(jnp.where(lanes == e, rank_ref[rows, :], 0.0), axis=1, keepdims=True)
                    w_col = jnp.sum(jnp.where(lanes == e, comb_ref[rows, :], 0.0), axis=1, keepdims=True)
                    put = jnp.where(rk_col == ids, 1.0, 0.0).astype(BF16)
                    o_ref[rows, :] += w_col * jnp.dot(put, yt, preferred_element_type=F32)
            return carry

        lax.fori_loop(0, n_tiles, body, 0)

    @pl.when(jnp.logical_and(e == n_experts - 1, f == MOE_FCHUNKS - 1))
    def _finish():
        for t in range(MOE_BLOCK // ROW_TILE):
            rows = slice(t * ROW_TILE, (t + 1) * ROW_TILE)
            out = h_ref[rows, :] + mod_ref[2:3, :] * o_ref[rows, :]
            if final_norm:
                out = out * lax.rsqrt(jnp.mean(out * out, axis=-1, keepdims=True) + EPS) * fg_ref[...]
            o_ref[rows, :] = out


def _moe(h, mod, norm_g, final_g, router_w, w1, w3, w2, final_norm):
    n_batch, s, d = h.shape
    n_exp, _, f = w1.shape
    assert s % MOE_BLOCK == 0 and f % (MOE_FCHUNKS * LANES) == 0 and n_exp <= 8
    fc = f // MOE_FCHUNKS
    blk = lambda b, j, e, c: (b, j, 0)
    single = pl.Buffered(1)
    w13 = jnp.concatenate([w1.reshape(n_exp, d, MOE_FCHUNKS, fc), w3.reshape(n_exp, d, MOE_FCHUNKS, fc)],
                          axis=-1).reshape(n_exp, d, 2 * f)
    return pl.pallas_call(
        functools.partial(_moe_kernel, n_experts=n_exp, final_norm=final_norm),
        grid=(n_batch, s // MOE_BLOCK, n_exp, MOE_FCHUNKS),
        in_specs=[pl.BlockSpec((None, MOE_BLOCK, d), blk, pipeline_mode=single),
                  pl.BlockSpec((None, 3, d), lambda b, j, e, c: (b, 0, 0)),
                  _const_spec((1, d)), _const_spec((1, d)), _const_spec(router_w.shape),
                  pl.BlockSpec((None, d, 2 * fc), lambda b, j, e, c: (e, 0, c)),
                  pl.BlockSpec((None, fc, d), lambda b, j, e, c: (e, c, 0))],
        out_specs=pl.BlockSpec((None, MOE_BLOCK, d), blk, pipeline_mode=single),
        out_shape=jax.ShapeDtypeStruct((n_batch, s, d), F32),
        scratch_shapes=[pltpu.VMEM((MOE_BLOCK, d), BF16),
                        pltpu.VMEM((MOE_ROWS, d), BF16),
                        pltpu.VMEM((MOE_ROWS, d), F32),
                        pltpu.VMEM((MOE_BLOCK, LANES), F32),
                        pltpu.VMEM((LANES, MOE_BLOCK), F32),
                        pltpu.VMEM((MOE_BLOCK, LANES), F32),
                        pltpu.VMEM((HALO, LANES), F32)],
        compiler_params=_cparams(("parallel", "parallel", "arbitrary", "arbitrary")),
        name="moe_top2",
    )(h, mod, norm_g.reshape(1, d), final_g.reshape(1, d), router_w, w13, w2)


def _block_diag(w):
    g, n, _ = w.shape
    out = jnp.zeros((g * n, g * n), w.dtype)
    for i in range(g):
        out = out.at[i * n:(i + 1) * n, i * n:(i + 1) * n].set(w[i])
    return out


def kernel(x, c, ctx, c_ctx, ada_w, ada_b, norm1_g, norm2_g, w_in, attn_lambda, attn_subln_g, dn_conv_w,
           dn_a_log, dn_dt_bias, dn_norm_g, pool_w, pool_scale, w_branch, w_out, ffn_w1, ffn_w3, ffn_w2,
           router_w, moe_w1, moe_w3, moe_w2, final_norm_g):
    n_batch, seq, d = x.shape
    n_ctx = ctx.shape[1]
    depth = ada_w.shape[0]
    assert n_ctx == ROW_TILE and seq % ROW_TILE == 0 and seq % GRID_W == 0 and d == 1024
    t = n_ctx + seq
    tpb = t // ROW_TILE

    h = jnp.concatenate([ctx, x], axis=1)
    c_rows = jnp.concatenate([c, c_ctx[None, :], jnp.zeros((8 - n_batch - 1, d), F32)], axis=0)
    mods = _ada_mod(c_rows, ada_w, ada_b).reshape(depth, 8, 6, d)
    rope_tabs = _rope_tables(n_ctx, seq)

    sizes = (ATT_W, ATT_W, ATT_W, DN_W, DN_W, DN_W, DN_W, 2 * DN_HEADS, 2 * DN_HEADS, POOL_W, 3 * d)
    offs = np.concatenate([[0], np.cumsum(sizes)]).tolist()

    for l in range(depth):
        last = l == depth - 1
        first_tile = 1 if last else 0
        lambda_init = 0.8 - 0.6 * math.exp(-0.3 * l)
        wl = w_in[l].astype(BF16)
        seg = lambda a, b: wl[:, offs[a]:offs[b]]
        w_ab = jnp.concatenate([seg(7, 9), jnp.zeros((d, LANES - 4 * DN_HEADS), BF16)], axis=1)
        weights = [seg(0, 1), seg(1, 2), seg(2, 3), seg(3, 7), w_ab, seg(9, 10), seg(10, 11)]
        mod = mods[l, :n_batch + 1]
        q, k, v, dn, ab, pool, gates = _in_projection(
            h.reshape(n_batch * t, d), mod[:, 0:2], norm1_g[l], rope_tabs, weights, n_batch, tpb)
        q, k, v = (a.reshape(n_batch, t, ATT_W) for a in (q, k, v))
        att, att_ctx = _diff_attention(q, k, v, attn_lambda[l], attn_subln_g[l], lambda_init, n_ctx, not last)
        dn = dn.reshape(n_batch, t, 4 * DN_W)
        prep = _dn_prep(dn, ab.reshape(n_batch, t, LANES), dn_conv_w[l], dn_a_log[l], dn_dt_bias[l], tpb)
        o_f, o_b = _dn_scan(prep, n_ctx)
        wb = w_branch[l].astype(BF16)
        h = _merge(h, mod[:, 2:3], att, att_ctx, o_f, o_b, dn, dn_norm_g[l], pool.reshape(n_batch, t, POOL_W),
                   _block_diag(pool_w[l]).astype(BF16), pool_scale[l], gates.reshape(n_batch, t, 3 * d),
                   wb[:ATT_W], wb[ATT_W:ATT_W + DN_W], wb[ATT_W + DN_W:], w_out[l].astype(BF16),
                   first_tile, n_ctx)
        i = l // 2
        if l % 2 == 0:
            h = _ffn(h, mod[:, 3:6], norm2_g[l], final_norm_g, None, ffn_w1[i][None].astype(BF16),
                     ffn_w3[i][None].astype(BF16), ffn_w2[i][None].astype(BF16), not last, last)
        else:
            rw = jnp.concatenate([router_w[i], jnp.zeros((d, LANES - N_EXPERTS), F32)], axis=1).astype(BF16)
            w1, w3, w2 = moe_w1[i].astype(BF16), moe_w3[i].astype(BF16), moe_w2[i].astype(BF16)
            if last:
                h = _moe(h, mod[:n_batch, 3:6], norm2_g[l], final_norm_g, rw, w1, w3, w2, True)
            else:
                h = _ffn(h, mod[:, 3:6], norm2_g[l], final_norm_g, rw, w1, w3, w2, True, False)
    return h
```

```python
import functools
import math

import numpy as np
import jax
import jax.numpy as jnp
from jax import lax
from jax.experimental import pallas as pl
from jax.experimental.pallas import tpu as pltpu

F32 = jnp.float32
BF16 = jnp.bfloat16

EPS = 1e-6
GRID_W = 64
ROPE_BASE = 10000.0
ATT_HEADS = 4
ATT_DH = 64
ATT_VD = 2 * ATT_DH
ATT_W = ATT_HEADS * ATT_VD
DN_HEADS = 4
DN_DH = 64
DN_W = DN_HEADS * DN_DH
DN_CONV = 5
DN_CHUNK = 64
DN_SUB = 16
POOL_WINDOWS = (2, 4, 8, 16)
POOL_GD = 64
POOL_W = len(POOL_WINDOWS) * POOL_GD
N_EXPERTS = 8
TOP_K = 2

ROW_TILE = 256
HALO = 8
LANES = 128
VMEM_LIMIT = 56 * 1024 * 1024


def _cparams(sem):
    return pltpu.CompilerParams(dimension_semantics=sem, vmem_limit_bytes=VMEM_LIMIT)


def _dot(a, b):
    return jnp.dot(a.astype(BF16), b.astype(BF16), preferred_element_type=F32)


def _dot_nt(a, b):
    return lax.dot_general(a.astype(BF16), b.astype(BF16), (((1,), (1,)), ((), ())),
                           preferred_element_type=F32)


def _bmm(a, b):
    return lax.dot_general(a.astype(BF16), b.astype(BF16), (((2,), (1,)), ((0,), (0,))),
                           preferred_element_type=F32)


def _bmm_nt(a, b):
    return lax.dot_general(a.astype(BF16), b.astype(BF16), (((2,), (2,)), ((0,), (0,))),
                           preferred_element_type=F32)


def _sigmoid(x):
    return 1.0 / (1.0 + jnp.exp(-x))


def _silu(x):
    return x * _sigmoid(x)


def _const_spec(shape):
    nd = len(shape)
    return pl.BlockSpec(shape, lambda *_: (0,) * nd)


def _ada_kernel(c_ref, w_ref, b_ref, o_ref):
    o_ref[...] = _dot(_silu(c_ref[...]), w_ref[...]) + b_ref[...]


def _ada_mod(c_rows, ada_w, ada_b):
    depth, d, six_d = ada_w.shape
    n = six_d // d
    rows = c_rows.shape[0]
    return pl.pallas_call(
        _ada_kernel,
        grid=(depth, n),
        in_specs=[pl.BlockSpec((rows, d), lambda l, j: (0, 0)),
                  pl.BlockSpec((None, d, d), lambda l, j: (l, 0, j)),
                  pl.BlockSpec((None, 1, d), lambda l, j: (l, 0, j))],
        out_specs=pl.BlockSpec((None, rows, d), lambda l, j: (l, 0, j)),
        out_shape=jax.ShapeDtypeStruct((depth, rows, six_d), F32),
        compiler_params=_cparams(("parallel", "parallel")),
        name="ada_mod",
    )(c_rows, ada_w, ada_b.reshape(depth, 1, six_d))


def _norm_mod(x, g, shift, scale):
    y = x * lax.rsqrt(jnp.mean(x * x, axis=-1, keepdims=True) + EPS) * g
    return y * (1.0 + scale) + shift


def _inproj_kernel(*refs, tiles_per_batch, split_in):
    if split_in:
        ctx_ref, x_ref = refs[:2]
        (mod_ref, g_ref, cos_ref, su_ref, sd_ref, wq_ref, wk_ref, wv_ref, wdn_ref, wab_ref, wpool_ref, wg_ref,
         q_ref, k_ref, v_ref, dn_ref, ab_ref, pool_ref, gate_ref, h_out_ref) = refs[2:]
        h = jnp.where(pl.program_id(0) % tiles_per_batch == 0, ctx_ref[...], x_ref[...])
        h_out_ref[...] = h
    else:
        (h_ref, mod_ref, g_ref, cos_ref, su_ref, sd_ref, wq_ref, wk_ref, wv_ref, wdn_ref, wab_ref, wpool_ref,
         wg_ref, q_ref, k_ref, v_ref, dn_ref, ab_ref, pool_ref, gate_ref) = refs
        h = h_ref[...]
    u = _norm_mod(h, g_ref[...], mod_ref[0:1, :], mod_ref[1:2, :]).astype(BF16)
    reps = ATT_W // LANES
    cos = jnp.tile(cos_ref[...], (1, reps))
    s_up = jnp.tile(su_ref[...], (1, reps))
    s_dn = jnp.tile(sd_ref[...], (1, reps))
    quarter = ATT_DH // 4

    def rope(t):
        return (t * cos + pltpu.roll(t, ATT_W - quarter, 1) * s_up + pltpu.roll(t, quarter, 1) * s_dn)

    q = rope(jnp.dot(u, wq_ref[...], preferred_element_type=F32))
    q_ref[...] = (q * (ATT_DH ** -0.5 * LOG2E)).astype(q_ref.dtype)
    k_ref[...] = rope(jnp.dot(u, wk_ref[...], preferred_element_type=F32)).astype(k_ref.dtype)
    v_ref[...] = jnp.dot(u, wv_ref[...], preferred_element_type=F32).astype(v_ref.dtype)
    dn_ref[...] = jnp.dot(u, wdn_ref[...], preferred_element_type=F32)
    ab_ref[...] = jnp.dot(u, wab_ref[...], preferred_element_type=F32)
    pool_ref[...] = jnp.dot(u, wpool_ref[...], preferred_element_type=F32)
    gate_ref[...] = _sigmoid(jnp.dot(u, wg_ref[...], preferred_element_type=F32)).astype(gate_ref.dtype)


def _in_projection(h, mod, norm_g, rope_tabs, weights, n_batch, tiles_per_batch):
    split_in = isinstance(h, tuple)
    tpb = tiles_per_batch
    d = mod.shape[-1]
    n_tiles = n_batch * tpb
    rows = n_tiles * ROW_TILE
    widths = [w.shape[1] for w in weights]
    out_dtypes = [BF16, BF16, BF16, F32, F32, F32, BF16]

    def mod_idx(i):
        return (jnp.where(i % tpb == 0, n_batch, i // tpb), 0, 0)

    if split_in:
        h_args = list(h)
        in_specs = [pl.BlockSpec((None, ROW_TILE, d), lambda i: (i // tpb, 0, 0)),
                    pl.BlockSpec((None, ROW_TILE, d), lambda i: (i // tpb, jnp.maximum(i % tpb - 1, 0), 0))]
        widths.append(d)
        out_dtypes.append(F32)
    else:
        h_args = [h]
        in_specs = [pl.BlockSpec((ROW_TILE, d), lambda i: (i, 0))]
    in_specs += [pl.BlockSpec((None, 2, d), mod_idx), _const_spec((1, d))]
    in_specs += [pl.BlockSpec((ROW_TILE, LANES), lambda i: (i % tpb, 0)) for _ in range(3)]
    in_specs += [pl.BlockSpec(w.shape, lambda i: (0, 0), pipeline_mode=pl.Buffered(1)) for w in weights]
    out_specs = [pl.BlockSpec((ROW_TILE, w), lambda i: (i, 0)) for w in widths]
    out_shape = [jax.ShapeDtypeStruct((rows, w), dt) for w, dt in zip(widths, out_dtypes)]
    return pl.pallas_call(
        functools.partial(_inproj_kernel, tiles_per_batch=tpb, split_in=split_in),
        grid=(n_tiles,),
        in_specs=in_specs,
        out_specs=out_specs,
        out_shape=out_shape,
        compiler_params=_cparams(("parallel",)),
        name="in_projection",
    )(*h_args, mod, norm_g.reshape(1, d), *rope_tabs, *weights)


def _rope_tables(n_ctx, seq):
    rows = seq // GRID_W
    row = jnp.repeat(jnp.arange(rows), GRID_W).astype(F32)
    col = jnp.tile(jnp.arange(GRID_W), rows).astype(F32)
    n_freq = ATT_DH // 4
    inv = ROPE_BASE ** (-jnp.arange(n_freq, dtype=F32) / n_freq)
    ar = row[:, None] * inv
    ac = col[:, None] * inv
    cr, sr, cc, sc = jnp.cos(ar), jnp.sin(ar), jnp.cos(ac), jnp.sin(ac)
    z = jnp.zeros_like(sr)
    cos = jnp.concatenate([cr, cr, cc, cc], axis=-1)
    s_up = jnp.concatenate([-sr, z, -sc, z], axis=-1)
    s_dn = jnp.concatenate([z, sr, z, sc], axis=-1)

    def full(t, ctx_val):
        t = jnp.tile(t, (1, LANES // ATT_DH))
        return jnp.concatenate([jnp.full((n_ctx, LANES), ctx_val, F32), t], axis=0)

    return full(cos, 1.0), full(s_up, 0.0), full(s_dn, 0.0)


ATT_KEY_TILE = 512
LOG2E = math.log2(math.e)


ATT_STREAMS = 2


class _AttnStream:
    def __init__(self, q, e_ref, k_ref, v_ref):
        lane = lax.broadcasted_iota(jnp.int32, q.shape, 1)
        zero = jnp.zeros_like(q)
        self.qm = (jnp.where(lane < ATT_DH, q, zero), jnp.where(lane >= ATT_DH, q, zero))
        self.e_ref, self.k_ref, self.v_ref = e_ref, k_ref, v_ref
        tq = q.shape[0]
        self.m_part = [jnp.full((tq, LANES), -jnp.inf, F32)] * 2
        self.l_part = [jnp.zeros((tq, LANES), F32)] * 2
        self.acc = jnp.zeros((tq, ATT_VD), F32)

    @staticmethod
    def _lane_fold(x, op):
        out = x[:, 0:LANES]
        for c in range(1, x.shape[1] // LANES):
            out = op(out, x[:, c * LANES:(c + 1) * LANES])
        return out

    def scores(self, k0, w):
        kt = self.k_ref[k0:k0 + w, :]
        for m in range(2):
            s = _dot_nt(self.qm[m], kt)
            self.e_ref[m, :, k0:k0 + w] = s
            self.m_part[m] = jnp.maximum(self.m_part[m], self._lane_fold(s, jnp.maximum))

    def end_scores(self):
        self.mx = [jnp.max(p, axis=-1, keepdims=True) for p in self.m_part]

    def exps(self, k0, w):
        for m in range(2):
            e = jnp.exp2(self.e_ref[m, :, k0:k0 + w] - self.mx[m])
            self.e_ref[m, :, k0:k0 + w] = e
            self.l_part[m] = self.l_part[m] + self._lane_fold(e, jnp.add)

    def end_exps(self, lam):
        self.inv0 = 1.0 / jnp.sum(self.l_part[0], axis=-1, keepdims=True)
        self.inv1 = lam / jnp.sum(self.l_part[1], axis=-1, keepdims=True)

    def values(self, k0, w):
        a = self.e_ref[0, :, k0:k0 + w] * self.inv0 - self.e_ref[1, :, k0:k0 + w] * self.inv1
        self.acc = self.acc + jnp.dot(a.astype(BF16), self.v_ref[k0:k0 + w, :], preferred_element_type=F32)

    def result(self, g, lambda_init):
        y = self.acc * lax.rsqrt(jnp.mean(self.acc * self.acc, axis=-1, keepdims=True) + EPS) * g
        return y * (1.0 - lambda_init)


def _attn_run(streams, tiles, lam, g, lambda_init, outs):
    passes = ("scores", "exps", "values")
    for step in range(len(passes) + len(streams) - 1):
        active = [(st, passes[step - i]) for i, st in enumerate(streams) if 0 <= step - i < len(passes)]
        for k0, w in tiles:
            for st, name in active:
                getattr(st, name)(k0, w)
        for st, name in active:
            if name == "scores":
                st.end_scores()
            elif name == "exps":
                st.end_exps(lam)
    for st, (o_ref, rows) in zip(streams, outs):
        o_ref[rows, :] = st.result(g, lambda_init).astype(o_ref.dtype)


def _attn_kernel(*refs, lambda_init, n_ctx, ctx_out):
    if ctx_out:
        lam_ref, g_ref, q_ref, qc_ref, k_ref, v_ref, o_ref, oc_ref, e_ref = refs
    else:
        lam_ref, g_ref, q_ref, k_ref, v_ref, o_ref, e_ref = refs
    n_keys = k_ref.shape[0]
    ctx_tiles = ((0, n_ctx),)
    all_tiles = ctx_tiles + tuple((k0, ATT_KEY_TILE) for k0 in range(n_ctx, n_keys, ATT_KEY_TILE))
    lp = lam_ref[...]
    lam = (jnp.exp(jnp.sum(lp[0:1] * lp[1:2], keepdims=True))
           - jnp.exp(jnp.sum(lp[2:3] * lp[3:4], keepdims=True)) + lambda_init)
    g = g_ref[...]

    if ctx_out:
        @pl.when(pl.program_id(2) == 0)
        def _():
            st = _AttnStream(qc_ref[...], e_ref.at[0], k_ref, v_ref)
            _attn_run([st], ctx_tiles, lam, g, lambda_init, [(oc_ref, slice(None))])

    streams, outs = [], []
    for i in range(ATT_STREAMS):
        rows = slice(i * ROW_TILE, (i + 1) * ROW_TILE)
        streams.append(_AttnStream(q_ref[0, rows, :], e_ref.at[i], k_ref, v_ref))
        outs.append((o_ref, rows))
    _attn_run(streams, all_tiles, lam, g, lambda_init, outs)


def _diff_attention(q, k, v, lam_params, subln_g, lambda_init, n_ctx, ctx_out):
    n_batch, t, _ = q.shape
    seq = t - n_ctx
    tq = ATT_STREAMS * ROW_TILE
    assert seq % ATT_KEY_TILE == 0 and seq % tq == 0
    kv_spec = pl.BlockSpec((None, t, ATT_VD), lambda b, h, j: (b, 0, h))
    ctx_spec = pl.BlockSpec((None, n_ctx, ATT_VD), lambda b, h, j: (b, 0, h))
    in_specs = [_const_spec(lam_params.shape), _const_spec((1, ATT_VD)),
                pl.BlockSpec((pl.Element(1), pl.Element(tq), pl.Element(ATT_VD)),
                             lambda b, h, j: (b, pl.multiple_of(n_ctx + j * tq, ROW_TILE),
                                              pl.multiple_of(h * ATT_VD, LANES)))]
    out_specs = [pl.BlockSpec((None, tq, ATT_VD), lambda b, h, j: (b, j, h))]
    out_shape = [jax.ShapeDtypeStruct((n_batch, seq, ATT_W), BF16)]
    args = [lam_params, subln_g.reshape(1, ATT_VD), q]
    if ctx_out:
        in_specs.append(ctx_spec)
        args.append(q)
        out_specs.append(ctx_spec)
        out_shape.append(jax.ShapeDtypeStruct((n_batch, n_ctx, ATT_W), BF16))
    kern = functools.partial(_attn_kernel, lambda_init=lambda_init, n_ctx=n_ctx, ctx_out=ctx_out)
    outs = pl.pallas_call(
        kern,
        grid=(n_batch, ATT_HEADS, seq // tq),
        in_specs=in_specs + [kv_spec, kv_spec],
        out_specs=out_specs,
        out_shape=out_shape,
        scratch_shapes=[pltpu.VMEM((ATT_STREAMS, 2, ROW_TILE, t), F32)],
        compiler_params=_cparams(("parallel", "parallel", "arbitrary")),
        name="diff_attention",
    )(*args, k, v)
    return outs[0], (outs[1] if ctx_out else None)


def _halo_specs(width, tiles_per_batch):
    per = ROW_TILE // HALO
    n_blocks = None

    def prev_idx(b, j):
        return (b, jnp.maximum(j * per - 1, 0), 0)

    def next_idx(b, j):
        return (b, jnp.minimum((j + 1) * per, tiles_per_batch * per - 1), 0)

    del n_blocks
    return (pl.BlockSpec((None, HALO, width), prev_idx), pl.BlockSpec((None, HALO, width), next_idx))


def _with_halo(prev_ref, cur, next_ref, width, tiles_per_batch, j):
    left_ok = jnp.where(j > 1, 1.0, 0.0)
    right_ok = jnp.where(jnp.logical_and(j > 0, j < tiles_per_batch - 1), 1.0, 0.0)
    return jnp.concatenate([prev_ref[:, 0:width] * left_ok, cur, next_ref[:, 0:width] * right_ok], axis=0)


def _dn_prep_kernel(dn_ref, prev_ref, next_ref, ab_ref, cw_ref, alog_ref, dtb_ref,
                    u_ref, w_ref, qg_ref, qk_ref, kt_ref, gl_ref, *, tiles_per_batch):
    j = pl.program_id(1)
    width = 3 * DN_W
    xe = _with_halo(prev_ref, dn_ref[:, 0:width], next_ref, width, tiles_per_batch, j)
    half = DN_CONV // 2
    acc = None
    for tap in range(DN_CONV):
        lo = HALO - half + tap
        term = xe[lo:lo + ROW_TILE, :] * cw_ref[tap:tap + 1, :]
        acc = term if acc is None else acc + term
    qkv = _silu(acc)

    def l2n(x):
        return x * lax.rsqrt(jnp.sum(x * x, axis=-1, keepdims=True) + EPS)

    heads = range(DN_HEADS)
    qn = [l2n(qkv[:, h * DN_DH:(h + 1) * DN_DH]) * (DN_DH ** -0.5) for h in heads]
    kn = [l2n(qkv[:, DN_W + h * DN_DH:DN_W + (h + 1) * DN_DH]) for h in heads]
    vv = [qkv[:, 2 * DN_W + h * DN_DH:2 * DN_W + (h + 1) * DN_DH] for h in heads]
    kn_t = jnp.concatenate(kn, axis=-1).T

    ab = ab_ref[...]
    x = ab + dtb_ref[...]
    g = -jnp.exp(alog_ref[...]) * (jnp.maximum(x, 0.0) + jnp.log(1.0 + jnp.exp(-jnp.abs(x))))
    n_dh = 2 * DN_HEADS
    beta = _sigmoid(pltpu.roll(ab, LANES - n_dh, 1))
    r = lax.broadcasted_iota(jnp.int32, (ROW_TILE, ROW_TILE), 0)
    c = lax.broadcasted_iota(jnp.int32, (ROW_TILE, ROW_TILE), 1)
    same_chunk = (r >> 6) == (c >> 6)
    tri_f = jnp.where(jnp.logical_and(same_chunk, c <= r), 1.0, 0.0)
    tri_b = jnp.where(jnp.logical_and(same_chunk, c >= r), 1.0, 0.0)
    hp = lax.Precision.HIGHEST
    lane = lax.broadcasted_iota(jnp.int32, g.shape, 1)
    gcum = jnp.where(lane < DN_HEADS,
                     jnp.dot(tri_f, g, precision=hp, preferred_element_type=F32),
                     jnp.dot(tri_b, g, precision=hp, preferred_element_type=F32))
    gcum_t = gcum.T

    ri = lax.broadcasted_iota(jnp.int32, (DN_CHUNK, DN_CHUNK), 0)
    ci = lax.broadcasted_iota(jnp.int32, (DN_CHUNK, DN_CHUNK), 1)
    eye = jnp.where(ri == ci, 1.0, 0.0)
    same_sub = (ri >> 4) == (ci >> 4)
    n_chunks = ROW_TILE // DN_CHUNK

    for ch in range(n_chunks):
        rows = slice(ch * DN_CHUNK, (ch + 1) * DN_CHUNK)
        gram = [_dot_nt(kn[h][rows], kn[h][rows]) for h in heads]
        qkm = [_dot_nt(qn[h][rows], kn[h][rows]) for h in heads]
        m_list, rhs_list = [], []
        for d in range(2):
            incl = (ci <= ri) if d == 0 else (ci >= ri)
            strict = (ci < ri) if d == 0 else (ci > ri)
            last = DN_CHUNK - 1 if d == 0 else 0
            for h in heads:
                idx = d * DN_HEADS + h
                g_col = gcum[rows, idx:idx + 1]
                g_row = gcum_t[idx:idx + 1, rows]
                b_col = beta[rows, idx:idx + 1]
                decay = jnp.where(incl, jnp.exp(jnp.minimum(g_col - g_row, 0.0)), 0.0)
                eg = jnp.exp(g_col)
                m_list.append(jnp.where(strict, gram[h] * decay, 0.0) * b_col)
                rhs_list.append(jnp.concatenate([vv[h][rows] * b_col, kn[h][rows] * (b_col * eg)], axis=-1))
                g_last = g_row[:, last:last + 1]
                qg_ref[idx, rows, :] = (qn[h][rows] * eg).astype(qg_ref.dtype)
                qk_ref[idx, rows, :] = (qkm[h] * decay).astype(qk_ref.dtype)
                kt_ref[idx, ch] = (kn_t[h * DN_DH:(h + 1) * DN_DH, rows]
                                   * jnp.exp(g_last - g_row)).astype(kt_ref.dtype)
                gl_ref[idx, ch * HALO:(ch + 1) * HALO, :] = jnp.broadcast_to(jnp.exp(g_last), (HALO, DN_DH))
        m = jnp.stack(m_list)
        rhs = jnp.stack(rhs_list)
        n_diag = jnp.where(same_sub, m, 0.0)
        m_off = m - n_diag
        xp = -n_diag
        dinv = eye + xp
        for _ in range(3):
            xp = _bmm(xp, xp)
            dinv = dinv + _bmm(dinv, xp)
        f = _bmm(dinv, m_off)
        f2 = _bmm(f, f)
        y = _bmm(dinv, rhs)
        y = y - _bmm(f, y)
        sol = y + _bmm(f2, y)
        for idx in range(n_dh):
            u_ref[idx, rows, :] = sol[idx, :, 0:DN_DH]
            w_ref[idx, rows, :] = sol[idx, :, DN_DH:2 * DN_DH].astype(w_ref.dtype)


def _dn_prep(dn, ab, conv_w, a_log, dt_bias, tiles_per_batch):
    n_batch, t, dn_cols = dn.shape
    n_dh = 2 * DN_HEADS
    pad = jnp.zeros((1, LANES - n_dh), F32)
    alog_row = jnp.concatenate([a_log.reshape(1, n_dh).astype(F32), pad], axis=-1)
    dtb_row = jnp.concatenate([dt_bias.reshape(1, n_dh).astype(F32), pad], axis=-1)
    prev_spec, next_spec = _halo_specs(dn_cols, tiles_per_batch)
    tok = lambda b, j: (b, 0, j, 0)
    out_specs = [pl.BlockSpec((None, n_dh, ROW_TILE, DN_DH), tok)] * 4
    out_specs += [pl.BlockSpec((None, n_dh, ROW_TILE // DN_CHUNK, DN_DH, DN_CHUNK), lambda b, j: (b, 0, j, 0, 0)),
                  pl.BlockSpec((None, n_dh, HALO * ROW_TILE // DN_CHUNK, DN_DH), tok)]
    tok_shape = (n_batch, n_dh, t, DN_DH)
    out_shape = [jax.ShapeDtypeStruct(tok_shape, F32),
                 jax.ShapeDtypeStruct(tok_shape, BF16),
                 jax.ShapeDtypeStruct(tok_shape, BF16),
                 jax.ShapeDtypeStruct(tok_shape, BF16),
                 jax.ShapeDtypeStruct((n_batch, n_dh, t // DN_CHUNK, DN_DH, DN_CHUNK), BF16),
                 jax.ShapeDtypeStruct((n_batch, n_dh, HALO * t // DN_CHUNK, DN_DH), F32)]
    return pl.pallas_call(
        functools.partial(_dn_prep_kernel, tiles_per_batch=tiles_per_batch),
        grid=(n_batch, tiles_per_batch),
        in_specs=[pl.BlockSpec((None, ROW_TILE, dn_cols), lambda b, j: (b, j, 0)),
                  prev_spec, next_spec,
                  pl.BlockSpec((None, ROW_TILE, LANES), lambda b, j: (b, j, 0)),
                  _const_spec(conv_w.shape), _const_spec((1, LANES)), _const_spec((1, LANES))],
        out_specs=out_specs,
        out_shape=out_shape,
        compiler_params=_cparams(("parallel", "parallel")),
        name="dn_prep",
    )(dn, dn, dn, ab, conv_w, alog_row, dtb_row)


def _dn_scan_kernel(*refs):
    ins, (of_ref, ob_ref, s_ref) = refs[:12], refs[12:]

    @pl.when(pl.program_id(0) == 0)
    def _():
        s_ref[...] = jnp.zeros_like(s_ref)

    for d, o_ref in enumerate((of_ref, ob_ref)):
        u_ref, w_ref, qg_ref, qk_ref, kt_ref, gl_ref = ins[6 * d:6 * d + 6]
        flat = lambda r: r[...].reshape((-1,) + r.shape[2:])
        s = s_ref[d]
        v_new = flat(u_ref) - _bmm(flat(w_ref), s)
        o = _bmm(flat(qg_ref), s) + _bmm(flat(qk_ref), v_new)
        o_ref[...] = o.reshape(o_ref.shape)
        s_ref[d] = s * flat(gl_ref)[:, 0:1, :] + _bmm(flat(kt_ref), v_new)


def _dn_scan(prep, n_ctx):
    u, w, qg, qk, kt, gl = prep
    n_batch, _, t, _ = u.shape
    n_steps = t // DN_CHUNK
    ctx_chunks = n_ctx // DN_CHUNK

    def chunk_of(d, i):
        if d == 0:
            return i
        return jnp.where(i < ctx_chunks, ctx_chunks - 1 - i, n_steps - 1 + ctx_chunks - i)

    in_specs = []
    for d in range(2):
        tok = functools.partial(lambda i, d: (0, d, chunk_of(d, i), 0), d=d)
        in_specs += [pl.BlockSpec((n_batch, DN_HEADS, DN_CHUNK, DN_DH), tok)] * 4
        in_specs += [pl.BlockSpec((n_batch, DN_HEADS, None, DN_DH, DN_CHUNK),
                                  functools.partial(lambda i, d: (0, d, chunk_of(d, i), 0, 0), d=d)),
                     pl.BlockSpec((n_batch, DN_HEADS, HALO, DN_DH), tok)]
    out_specs = [pl.BlockSpec((n_batch, DN_HEADS, DN_CHUNK, DN_DH),
                              functools.partial(lambda i, d: (0, 0, chunk_of(d, i), 0), d=d))
                 for d in range(2)]
    out_shape = [jax.ShapeDtypeStruct((n_batch, DN_HEADS, t, DN_DH), F32)] * 2
    return pl.pallas_call(
        _dn_scan_kernel,
        grid=(n_steps,),
        in_specs=in_specs,
        out_specs=out_specs,
        out_shape=out_shape,
        scratch_shapes=[pltpu.VMEM((2, n_batch * DN_HEADS, DN_DH, DN_DH), F32)],
        compiler_params=_cparams(("arbitrary",)),
        name="dn_scan",
    )(u, w, qg, qk, kt, gl, u, w, qg, qk, kt, gl)


def _merge_kernel(h_ref, mod_ref, att_ref, attc_ref, of_ref, ob_ref, z_ref, dng_ref,
                  pool_ref, pprev_ref, pnext_ref, pw_ref, ps_ref, gate_ref,
                  wba_ref, wbd_ref, wbp_ref, wo_ref, o_ref, *, tiles_per_batch, first_tile, seq, n_ctx):
    j = pl.program_id(1) + first_tile
    d = h_ref.shape[-1]
    dn_heads = []
    for h in range(DN_HEADS):
        o = of_ref[h] + ob_ref[h]
        dn_heads.append(o * lax.rsqrt(jnp.mean(o * o, axis=-1, keepdims=True) + EPS) * dng_ref[...])
    dn = jnp.concatenate(dn_heads, axis=-1) * _silu(z_ref[...])
    cur = pool_ref[...]
    xe = _with_halo(pprev_ref, cur, pnext_ref, POOL_W, tiles_per_batch, j)
    length = jnp.where(j == 0, n_ctx, seq)
    t0 = jnp.where(j == 0, 0, (j - 1) * ROW_TILE)
    tpos = t0 + lax.broadcasted_iota(jnp.int32, (ROW_TILE, POOL_W), 0)
    lane = lax.broadcasted_iota(jnp.int32, (ROW_TILE, POOL_W), 1)
    n_ext = ROW_TILE + 2 * HALO
    sums = {1: xe}
    win = 1
    while win < max(POOL_WINDOWS):
        prev = sums[win]
        rows = prev.shape[0] - win
        sums[2 * win] = prev[0:rows, :] + prev[win:win + rows, :]
        win *= 2
    del n_ext
    mean = jnp.zeros((ROW_TILE, POOL_W), F32)
    for gi, win in enumerate(POOL_WINDOWS):
        start = HALO - win // 2
        wsum = sums[win][start:start + ROW_TILE, :]
        lo = jnp.clip(tpos - win // 2, 0, length)
        hi = jnp.clip(tpos - win // 2 + win, 0, length)
        mean = jnp.where((lane >> 6) == gi, wsum / (hi - lo).astype(F32), mean)
    pooled = _dot(mean - cur, pw_ref[...]) * ps_ref[...]
    ga = gate_ref[:, 0:d].astype(F32)
    gd = gate_ref[:, d:2 * d].astype(F32)
    gp = gate_ref[:, 2 * d:3 * d].astype(F32)
    att = jnp.where(j == 0, attc_ref[...], att_ref[...])
    mix = (ga * jnp.dot(att, wba_ref[...], preferred_element_type=F32)
           + gd * _dot(dn, wbd_ref[...]) + gp * _dot(pooled, wbp_ref[...]))
    y = _dot(mix, wo_ref[...])
    o_ref[...] = h_ref[...] + mod_ref[0:1, :] * y


def _merge(h, mod_gate, att, att_ctx, o_f, o_b, dn, dn_norm_g, pool, pool_w_bd, pool_scale, gates,
           wb_att, wb_dn, wb_pool, w_out, first_tile, n_ctx):
    n_batch, t, d = h.shape
    tpb = t // ROW_TILE
    n_rows = tpb - first_tile
    seq = t - n_ctx
    ft = first_tile
    prev_spec, next_spec = _halo_specs(POOL_W, tpb)
    shift = lambda f: (lambda b, j: f(b, j + ft))
    prev_spec = pl.BlockSpec(prev_spec.block_shape, shift(prev_spec.index_map))
    next_spec = pl.BlockSpec(next_spec.block_shape, shift(next_spec.index_map))
    row = lambda b, j: (b, j + ft, 0)

    def mod_idx(b, j):
        return (jnp.where(j + ft == 0, n_batch, b), 0, 0)

    weights = [wb_att, wb_dn, wb_pool, w_out]
    kern = functools.partial(_merge_kernel, tiles_per_batch=tpb, first_tile=ft, seq=seq, n_ctx=n_ctx)
    return pl.pallas_call(
        kern,
        grid=(n_batch, n_rows),
        in_specs=[pl.BlockSpec((None, ROW_TILE, d), row),
                  pl.BlockSpec((None, 1, d), mod_idx),
                  pl.BlockSpec((None, ROW_TILE, ATT_W), lambda b, j: (b, jnp.maximum(j + ft - 1, 0), 0)),
                  pl.BlockSpec((None, ROW_TILE, ATT_W), lambda b, j: (b, 0, 0)),
                  pl.BlockSpec((None, DN_HEADS, ROW_TILE, DN_DH), lambda b, j: (b, 0, j + ft, 0)),
                  pl.BlockSpec((None, DN_HEADS, ROW_TILE, DN_DH), lambda b, j: (b, 0, j + ft, 0)),
                  pl.BlockSpec((None, ROW_TILE, DN_W), lambda b, j: (b, j + ft, 3)),
                  _const_spec((1, DN_DH)),
                  pl.BlockSpec((None, ROW_TILE, POOL_W), row),
                  prev_spec, next_spec,
                  _const_spec(pool_w_bd.shape), _const_spec((1, POOL_W)),
                  pl.BlockSpec((None, ROW_TILE, 3 * d), row)]
                 + [_const_spec(w.shape) for w in weights],
        out_specs=pl.BlockSpec((None, ROW_TILE, d), lambda b, j: (b, j, 0)),
        out_shape=jax.ShapeDtypeStruct((n_batch, n_rows * ROW_TILE, d), F32),
        compiler_params=_cparams(("parallel", "parallel")),
        name="merge",
    )(h, mod_gate, att, att if att_ctx is None else att_ctx, o_f, o_b, dn, dn_norm_g.reshape(1, DN_DH),
      pool, pool, pool,
      pool_w_bd, pool_scale.reshape(1, POOL_W), gates, *weights)


def _ffn_kernel(*refs, n_experts, n_fchunks, final_norm):
    if n_experts > 1:
        h_ref, mod_ref, g_ref, fg_ref, rw_ref, w1_ref, w3_ref, w2_ref, o_ref, u_ref, acc_ref, comb_ref = refs
    else:
        h_ref, mod_ref, g_ref, fg_ref, w1_ref, w3_ref, w2_ref, o_ref, u_ref, acc_ref = refs
    step = pl.program_id(2)
    e = step // n_fchunks

    @pl.when(step == 0)
    def _():
        u = _norm_mod(h_ref[...], g_ref[...], mod_ref[0:1, :], mod_ref[1:2, :])
        u_ref[...] = u.astype(BF16)
        acc_ref[...] = jnp.zeros_like(acc_ref)
        if n_experts > 1:
            logits = _dot(u, rw_ref[...])
            lane = lax.broadcasted_iota(jnp.int32, logits.shape, 1).astype(F32)
            neg = jnp.float32(-jnp.inf)
            lg = jnp.where(lane < n_experts, logits, neg)
            v1 = jnp.max(lg, axis=-1, keepdims=True)
            i1 = jnp.min(jnp.where(lg == v1, lane, float(LANES)), axis=-1, keepdims=True)
            lg2 = jnp.where(lane == i1, neg, lg)
            v2 = jnp.max(lg2, axis=-1, keepdims=True)
            i2 = jnp.min(jnp.where(lg2 == v2, lane, float(LANES)), axis=-1, keepdims=True)
            e2 = jnp.exp(v2 - v1)
            den = 1.0 + e2
            comb_ref[...] = jnp.where(lane == i1, 1.0 / den, 0.0) + jnp.where(lane == i2, e2 / den, 0.0)

    u = u_ref[...]
    hid = _silu(jnp.dot(u, w1_ref[...], preferred_element_type=F32)) * jnp.dot(
        u, w3_ref[...], preferred_element_type=F32)
    y = _dot(hid, w2_ref[...])
    if n_experts > 1:
        lane = lax.broadcasted_iota(jnp.int32, comb_ref.shape, 1)
        y = y * jnp.sum(jnp.where(lane == e, comb_ref[...], 0.0), axis=-1, keepdims=True)
    acc_ref[...] += y

    @pl.when(step == n_experts * n_fchunks - 1)
    def _():
        out = h_ref[...] + mod_ref[2:3, :] * acc_ref[...]
        if final_norm:
            out = out * lax.rsqrt(jnp.mean(out * out, axis=-1, keepdims=True) + EPS) * fg_ref[...]
        o_ref[...] = out


def _ffn(h, mod, norm_g, final_g, router_w, w1, w3, w2, has_ctx, final_norm):
    n_batch, t, d = h.shape
    n_exp, _, f = w1.shape
    n_rows = t // ROW_TILE
    row = lambda b, j, e: (b, j, 0)

    def mod_idx(b, j, e):
        return (jnp.where(jnp.logical_and(has_ctx, j == 0), n_batch, b), 0, 0)

    in_specs = [pl.BlockSpec((None, ROW_TILE, d), row),
                pl.BlockSpec((None, 3, d), mod_idx),
                _const_spec((1, d)), _const_spec((1, d))]
    args = [h, mod, norm_g.reshape(1, d), final_g.reshape(1, d)]
    scratch = [pltpu.VMEM((ROW_TILE, d), BF16), pltpu.VMEM((ROW_TILE, d), F32)]
    if n_exp > 1:
        in_specs.append(_const_spec(router_w.shape))
        args.append(router_w)
        scratch.append(pltpu.VMEM((ROW_TILE, LANES), F32))
    nfc = 1 if n_exp == 1 else 2
    fc = f // nfc
    assert fc * nfc == f and fc % LANES == 0
    in_specs += [pl.BlockSpec((None, d, fc), lambda b, j, s: (s // nfc, 0, s % nfc)),
                 pl.BlockSpec((None, d, fc), lambda b, j, s: (s // nfc, 0, s % nfc)),
                 pl.BlockSpec((None, fc, d), lambda b, j, s: (s // nfc, s % nfc, 0))]
    args += [w1, w3, w2]
    return pl.pallas_call(
        functools.partial(_ffn_kernel, n_experts=n_exp, n_fchunks=nfc, final_norm=final_norm),
        grid=(n_batch, n_rows, n_exp * nfc),
        in_specs=in_specs,
        out_specs=pl.BlockSpec((None, ROW_TILE, d), row),
        out_shape=jax.ShapeDtypeStruct((n_batch, t, d), F32),
        scratch_shapes=scratch,
        input_output_aliases={0: 0},
        compiler_params=_cparams(("parallel", "parallel", "arbitrary")),
        name="ffn_all_experts" if n_exp > 1 else "ffn_dense",
    )(*args)


MOE_BLOCK = 2048
MOE_SUB = 512
MOE_TILE = 288
MOE_ROW_ALIGN = 32
MOE_ROWS = -(-MOE_BLOCK // MOE_TILE) * MOE_TILE
MOE_PUT_TILE = 256
MOE_FCHUNKS = 4


def _top2_routing(logits, n_experts):
    lane = lax.broadcasted_iota(jnp.int32, logits.shape, 1).astype(F32)
    neg = jnp.float32(-jnp.inf)
    lg = jnp.where(lane < n_experts, logits, neg)
    v1 = jnp.max(lg, axis=-1, keepdims=True)
    i1 = jnp.min(jnp.where(lg == v1, lane, float(LANES)), axis=-1, keepdims=True)
    lg2 = jnp.where(lane == i1, neg, lg)
    v2 = jnp.max(lg2, axis=-1, keepdims=True)
    i2 = jnp.min(jnp.where(lg2 == v2, lane, float(LANES)), axis=-1, keepdims=True)
    e2 = jnp.exp(v2 - v1)
    den = 1.0 + e2
    first = lane == i1
    second = lane == i2
    sel = jnp.where(jnp.logical_or(first, second), 1.0, 0.0)
    comb = jnp.where(first, 1.0 / den, 0.0) + jnp.where(second, e2 / den, 0.0)
    return sel, comb


def _moe_kernel(h_ref, mod_ref, g_ref, fg_ref, rw_ref, w1_ref, w3_ref, w2_ref, o_ref,
                u_ref, x_ref, y_ref, rank_ref, rank_t_ref, comb_ref, cnt_ref, *, n_experts, final_norm):
    e = pl.program_id(2)
    f = pl.program_id(3)
    n_sub = MOE_BLOCK // MOE_SUB

    @pl.when(jnp.logical_and(e == 0, f == 0))
    def _route():
        r = lax.broadcasted_iota(jnp.int32, (ROW_TILE, ROW_TILE), 0)
        c = lax.broadcasted_iota(jnp.int32, (ROW_TILE, ROW_TILE), 1)
        before = jnp.where(c < r, 1.0, 0.0).astype(BF16)
        count = jnp.zeros((1, LANES), F32)
        for t in range(MOE_BLOCK // ROW_TILE):
            rows = slice(t * ROW_TILE, (t + 1) * ROW_TILE)
            u = _norm_mod(h_ref[rows, :], g_ref[...], mod_ref[0:1, :], mod_ref[1:2, :])
            u_ref[rows, :] = u.astype(BF16)
            sel, comb = _top2_routing(_dot(u, rw_ref[...]), n_experts)
            rank = jnp.dot(before, sel.astype(BF16), preferred_element_type=F32) + count
            rank_ref[rows, :] = jnp.where(sel > 0.0, rank, -1.0)
            comb_ref[rows, :] = comb
            count = count + jnp.sum(sel, axis=0, keepdims=True)
        cnt_ref[...] = jnp.broadcast_to(count, cnt_ref.shape)
        rank_t_ref[...] = rank_ref[...].T
        o_ref[...] = jnp.zeros_like(o_ref)

        @pl.when(jnp.logical_and(pl.program_id(0) == 0, pl.program_id(1) == 0))
        def _():
            y_ref[...] = jnp.zeros_like(y_ref)

    lane = lax.broadcasted_iota(jnp.int32, (1, LANES), 1)
    n_rows = jnp.sum(jnp.where(lane == e, cnt_ref[0:1, :], 0.0)).astype(jnp.int32)
    n_tiles = (n_rows + MOE_TILE - 1) // MOE_TILE
    rank_row = rank_t_ref[pl.ds(e, 1), :]

    def sub_hits(row0, s, tile=MOE_TILE):
        rk = rank_row[:, s * MOE_SUB:(s + 1) * MOE_SUB]
        lo = row0.astype(F32)
        inside = jnp.logical_and(rk >= lo, rk < lo + tile)
        return rk, jnp.max(jnp.where(inside, 1.0, 0.0)) > 0.0

    @pl.when(f == 0)
    def _compact():
        def body(i, carry):
            row0 = pl.multiple_of(i * MOE_TILE, MOE_ROW_ALIGN)
            ids = (row0 + lax.broadcasted_iota(jnp.int32, (MOE_TILE, 1), 0)).astype(F32)
            x_ref[pl.ds(row0, MOE_TILE), :] = jnp.zeros((MOE_TILE, x_ref.shape[1]), BF16)
            for s in range(n_sub):
                rk, hit = sub_hits(row0, s)

                @pl.when(hit)
                def _():
                    pick = jnp.where(rk == ids, 1.0, 0.0).astype(BF16)
                    got = jnp.dot(pick, u_ref[s * MOE_SUB:(s + 1) * MOE_SUB, :], preferred_element_type=F32)
                    x_ref[pl.ds(row0, MOE_TILE), :] += got.astype(BF16)
            return carry

        lax.fori_loop(0, n_tiles, body, 0)

    def expert_body(i, carry):
        row0 = pl.multiple_of(i * MOE_TILE, MOE_ROW_ALIGN)
        x = x_ref[pl.ds(row0, MOE_TILE), :]
        hid = _silu(jnp.dot(x, w1_ref[...], preferred_element_type=F32)) * jnp.dot(
            x, w3_ref[...], preferred_element_type=F32)
        y = _dot(hid, w2_ref[...])

        @pl.when(f == 0)
        def _():
            y_ref[pl.ds(row0, MOE_TILE), :] = y

        @pl.when(f > 0)
        def _():
            y_ref[pl.ds(row0, MOE_TILE), :] += y

        return carry

    lax.fori_loop(0, n_tiles, expert_body, 0)

    @pl.when(f == MOE_FCHUNKS - 1)
    def _expand():
        lanes = lax.broadcasted_iota(jnp.int32, (MOE_SUB, LANES), 1)

        def body(i, carry):
            row0 = pl.multiple_of(i * MOE_PUT_TILE, MOE_PUT_TILE)
            yt = y_ref[pl.ds(row0, MOE_PUT_TILE), :].astype(BF16)
            ids = (row0 + lax.broadcasted_iota(jnp.int32, (1, MOE_PUT_TILE), 1)).astype(F32)
            for s in range(n_sub):
                _, hit = sub_hits(row0, s, MOE_PUT_TILE)

                @pl.when(hit)
                def _():
                    rows = slice(s * MOE_SUB, (s + 1) * MOE_SUB)
                    rk_col = jnp.sum(jnp.where(lanes == e, rank_ref[rows, :], 0.0), axis=1, keepdims=True)
                    w_col = jnp.sum(jnp.where(lanes == e, comb_ref[rows, :], 0.0), axis=1, keepdims=True)
                    put = jnp.where(rk_col == ids, 1.0, 0.0).astype(BF16)
                    o_ref[rows, :] += w_col * jnp.dot(put, yt, preferred_element_type=F32)
            return carry

        lax.fori_loop(0, (n_rows + MOE_PUT_TILE - 1) // MOE_PUT_TILE, body, 0)

    @pl.when(jnp.logical_and(e == n_experts - 1, f == MOE_FCHUNKS - 1))
    def _finish():
        for t in range(MOE_BLOCK // ROW_TILE):
            rows = slice(t * ROW_TILE, (t + 1) * ROW_TILE)
            out = h_ref[rows, :] + mod_ref[2:3, :] * o_ref[rows, :]
            if final_norm:
                out = out * lax.rsqrt(jnp.mean(out * out, axis=-1, keepdims=True) + EPS) * fg_ref[...]
            o_ref[rows, :] = out


def _moe(h, mod, norm_g, final_g, router_w, w1, w3, w2, final_norm):
    n_batch, s, d = h.shape
    n_exp, _, f = w1.shape
    assert s % MOE_BLOCK == 0 and f % (MOE_FCHUNKS * LANES) == 0 and n_exp <= 8
    fc = f // MOE_FCHUNKS
    blk = lambda b, j, e, c: (b, j, 0)
    single = pl.Buffered(1)
    return pl.pallas_call(
        functools.partial(_moe_kernel, n_experts=n_exp, final_norm=final_norm),
        grid=(n_batch, s // MOE_BLOCK, n_exp, MOE_FCHUNKS),
        in_specs=[pl.BlockSpec((None, MOE_BLOCK, d), blk, pipeline_mode=single),
                  pl.BlockSpec((None, 3, d), lambda b, j, e, c: (b, 0, 0)),
                  _const_spec((1, d)), _const_spec((1, d)), _const_spec(router_w.shape),
                  pl.BlockSpec((None, d, fc), lambda b, j, e, c: (e, 0, c)),
                  pl.BlockSpec((None, d, fc), lambda b, j, e, c: (e, 0, c)),
                  pl.BlockSpec((None, fc, d), lambda b, j, e, c: (e, c, 0))],
        out_specs=pl.BlockSpec((None, MOE_BLOCK, d), blk, pipeline_mode=single),
        out_shape=jax.ShapeDtypeStruct((n_batch, s, d), F32),
        scratch_shapes=[pltpu.VMEM((MOE_BLOCK, d), BF16),
                        pltpu.VMEM((MOE_ROWS, d), BF16),
                        pltpu.VMEM((MOE_ROWS, d), F32),
                        pltpu.VMEM((MOE_BLOCK, LANES), F32),
                        pltpu.VMEM((LANES, MOE_BLOCK), F32),
                        pltpu.VMEM((MOE_BLOCK, LANES), F32),
                        pltpu.VMEM((HALO, LANES), F32)],
        compiler_params=_cparams(("arbitrary", "arbitrary", "arbitrary", "arbitrary")),
        name="moe_top2",
    )(h, mod, norm_g.reshape(1, d), final_g.reshape(1, d), router_w, w1, w3, w2)


def _block_diag(w):
    g, n, _ = w.shape
    out = jnp.zeros((g * n, g * n), w.dtype)
    for i in range(g):
        out = out.at[i * n:(i + 1) * n, i * n:(i + 1) * n].set(w[i])
    return out


def kernel(x, c, ctx, c_ctx, ada_w, ada_b, norm1_g, norm2_g, w_in, attn_lambda, attn_subln_g, dn_conv_w,
           dn_a_log, dn_dt_bias, dn_norm_g, pool_w, pool_scale, w_branch, w_out, ffn_w1, ffn_w3, ffn_w2,
           router_w, moe_w1, moe_w3, moe_w2, final_norm_g):
    n_batch, seq, d = x.shape
    n_ctx = ctx.shape[1]
    depth = ada_w.shape[0]
    assert n_ctx == ROW_TILE and seq % ROW_TILE == 0 and seq % GRID_W == 0 and d == 1024
    t = n_ctx + seq
    tpb = t // ROW_TILE

    h = None
    c_rows = jnp.concatenate([c, c_ctx[None, :], jnp.zeros((8 - n_batch - 1, d), F32)], axis=0)
    mods = _ada_mod(c_rows, ada_w, ada_b).reshape(depth, 8, 6, d)
    rope_tabs = _rope_tables(n_ctx, seq)

    sizes = (ATT_W, ATT_W, ATT_W, DN_W, DN_W, DN_W, DN_W, 2 * DN_HEADS, 2 * DN_HEADS, POOL_W, 3 * d)
    offs = np.concatenate([[0], np.cumsum(sizes)]).tolist()

    for l in range(depth):
        last = l == depth - 1
        first_tile = 1 if last else 0
        lambda_init = 0.8 - 0.6 * math.exp(-0.3 * l)
        wl = w_in[l].astype(BF16)
        seg = lambda a, b: wl[:, offs[a]:offs[b]]
        w_ab = jnp.concatenate([seg(7, 9), jnp.zeros((d, LANES - 4 * DN_HEADS), BF16)], axis=1)
        weights = [seg(0, 1), seg(1, 2), seg(2, 3), seg(3, 7), w_ab, seg(9, 10), seg(10, 11)]
        mod = mods[l, :n_batch + 1]
        src = (ctx, x) if l == 0 else h.reshape(n_batch * t, d)
        outs = _in_projection(src, mod[:, 0:2], norm1_g[l], rope_tabs, weights, n_batch, tpb)
        q, k, v, dn, ab, pool, gates = outs[:7]
        if l == 0:
            h = outs[7].reshape(n_batch, t, d)
        q, k, v = (a.reshape(n_batch, t, ATT_W) for a in (q, k, v))
        att, att_ctx = _diff_attention(q, k, v, attn_lambda[l], attn_subln_g[l], lambda_init, n_ctx, not last)
        dn = dn.reshape(n_batch, t, 4 * DN_W)
        prep = _dn_prep(dn, ab.reshape(n_batch, t, LANES), dn_conv_w[l], dn_a_log[l], dn_dt_bias[l], tpb)
        o_f, o_b = _dn_scan(prep, n_ctx)
        wb = w_branch[l].astype(BF16)
        h = _merge(h, mod[:, 2:3], att, att_ctx, o_f, o_b, dn, dn_norm_g[l], pool.reshape(n_batch, t, POOL_W),
                   _block_diag(pool_w[l]).astype(BF16), pool_scale[l], gates.reshape(n_batch, t, 3 * d),
                   wb[:ATT_W], wb[ATT_W:ATT_W + DN_W], wb[ATT_W + DN_W:], w_out[l].astype(BF16),
                   first_tile, n_ctx)
        i = l // 2
        if l % 2 == 0:
            h = _ffn(h, mod[:, 3:6], norm2_g[l], final_norm_g, None, ffn_w1[i][None].astype(BF16),
                     ffn_w3[i][None].astype(BF16), ffn_w2[i][None].astype(BF16), not last, last)
        else:
            rw = jnp.concatenate([router_w[i], jnp.zeros((d, LANES - N_EXPERTS), F32)], axis=1).astype(BF16)
            w1, w3, w2 = moe_w1[i].astype(BF16), moe_w3[i].astype(BF16), moe_w2[i].astype(BF16)
            if last:
                h = _moe(h, mod[:n_batch, 3:6], norm2_g[l], final_norm_g, rw, w1, w3, w2, True)
            else:
                h = _ffn(h, mod[:, 3:6], norm2_g[l], final_norm_g, rw, w1, w3, w2, True, False)
    return h
```

```python
import functools
import math

import numpy as np
import jax
import jax.numpy as jnp
from jax import lax
from jax.experimental import pallas as pl
from jax.experimental.pallas import tpu as pltpu

F32 = jnp.float32
BF16 = jnp.bfloat16

EPS = 1e-6
GRID_W = 64
ROPE_BASE = 10000.0
ATT_HEADS = 4
ATT_DH = 64
ATT_VD = 2 * ATT_DH
ATT_W = ATT_HEADS * ATT_VD
DN_HEADS = 4
DN_DH = 64
DN_W = DN_HEADS * DN_DH
DN_PAIRS = DN_HEADS // 2
DN_CONV = 5
DN_CHUNK = 64
DN_SUB = 16
POOL_WINDOWS = (2, 4, 8, 16)
POOL_GD = 64
POOL_W = len(POOL_WINDOWS) * POOL_GD
N_EXPERTS = 8
TOP_K = 2

ROW_TILE = 256
HALO = 8
LANES = 128
VMEM_LIMIT = 56 * 1024 * 1024


def _cparams(sem):
    return pltpu.CompilerParams(dimension_semantics=sem, vmem_limit_bytes=VMEM_LIMIT)


def _dot(a, b):
    return jnp.dot(a.astype(BF16), b.astype(BF16), preferred_element_type=F32)


def _dot_nt(a, b):
    return lax.dot_general(a.astype(BF16), b.astype(BF16), (((1,), (1,)), ((), ())),
                           preferred_element_type=F32)


def _bmm(a, b):
    return lax.dot_general(a.astype(BF16), b.astype(BF16), (((2,), (1,)), ((0,), (0,))),
                           preferred_element_type=F32)


def _bmm_nt(a, b):
    return lax.dot_general(a.astype(BF16), b.astype(BF16), (((2,), (2,)), ((0,), (0,))),
                           preferred_element_type=F32)


def _sigmoid(x):
    return 1.0 / (1.0 + jnp.exp(-x))


def _silu(x):
    return x * _sigmoid(x)


def _const_spec(shape):
    nd = len(shape)
    return pl.BlockSpec(shape, lambda *_: (0,) * nd)


def _ada_kernel(c_ref, w_ref, b_ref, o_ref):
    o_ref[...] = _dot(_silu(c_ref[...]), w_ref[...]) + b_ref[...]


def _ada_mod(c_rows, ada_w, ada_b):
    depth, d, six_d = ada_w.shape
    n = six_d // d
    rows = c_rows.shape[0]
    return pl.pallas_call(
        _ada_kernel,
        grid=(depth, n),
        in_specs=[pl.BlockSpec((rows, d), lambda l, j: (0, 0)),
                  pl.BlockSpec((None, d, d), lambda l, j: (l, 0, j)),
                  pl.BlockSpec((None, 1, d), lambda l, j: (l, 0, j))],
        out_specs=pl.BlockSpec((None, rows, d), lambda l, j: (l, 0, j)),
        out_shape=jax.ShapeDtypeStruct((depth, rows, six_d), F32),
        compiler_params=_cparams(("parallel", "parallel")),
        name="ada_mod",
    )(c_rows, ada_w, ada_b.reshape(depth, 1, six_d))


def _norm_mod(x, g, shift, scale):
    y = x * lax.rsqrt(jnp.mean(x * x, axis=-1, keepdims=True) + EPS) * g
    return y * (1.0 + scale) + shift


def _inproj_kernel(*refs, tiles_per_batch, split_in):
    if split_in:
        ctx_ref, x_ref = refs[:2]
        (mod_ref, g_ref, cos_ref, su_ref, sd_ref, wq_ref, wk_ref, wv_ref, wdn_ref, wab_ref, wpool_ref, wg_ref,
         q_ref, k_ref, v_ref, dn_ref, ab_ref, pool_ref, gate_ref, h_out_ref) = refs[2:]
        h = jnp.where(pl.program_id(0) % tiles_per_batch == 0, ctx_ref[...], x_ref[...])
        h_out_ref[...] = h
    else:
        (h_ref, mod_ref, g_ref, cos_ref, su_ref, sd_ref, wq_ref, wk_ref, wv_ref, wdn_ref, wab_ref, wpool_ref,
         wg_ref, q_ref, k_ref, v_ref, dn_ref, ab_ref, pool_ref, gate_ref) = refs
        h = h_ref[...]
    u = _norm_mod(h, g_ref[...], mod_ref[0:1, :], mod_ref[1:2, :]).astype(BF16)
    reps = ATT_W // LANES
    cos = jnp.tile(cos_ref[...], (1, reps))
    s_up = jnp.tile(su_ref[...], (1, reps))
    s_dn = jnp.tile(sd_ref[...], (1, reps))
    quarter = ATT_DH // 4

    def rope(t):
        return (t * cos + pltpu.roll(t, ATT_W - quarter, 1) * s_up + pltpu.roll(t, quarter, 1) * s_dn)

    q = rope(jnp.dot(u, wq_ref[...], preferred_element_type=F32))
    k = rope(jnp.dot(u, wk_ref[...], preferred_element_type=F32))
    v = jnp.dot(u, wv_ref[...], preferred_element_type=F32)
    for hd in range(ATT_HEADS):
        cols = slice(hd * ATT_VD, (hd + 1) * ATT_VD)
        q_ref[hd] = (q[:, cols] * (ATT_DH ** -0.5 * LOG2E)).astype(q_ref.dtype)
        k_ref[hd] = k[:, cols].astype(k_ref.dtype)
        v_ref[hd] = v[:, cols].astype(v_ref.dtype)
    dn_ref[...] = jnp.dot(u, wdn_ref[...], preferred_element_type=F32)
    ab_ref[...] = jnp.dot(u, wab_ref[...], preferred_element_type=F32)
    pool_ref[...] = jnp.dot(u, wpool_ref[...], preferred_element_type=F32)
    gate_ref[...] = _sigmoid(jnp.dot(u, wg_ref[...], preferred_element_type=F32)).astype(gate_ref.dtype)


def _in_projection(h, mod, norm_g, rope_tabs, weights, n_batch, tiles_per_batch):
    split_in = isinstance(h, tuple)
    tpb = tiles_per_batch
    d = mod.shape[-1]
    n_tiles = n_batch * tpb
    rows = n_tiles * ROW_TILE
    widths = [w.shape[1] for w in weights]
    out_dtypes = [BF16, BF16, BF16, F32, F32, F32, BF16]

    def mod_idx(i):
        return (jnp.where(i % tpb == 0, n_batch, i // tpb), 0, 0)

    if split_in:
        h_args = list(h)
        in_specs = [pl.BlockSpec((None, ROW_TILE, d), lambda i: (i // tpb, 0, 0)),
                    pl.BlockSpec((None, ROW_TILE, d), lambda i: (i // tpb, jnp.maximum(i % tpb - 1, 0), 0))]
        widths.append(d)
        out_dtypes.append(F32)
    else:
        h_args = [h]
        in_specs = [pl.BlockSpec((ROW_TILE, d), lambda i: (i, 0))]
    in_specs += [pl.BlockSpec((None, 2, d), mod_idx), _const_spec((1, d))]
    in_specs += [pl.BlockSpec((ROW_TILE, LANES), lambda i: (i % tpb, 0)) for _ in range(3)]
    in_specs += [pl.BlockSpec(w.shape, lambda i: (0, 0), pipeline_mode=pl.Buffered(1)) for w in weights]
    out_specs = [pl.BlockSpec((ROW_TILE, w), lambda i: (i, 0)) for w in widths]
    out_shape = [jax.ShapeDtypeStruct((rows, w), dt) for w, dt in zip(widths, out_dtypes)]
    for o in range(3):
        out_specs[o] = pl.BlockSpec((ATT_HEADS, ROW_TILE, ATT_VD), lambda i: (0, i, 0))
        out_shape[o] = jax.ShapeDtypeStruct((ATT_HEADS, rows, ATT_VD), BF16)
    return pl.pallas_call(
        functools.partial(_inproj_kernel, tiles_per_batch=tpb, split_in=split_in),
        grid=(n_tiles,),
        in_specs=in_specs,
        out_specs=out_specs,
        out_shape=out_shape,
        compiler_params=_cparams(("parallel",)),
        name="in_projection",
    )(*h_args, mod, norm_g.reshape(1, d), *rope_tabs, *weights)


def _rope_tables(n_ctx, seq):
    rows = seq // GRID_W
    row = jnp.repeat(jnp.arange(rows), GRID_W).astype(F32)
    col = jnp.tile(jnp.arange(GRID_W), rows).astype(F32)
    n_freq = ATT_DH // 4
    inv = ROPE_BASE ** (-jnp.arange(n_freq, dtype=F32) / n_freq)
    ar = row[:, None] * inv
    ac = col[:, None] * inv
    cr, sr, cc, sc = jnp.cos(ar), jnp.sin(ar), jnp.cos(ac), jnp.sin(ac)
    z = jnp.zeros_like(sr)
    cos = jnp.concatenate([cr, cr, cc, cc], axis=-1)
    s_up = jnp.concatenate([-sr, z, -sc, z], axis=-1)
    s_dn = jnp.concatenate([z, sr, z, sc], axis=-1)

    def full(t, ctx_val):
        t = jnp.tile(t, (1, LANES // ATT_DH))
        return jnp.concatenate([jnp.full((n_ctx, LANES), ctx_val, F32), t], axis=0)

    return full(cos, 1.0), full(s_up, 0.0), full(s_dn, 0.0)


ATT_KEY_TILE = 512
LOG2E = math.log2(math.e)


ATT_STREAMS = 2


class _AttnStream:
    def __init__(self, q, e_ref, k_ref, v_ref):
        lane = lax.broadcasted_iota(jnp.int32, q.shape, 1)
        zero = jnp.zeros_like(q)
        self.qm = (jnp.where(lane < ATT_DH, q, zero), jnp.where(lane >= ATT_DH, q, zero))
        self.e_ref, self.k_ref, self.v_ref = e_ref, k_ref, v_ref
        tq = q.shape[0]
        self.m_part = [jnp.full((tq, LANES), -jnp.inf, F32)] * 2
        self.l_part = [jnp.zeros((tq, LANES), F32)] * 2
        self.acc = jnp.zeros((tq, ATT_VD), F32)

    @staticmethod
    def _lane_fold(x, op):
        out = x[:, 0:LANES]
        for c in range(1, x.shape[1] // LANES):
            out = op(out, x[:, c * LANES:(c + 1) * LANES])
        return out

    def scores(self, k0, w):
        kt = self.k_ref[k0:k0 + w, :]
        for m in range(2):
            s = _dot_nt(self.qm[m], kt)
            self.e_ref[m, :, k0:k0 + w] = s
            self.m_part[m] = jnp.maximum(self.m_part[m], self._lane_fold(s, jnp.maximum))

    def end_scores(self):
        self.mx = [jnp.max(p, axis=-1, keepdims=True) for p in self.m_part]

    def exps(self, k0, w):
        for m in range(2):
            e = jnp.exp2(self.e_ref[m, :, k0:k0 + w] - self.mx[m])
            self.e_ref[m, :, k0:k0 + w] = e
            self.l_part[m] = self.l_part[m] + self._lane_fold(e, jnp.add)

    def end_exps(self, lam):
        self.inv0 = 1.0 / jnp.sum(self.l_part[0], axis=-1, keepdims=True)
        self.inv1 = lam / jnp.sum(self.l_part[1], axis=-1, keepdims=True)

    def values(self, k0, w):
        a = self.e_ref[0, :, k0:k0 + w] * self.inv0 - self.e_ref[1, :, k0:k0 + w] * self.inv1
        self.acc = self.acc + jnp.dot(a.astype(BF16), self.v_ref[k0:k0 + w, :], preferred_element_type=F32)

    def result(self, g, lambda_init):
        y = self.acc * lax.rsqrt(jnp.mean(self.acc * self.acc, axis=-1, keepdims=True) + EPS) * g
        return y * (1.0 - lambda_init)


def _attn_run(streams, tiles, lam, g, lambda_init, outs):
    passes = ("scores", "exps", "values")
    for step in range(len(passes) + len(streams) - 1):
        active = [(st, passes[step - i]) for i, st in enumerate(streams) if 0 <= step - i < len(passes)]
        for k0, w in tiles:
            for st, name in active:
                getattr(st, name)(k0, w)
        for st, name in active:
            if name == "scores":
                st.end_scores()
            elif name == "exps":
                st.end_exps(lam)
    for st, (o_ref, rows) in zip(streams, outs):
        o_ref[rows, :] = st.result(g, lambda_init).astype(o_ref.dtype)


def _attn_kernel(*refs, lambda_init, n_ctx, ctx_out):
    if ctx_out:
        lam_ref, g_ref, q_ref, qc_ref, k_ref, v_ref, o_ref, oc_ref, e_ref = refs
    else:
        lam_ref, g_ref, q_ref, k_ref, v_ref, o_ref, e_ref = refs
    n_keys = k_ref.shape[0]
    ctx_tiles = ((0, n_ctx),)
    all_tiles = ctx_tiles + tuple((k0, ATT_KEY_TILE) for k0 in range(n_ctx, n_keys, ATT_KEY_TILE))
    lp = lam_ref[...]
    lam = (jnp.exp(jnp.sum(lp[0:1] * lp[1:2], keepdims=True))
           - jnp.exp(jnp.sum(lp[2:3] * lp[3:4], keepdims=True)) + lambda_init)
    g = g_ref[...]

    if ctx_out:
        @pl.when(pl.program_id(2) == 0)
        def _():
            st = _AttnStream(qc_ref[...], e_ref.at[0], k_ref, v_ref)
            _attn_run([st], ctx_tiles, lam, g, lambda_init, [(oc_ref, slice(None))])

    streams, outs = [], []
    for i in range(ATT_STREAMS):
        rows = slice(i * ROW_TILE, (i + 1) * ROW_TILE)
        streams.append(_AttnStream(q_ref[0, 0, rows, :], e_ref.at[i], k_ref, v_ref))
        outs.append((o_ref, rows))
    _attn_run(streams, all_tiles, lam, g, lambda_init, outs)


def _diff_attention(q, k, v, lam_params, subln_g, lambda_init, n_ctx, ctx_out):
    _, n_batch, t, _ = q.shape
    seq = t - n_ctx
    tq = ATT_STREAMS * ROW_TILE
    assert seq % ATT_KEY_TILE == 0 and seq % tq == 0
    kv_spec = pl.BlockSpec((None, None, t, ATT_VD), lambda b, h, j: (h, b, 0, 0))
    ctx_spec = pl.BlockSpec((None, None, n_ctx, ATT_VD), lambda b, h, j: (h, b, 0, 0))
    in_specs = [_const_spec(lam_params.shape), _const_spec((1, ATT_VD)),
                pl.BlockSpec((pl.Element(1), pl.Element(1), pl.Element(tq), pl.Element(ATT_VD)),
                             lambda b, h, j: (h, b, pl.multiple_of(n_ctx + j * tq, ROW_TILE), 0))]
    out_specs = [pl.BlockSpec((None, None, tq, ATT_VD), lambda b, h, j: (h, b, j, 0))]
    out_shape = [jax.ShapeDtypeStruct((ATT_HEADS, n_batch, seq, ATT_VD), BF16)]
    args = [lam_params, subln_g.reshape(1, ATT_VD), q]
    if ctx_out:
        in_specs.append(ctx_spec)
        args.append(q)
        out_specs.append(ctx_spec)
        out_shape.append(jax.ShapeDtypeStruct((ATT_HEADS, n_batch, n_ctx, ATT_VD), BF16))
    kern = functools.partial(_attn_kernel, lambda_init=lambda_init, n_ctx=n_ctx, ctx_out=ctx_out)
    outs = pl.pallas_call(
        kern,
        grid=(n_batch, ATT_HEADS, seq // tq),
        in_specs=in_specs + [kv_spec, kv_spec],
        out_specs=out_specs,
        out_shape=out_shape,
        scratch_shapes=[pltpu.VMEM((ATT_STREAMS, 2, ROW_TILE, t), F32)],
        compiler_params=_cparams(("parallel", "parallel", "arbitrary")),
        name="diff_attention",
    )(*args, k, v)
    return outs[0], (outs[1] if ctx_out else None)


def _halo_specs(width, tiles_per_batch):
    per = ROW_TILE // HALO
    n_blocks = None

    def prev_idx(b, j):
        return (b, jnp.maximum(j * per - 1, 0), 0)

    def next_idx(b, j):
        return (b, jnp.minimum((j + 1) * per, tiles_per_batch * per - 1), 0)

    del n_blocks
    return (pl.BlockSpec((None, HALO, width), prev_idx), pl.BlockSpec((None, HALO, width), next_idx))


def _with_halo(prev_ref, cur, next_ref, width, tiles_per_batch, j):
    left_ok = jnp.where(j > 1, 1.0, 0.0)
    right_ok = jnp.where(jnp.logical_and(j > 0, j < tiles_per_batch - 1), 1.0, 0.0)
    return jnp.concatenate([prev_ref[:, 0:width] * left_ok, cur, next_ref[:, 0:width] * right_ok], axis=0)


def _dn_prep_kernel(dn_ref, prev_ref, next_ref, ab_ref, cw_ref, alog_ref, dtb_ref,
                    u_ref, w_ref, qg_ref, qk_ref, kt_ref, gl_ref, *, tiles_per_batch):
    j = pl.program_id(1)
    width = 3 * DN_W
    xe = _with_halo(prev_ref, dn_ref[:, 0:width], next_ref, width, tiles_per_batch, j)
    half = DN_CONV // 2
    acc = None
    for tap in range(DN_CONV):
        lo = HALO - half + tap
        term = xe[lo:lo + ROW_TILE, :] * cw_ref[tap:tap + 1, :]
        acc = term if acc is None else acc + term
    qkv = _silu(acc)

    lane_w = lax.broadcasted_iota(jnp.int32, (DN_W, DN_W), 1)
    row_w = lax.broadcasted_iota(jnp.int32, (DN_W, DN_W), 0)
    head_ones = jnp.where((lane_w >> 6) == (row_w >> 6), 1.0, 0.0).astype(BF16)

    def head_sums(x):
        hi = x.astype(BF16)
        lo = (x - hi.astype(F32)).astype(BF16)
        return (jnp.dot(hi, head_ones, preferred_element_type=F32)
                + jnp.dot(lo, head_ones, preferred_element_type=F32))

    def l2n(x):
        return x * lax.rsqrt(head_sums(x * x) + EPS)

    qn = l2n(qkv[:, 0:DN_W]) * (DN_DH ** -0.5)
    kn = l2n(qkv[:, DN_W:2 * DN_W])
    vv = qkv[:, 2 * DN_W:3 * DN_W]
    kn_t = kn.T

    ab = ab_ref[...]
    x = ab + dtb_ref[...]
    g = -jnp.exp(alog_ref[...]) * (jnp.maximum(x, 0.0) + jnp.log(1.0 + jnp.exp(-jnp.abs(x))))
    n_dh = 2 * DN_HEADS
    beta = _sigmoid(pltpu.roll(ab, LANES - n_dh, 1))
    r = lax.broadcasted_iota(jnp.int32, (ROW_TILE, ROW_TILE), 0)
    c = lax.broadcasted_iota(jnp.int32, (ROW_TILE, ROW_TILE), 1)
    same_chunk = (r >> 6) == (c >> 6)
    tri_f = jnp.where(jnp.logical_and(same_chunk, c <= r), 1.0, 0.0)
    tri_b = jnp.where(jnp.logical_and(same_chunk, c >= r), 1.0, 0.0)
    hp = lax.Precision.HIGHEST
    lane = lax.broadcasted_iota(jnp.int32, g.shape, 1)
    gcum = jnp.where(lane < DN_HEADS,
                     jnp.dot(tri_f, g, precision=hp, preferred_element_type=F32),
                     jnp.dot(tri_b, g, precision=hp, preferred_element_type=F32))
    gcum_t = gcum.T

    n_grp = 2 * DN_PAIRS
    sel_r = lax.broadcasted_iota(jnp.int32, (LANES, n_grp * LANES), 0)
    sel_c = lax.broadcasted_iota(jnp.int32, (LANES, n_grp * LANES), 1)
    grp = sel_c >> 7
    sel = jnp.where(sel_r == (grp >> 1) * DN_HEADS + (grp & 1) * 2 + ((sel_c >> 6) & 1), 1.0, 0.0).astype(BF16)

    def spread(x, pieces):
        out, rest = None, x
        for _ in range(pieces):
            part = rest.astype(BF16)
            rest = rest - part.astype(F32)
            term = jnp.dot(part, sel, preferred_element_type=F32)
            out = term if out is None else out + term
        return out

    g_lanes = spread(gcum, 3)
    b_lanes = spread(beta, 3)

    ri = lax.broadcasted_iota(jnp.int32, (DN_CHUNK, LANES), 0)
    ci = lax.broadcasted_iota(jnp.int32, (DN_CHUNK, LANES), 1) & (DN_DH - 1)
    left = lax.broadcasted_iota(jnp.int32, (DN_CHUNK, LANES), 1) < DN_DH
    eye2 = ri == ci
    same_sub = (ri >> 4) == (ci >> 4)
    n_chunks = ROW_TILE // DN_CHUNK

    def bdiag(x):
        return jnp.concatenate([jnp.where(left, x, 0.0), jnp.where(left, 0.0, x)], axis=-2)

    def pmm(a, b):
        return _bmm(a, bdiag(b))

    m_list, xu_list, xw_list = [], [], []
    for ch in range(n_chunks):
        rows = slice(ch * DN_CHUNK, (ch + 1) * DN_CHUNK)
        gram, qkm = [], []
        for p in range(DN_PAIRS):
            cols = slice(p * LANES, (p + 1) * LANES)
            kp = kn[rows, cols]
            kbd = bdiag(kp)
            gram.append(_dot_nt(kp, kbd))
            qkm.append(_dot_nt(qn[rows, cols], kbd))
        for grp_i in range(n_grp):
            d, p = grp_i // DN_PAIRS, grp_i % DN_PAIRS
            cols = slice(p * LANES, (p + 1) * LANES)
            gcols = slice(grp_i * LANES, (grp_i + 1) * LANES)
            incl = (ci <= ri) if d == 0 else (ci >= ri)
            strict = (ci < ri) if d == 0 else (ci > ri)
            last = DN_CHUNK - 1 if d == 0 else 0
            g_col = g_lanes[rows, gcols]
            b_col = b_lanes[rows, gcols]
            g_row = jnp.sum(jnp.where(eye2, g_col, 0.0), axis=0, keepdims=True)
            decay = jnp.where(incl, jnp.exp(jnp.minimum(g_col - g_row, 0.0)), 0.0)
            eg = jnp.exp(g_col)
            kp = kn[rows, cols]
            m_list.append(jnp.where(strict, gram[p] * decay, 0.0) * b_col)
            xu_list.append(vv[rows, cols] * b_col)
            xw_list.append(kp * (b_col * eg))
            qg_ref[grp_i, rows, :] = (qn[rows, cols] * eg).astype(qg_ref.dtype)
            qk_ref[grp_i, rows, :] = (qkm[p] * decay).astype(qk_ref.dtype)
            fac = []
            for hh in range(2):
                idx = d * DN_HEADS + 2 * p + hh
                gr = gcum_t[idx:idx + 1, rows]
                fac.append(jnp.broadcast_to(jnp.exp(gr[:, last:last + 1] - gr), (DN_DH, DN_CHUNK)))
            kt_ref[grp_i, ch] = (kn_t[cols, rows] * jnp.concatenate(fac, axis=0)).astype(kt_ref.dtype)
            gl_ref[grp_i, ch * HALO:(ch + 1) * HALO, :] = jnp.broadcast_to(eg[last:last + 1, :], (HALO, LANES))
    m = jnp.stack(m_list)
    n_diag = jnp.where(same_sub, m, 0.0)
    m_off = m - n_diag
    xp = -n_diag
    dinv = jnp.where(eye2, 1.0, 0.0) + xp
    for _ in range(3):
        xp = pmm(xp, xp)
        dinv = dinv + pmm(dinv, xp)
    f = pmm(dinv, m_off)
    f2 = pmm(f, f)

    def bdiag2(y):
        return jnp.concatenate([bdiag(y[..., 0:LANES]), bdiag(y[..., LANES:2 * LANES])], axis=-1)

    y = _bmm(dinv, jnp.concatenate([bdiag(jnp.stack(xu_list)), bdiag(jnp.stack(xw_list))], axis=-1))
    y = y - _bmm(f, bdiag2(y))
    sol = y + _bmm(f2, bdiag2(y))
    for ch in range(n_chunks):
        rows = slice(ch * DN_CHUNK, (ch + 1) * DN_CHUNK)
        for grp_i in range(n_grp):
            i = ch * n_grp + grp_i
            u_ref[grp_i, rows, :] = sol[i, :, 0:LANES]
            w_ref[grp_i, rows, :] = sol[i, :, LANES:2 * LANES].astype(w_ref.dtype)


def _dn_prep(dn, ab, conv_w, a_log, dt_bias, tiles_per_batch):
    n_batch, t, dn_cols = dn.shape
    n_dh = 2 * DN_HEADS
    n_grp = 2 * DN_PAIRS
    pad = jnp.zeros((1, LANES - n_dh), F32)
    alog_row = jnp.concatenate([a_log.reshape(1, n_dh).astype(F32), pad], axis=-1)
    dtb_row = jnp.concatenate([dt_bias.reshape(1, n_dh).astype(F32), pad], axis=-1)
    prev_spec, next_spec = _halo_specs(dn_cols, tiles_per_batch)
    tok = lambda b, j: (b, 0, j, 0)
    out_specs = [pl.BlockSpec((None, n_grp, ROW_TILE, LANES), tok)] * 4
    out_specs += [pl.BlockSpec((None, n_grp, ROW_TILE // DN_CHUNK, LANES, DN_CHUNK), lambda b, j: (b, 0, j, 0, 0)),
                  pl.BlockSpec((None, n_grp, HALO * ROW_TILE // DN_CHUNK, LANES), tok)]
    tok_shape = (n_batch, n_grp, t, LANES)
    out_shape = [jax.ShapeDtypeStruct(tok_shape, F32),
                 jax.ShapeDtypeStruct(tok_shape, BF16),
                 jax.ShapeDtypeStruct(tok_shape, BF16),
                 jax.ShapeDtypeStruct(tok_shape, BF16),
                 jax.ShapeDtypeStruct((n_batch, n_grp, t // DN_CHUNK, LANES, DN_CHUNK), BF16),
                 jax.ShapeDtypeStruct((n_batch, n_grp, HALO * t // DN_CHUNK, LANES), F32)]
    return pl.pallas_call(
        functools.partial(_dn_prep_kernel, tiles_per_batch=tiles_per_batch),
        grid=(n_batch, tiles_per_batch),
        in_specs=[pl.BlockSpec((None, ROW_TILE, dn_cols), lambda b, j: (b, j, 0)),
                  prev_spec, next_spec,
                  pl.BlockSpec((None, ROW_TILE, LANES), lambda b, j: (b, j, 0)),
                  _const_spec(conv_w.shape), _const_spec((1, LANES)), _const_spec((1, LANES))],
        out_specs=out_specs,
        out_shape=out_shape,
        compiler_params=_cparams(("parallel", "parallel")),
        name="dn_prep",
    )(dn, dn, dn, ab, conv_w, alog_row, dtb_row)


def _dn_scan_kernel(*refs):
    ins, (of_ref, ob_ref, s_ref) = refs[:12], refs[12:]

    @pl.when(pl.program_id(0) == 0)
    def _():
        s_ref[...] = jnp.zeros_like(s_ref)

    left = lax.broadcasted_iota(jnp.int32, (DN_DH, LANES), 1) < DN_DH

    def bdiag(x):
        return jnp.concatenate([jnp.where(left, x, 0.0), jnp.where(left, 0.0, x)], axis=-2)

    for d, o_ref in enumerate((of_ref, ob_ref)):
        u_ref, w_ref, qg_ref, qk_ref, kt_ref, gl_ref = ins[6 * d:6 * d + 6]
        flat = lambda r: r[...].reshape((-1,) + r.shape[2:])
        s = s_ref[d]
        s_bd = bdiag(s)
        v_new = flat(u_ref) - _bmm(flat(w_ref), s_bd)
        o = _bmm(flat(qg_ref), s_bd) + _bmm(flat(qk_ref), bdiag(v_new))
        o_ref[...] = o.reshape(o_ref.shape)
        full = _bmm(flat(kt_ref), v_new)
        s_ref[d] = s * flat(gl_ref)[:, 0:1, :] + jnp.where(left, full[:, 0:DN_DH, :], full[:, DN_DH:2 * DN_DH, :])


def _dn_scan(prep, n_ctx):
    u, w, qg, qk, kt, gl = prep
    n_batch, _, t, _ = u.shape
    n_steps = t // DN_CHUNK
    ctx_chunks = n_ctx // DN_CHUNK

    def chunk_of(d, i):
        if d == 0:
            return i
        return jnp.where(i < ctx_chunks, ctx_chunks - 1 - i, n_steps - 1 + ctx_chunks - i)

    in_specs = []
    for d in range(2):
        tok = functools.partial(lambda i, d: (0, d, chunk_of(d, i), 0), d=d)
        in_specs += [pl.BlockSpec((n_batch, DN_PAIRS, DN_CHUNK, LANES), tok)] * 4
        in_specs += [pl.BlockSpec((n_batch, DN_PAIRS, None, LANES, DN_CHUNK),
                                  functools.partial(lambda i, d: (0, d, chunk_of(d, i), 0, 0), d=d)),
                     pl.BlockSpec((n_batch, DN_PAIRS, HALO, LANES), tok)]
    out_specs = [pl.BlockSpec((n_batch, DN_PAIRS, DN_CHUNK, LANES),
                              functools.partial(lambda i, d: (0, 0, chunk_of(d, i), 0), d=d))
                 for d in range(2)]
    out_shape = [jax.ShapeDtypeStruct((n_batch, DN_PAIRS, t, LANES), F32)] * 2
    return pl.pallas_call(
        _dn_scan_kernel,
        grid=(n_steps,),
        in_specs=in_specs,
        out_specs=out_specs,
        out_shape=out_shape,
        scratch_shapes=[pltpu.VMEM((2, n_batch * DN_PAIRS, DN_DH, LANES), F32)],
        compiler_params=_cparams(("arbitrary",)),
        name="dn_scan",
    )(u, w, qg, qk, kt, gl, u, w, qg, qk, kt, gl)


def _merge_kernel(h_ref, mod_ref, att_ref, attc_ref, of_ref, ob_ref, z_ref, dng_ref,
                  pool_ref, pprev_ref, pnext_ref, pw_ref, ps_ref, gate_ref,
                  wba_ref, wbd_ref, wbp_ref, wo_ref, o_ref, *, tiles_per_batch, first_tile, seq, n_ctx):
    j = pl.program_id(1) + first_tile
    d = h_ref.shape[-1]
    o = jnp.concatenate([of_ref[p] + ob_ref[p] for p in range(DN_PAIRS)], axis=-1)
    lane_w = lax.broadcasted_iota(jnp.int32, (DN_W, DN_W), 1)
    row_w = lax.broadcasted_iota(jnp.int32, (DN_W, DN_W), 0)
    head_ones = jnp.where((lane_w >> 6) == (row_w >> 6), 1.0, 0.0).astype(BF16)
    sq = o * o
    sq_hi = sq.astype(BF16)
    sq_lo = (sq - sq_hi.astype(F32)).astype(BF16)
    ssq = (jnp.dot(sq_hi, head_ones, preferred_element_type=F32)
           + jnp.dot(sq_lo, head_ones, preferred_element_type=F32))
    dn = o * lax.rsqrt(ssq * (1.0 / DN_DH) + EPS) * dng_ref[...] * _silu(z_ref[...])
    cur = pool_ref[...]
    xe = _with_halo(pprev_ref, cur, pnext_ref, POOL_W, tiles_per_batch, j)
    length = jnp.where(j == 0, n_ctx, seq)
    t0 = jnp.where(j == 0, 0, (j - 1) * ROW_TILE)
    tpos = t0 + lax.broadcasted_iota(jnp.int32, (ROW_TILE, POOL_W), 0)
    lane = lax.broadcasted_iota(jnp.int32, (ROW_TILE, POOL_W), 1)
    n_ext = ROW_TILE + 2 * HALO
    sums = {1: xe}
    win = 1
    while win < max(POOL_WINDOWS):
        prev = sums[win]
        rows = prev.shape[0] - win
        sums[2 * win] = prev[0:rows, :] + prev[win:win + rows, :]
        win *= 2
    del n_ext
    mean = jnp.zeros((ROW_TILE, POOL_W), F32)
    for gi, win in enumerate(POOL_WINDOWS):
        start = HALO - win // 2
        wsum = sums[win][start:start + ROW_TILE, :]
        lo = jnp.clip(tpos - win // 2, 0, length)
        hi = jnp.clip(tpos - win // 2 + win, 0, length)
        mean = jnp.where((lane >> 6) == gi, wsum / (hi - lo).astype(F32), mean)
    pooled = _dot(mean - cur, pw_ref[...]) * ps_ref[...]
    ga = gate_ref[:, 0:d].astype(F32)
    gd = gate_ref[:, d:2 * d].astype(F32)
    gp = gate_ref[:, 2 * d:3 * d].astype(F32)
    att = jnp.concatenate([jnp.where(j == 0, attc_ref[hd], att_ref[hd])
                           for hd in range(ATT_HEADS)], axis=-1)
    mix = (ga * jnp.dot(att, wba_ref[...], preferred_element_type=F32)
           + gd * _dot(dn, wbd_ref[...]) + gp * _dot(pooled, wbp_ref[...]))
    y = _dot(mix, wo_ref[...])
    o_ref[...] = h_ref[...] + mod_ref[0:1, :] * y


def _merge(h, mod_gate, att, att_ctx, o_f, o_b, dn, dn_norm_g, pool, pool_w_bd, pool_scale, gates,
           wb_att, wb_dn, wb_pool, w_out, first_tile, n_ctx):
    n_batch, t, d = h.shape
    tpb = t // ROW_TILE
    n_rows = tpb - first_tile
    seq = t - n_ctx
    ft = first_tile
    prev_spec, next_spec = _halo_specs(POOL_W, tpb)
    shift = lambda f: (lambda b, j: f(b, j + ft))
    prev_spec = pl.BlockSpec(prev_spec.block_shape, shift(prev_spec.index_map))
    next_spec = pl.BlockSpec(next_spec.block_shape, shift(next_spec.index_map))
    row = lambda b, j: (b, j + ft, 0)

    def mod_idx(b, j):
        return (jnp.where(j + ft == 0, n_batch, b), 0, 0)

    weights = [wb_att, wb_dn, wb_pool, w_out]
    kern = functools.partial(_merge_kernel, tiles_per_batch=tpb, first_tile=ft, seq=seq, n_ctx=n_ctx)
    return pl.pallas_call(
        kern,
        grid=(n_batch, n_rows),
        in_specs=[pl.BlockSpec((None, ROW_TILE, d), row),
                  pl.BlockSpec((None, 1, d), mod_idx),
                  pl.BlockSpec((ATT_HEADS, None, ROW_TILE, ATT_VD),
                               lambda b, j: (0, b, jnp.maximum(j + ft - 1, 0), 0)),
                  pl.BlockSpec((ATT_HEADS, None, ROW_TILE, ATT_VD), lambda b, j: (0, b, 0, 0)),
                  pl.BlockSpec((None, DN_PAIRS, ROW_TILE, LANES), lambda b, j: (b, 0, j + ft, 0)),
                  pl.BlockSpec((None, DN_PAIRS, ROW_TILE, LANES), lambda b, j: (b, 0, j + ft, 0)),
                  pl.BlockSpec((None, ROW_TILE, DN_W), lambda b, j: (b, j + ft, 3)),
                  _const_spec((1, DN_W)),
                  pl.BlockSpec((None, ROW_TILE, POOL_W), row),
                  prev_spec, next_spec,
                  _const_spec(pool_w_bd.shape), _const_spec((1, POOL_W)),
                  pl.BlockSpec((None, ROW_TILE, 3 * d), row)]
                 + [_const_spec(w.shape) for w in weights],
        out_specs=pl.BlockSpec((None, ROW_TILE, d), lambda b, j: (b, j, 0)),
        out_shape=jax.ShapeDtypeStruct((n_batch, n_rows * ROW_TILE, d), F32),
        compiler_params=_cparams(("parallel", "parallel")),
        name="merge",
    )(h, mod_gate, att, att if att_ctx is None else att_ctx, o_f, o_b, dn,
      jnp.tile(dn_norm_g.reshape(1, DN_DH), (1, DN_HEADS)),
      pool, pool, pool,
      pool_w_bd, pool_scale.reshape(1, POOL_W), gates, *weights)


def _ffn_kernel(*refs, n_experts, n_fchunks, final_norm):
    if n_experts > 1:
        h_ref, mod_ref, g_ref, fg_ref, rw_ref, w1_ref, w3_ref, w2_ref, o_ref, u_ref, acc_ref, comb_ref = refs
    else:
        h_ref, mod_ref, g_ref, fg_ref, w1_ref, w3_ref, w2_ref, o_ref, u_ref, acc_ref = refs
    step = pl.program_id(2)
    e = step // n_fchunks

    @pl.when(step == 0)
    def _():
        u = _norm_mod(h_ref[...], g_ref[...], mod_ref[0:1, :], mod_ref[1:2, :])
        u_ref[...] = u.astype(BF16)
        acc_ref[...] = jnp.zeros_like(acc_ref)
        if n_experts > 1:
            logits = _dot(u, rw_ref[...])
            lane = lax.broadcasted_iota(jnp.int32, logits.shape, 1).astype(F32)
            neg = jnp.float32(-jnp.inf)
            lg = jnp.where(lane < n_experts, logits, neg)
            v1 = jnp.max(lg, axis=-1, keepdims=True)
            i1 = jnp.min(jnp.where(lg == v1, lane, float(LANES)), axis=-1, keepdims=True)
            lg2 = jnp.where(lane == i1, neg, lg)
            v2 = jnp.max(lg2, axis=-1, keepdims=True)
            i2 = jnp.min(jnp.where(lg2 == v2, lane, float(LANES)), axis=-1, keepdims=True)
            e2 = jnp.exp(v2 - v1)
            den = 1.0 + e2
            comb_ref[...] = jnp.where(lane == i1, 1.0 / den, 0.0) + jnp.where(lane == i2, e2 / den, 0.0)

    u = u_ref[...]
    hid = _silu(jnp.dot(u, w1_ref[...], preferred_element_type=F32)) * jnp.dot(
        u, w3_ref[...], preferred_element_type=F32)
    y = _dot(hid, w2_ref[...])
    if n_experts > 1:
        lane = lax.broadcasted_iota(jnp.int32, comb_ref.shape, 1)
        y = y * jnp.sum(jnp.where(lane == e, comb_ref[...], 0.0), axis=-1, keepdims=True)
    acc_ref[...] += y

    @pl.when(step == n_experts * n_fchunks - 1)
    def _():
        out = h_ref[...] + mod_ref[2:3, :] * acc_ref[...]
        if final_norm:
            out = out * lax.rsqrt(jnp.mean(out * out, axis=-1, keepdims=True) + EPS) * fg_ref[...]
        o_ref[...] = out


def _ffn(h, mod, norm_g, final_g, router_w, w1, w3, w2, has_ctx, final_norm):
    n_batch, t, d = h.shape
    n_exp, _, f = w1.shape
    n_rows = t // ROW_TILE
    row = lambda b, j, e: (b, j, 0)

    def mod_idx(b, j, e):
        return (jnp.where(jnp.logical_and(has_ctx, j == 0), n_batch, b), 0, 0)

    in_specs = [pl.BlockSpec((None, ROW_TILE, d), row),
                pl.BlockSpec((None, 3, d), mod_idx),
                _const_spec((1, d)), _const_spec((1, d))]
    args = [h, mod, norm_g.reshape(1, d), final_g.reshape(1, d)]
    scratch = [pltpu.VMEM((ROW_TILE, d), BF16), pltpu.VMEM((ROW_TILE, d), F32)]
    if n_exp > 1:
        in_specs.append(_const_spec(router_w.shape))
        args.append(router_w)
        scratch.append(pltpu.VMEM((ROW_TILE, LANES), F32))
    nfc = 1 if n_exp == 1 else 2
    fc = f // nfc
    assert fc * nfc == f and fc % LANES == 0
    in_specs += [pl.BlockSpec((None, d, fc), lambda b, j, s: (s // nfc, 0, s % nfc)),
                 pl.BlockSpec((None, d, fc), lambda b, j, s: (s // nfc, 0, s % nfc)),
                 pl.BlockSpec((None, fc, d), lambda b, j, s: (s // nfc, s % nfc, 0))]
    args += [w1, w3, w2]
    return pl.pallas_call(
        functools.partial(_ffn_kernel, n_experts=n_exp, n_fchunks=nfc, final_norm=final_norm),
        grid=(n_batch, n_rows, n_exp * nfc),
        in_specs=in_specs,
        out_specs=pl.BlockSpec((None, ROW_TILE, d), row),
        out_shape=jax.ShapeDtypeStruct((n_batch, t, d), F32),
        scratch_shapes=scratch,
        input_output_aliases={0: 0},
        compiler_params=_cparams(("parallel", "parallel", "arbitrary")),
        name="ffn_all_experts" if n_exp > 1 else "ffn_dense",
    )(*args)


MOE_BLOCK = 2048
MOE_SUB = 512
MOE_TILE = 288
MOE_ROW_ALIGN = 32
MOE_ROWS = -(-MOE_BLOCK // MOE_TILE) * MOE_TILE
MOE_PUT_TILE = 256
MOE_FCHUNKS = 4


def _top2_routing(logits, n_experts):
    lane = lax.broadcasted_iota(jnp.int32, logits.shape, 1).astype(F32)
    neg = jnp.float32(-jnp.inf)
    lg = jnp.where(lane < n_experts, logits, neg)
    v1 = jnp.max(lg, axis=-1, keepdims=True)
    i1 = jnp.min(jnp.where(lg == v1, lane, float(LANES)), axis=-1, keepdims=True)
    lg2 = jnp.where(lane == i1, neg, lg)
    v2 = jnp.max(lg2, axis=-1, keepdims=True)
    i2 = jnp.min(jnp.where(lg2 == v2, lane, float(LANES)), axis=-1, keepdims=True)
    e2 = jnp.exp(v2 - v1)
    den = 1.0 + e2
    first = lane == i1
    second = lane == i2
    sel = jnp.where(jnp.logical_or(first, second), 1.0, 0.0)
    comb = jnp.where(first, 1.0 / den, 0.0) + jnp.where(second, e2 / den, 0.0)
    return sel, comb


def _moe_kernel(h_ref, mod_ref, g_ref, fg_ref, rw_ref, w1_ref, w3_ref, w2_ref, o_ref,
                u_ref, x_ref, y_ref, rank_ref, rank_t_ref, comb_ref, cnt_ref, *, n_experts, final_norm):
    e = pl.program_id(2)
    f = pl.program_id(3)
    n_sub = MOE_BLOCK // MOE_SUB

    @pl.when(jnp.logical_and(e == 0, f == 0))
    def _route():
        r = lax.broadcasted_iota(jnp.int32, (ROW_TILE, ROW_TILE), 0)
        c = lax.broadcasted_iota(jnp.int32, (ROW_TILE, ROW_TILE), 1)
        before = jnp.where(c < r, 1.0, 0.0).astype(BF16)
        count = jnp.zeros((1, LANES), F32)
        for t in range(MOE_BLOCK // ROW_TILE):
            rows = slice(t * ROW_TILE, (t + 1) * ROW_TILE)
            u = _norm_mod(h_ref[rows, :], g_ref[...], mod_ref[0:1, :], mod_ref[1:2, :])
            u_ref[rows, :] = u.astype(BF16)
            sel, comb = _top2_routing(_dot(u, rw_ref[...]), n_experts)
            rank = jnp.dot(before, sel.astype(BF16), preferred_element_type=F32) + count
            rank_ref[rows, :] = jnp.where(sel > 0.0, rank, -1.0)
            comb_ref[rows, :] = comb
            count = count + jnp.sum(sel, axis=0, keepdims=True)
        cnt_ref[...] = jnp.broadcast_to(count, cnt_ref.shape)
        rank_t_ref[...] = rank_ref[...].T
        o_ref[...] = jnp.zeros_like(o_ref)

        @pl.when(jnp.logical_and(pl.program_id(0) == 0, pl.program_id(1) == 0))
        def _():
            y_ref[...] = jnp.zeros_like(y_ref)

    lane = lax.broadcasted_iota(jnp.int32, (1, LANES), 1)
    n_rows = jnp.sum(jnp.where(lane == e, cnt_ref[0:1, :], 0.0)).astype(jnp.int32)
    n_tiles = (n_rows + MOE_TILE - 1) // MOE_TILE
    rank_row = rank_t_ref[pl.ds(e, 1), :]

    def sub_hits(row0, s, tile=MOE_TILE):
        rk = rank_row[:, s * MOE_SUB:(s + 1) * MOE_SUB]
        lo = row0.astype(F32)
        inside = jnp.logical_and(rk >= lo, rk < lo + tile)
        return rk, jnp.max(jnp.where(inside, 1.0, 0.0)) > 0.0

    @pl.when(f == 0)
    def _compact():
        def body(i, carry):
            row0 = pl.multiple_of(i * MOE_TILE, MOE_ROW_ALIGN)
            ids = (row0 + lax.broadcasted_iota(jnp.int32, (MOE_TILE, 1), 0)).astype(F32)
            x_ref[pl.ds(row0, MOE_TILE), :] = jnp.zeros((MOE_TILE, x_ref.shape[1]), BF16)
            for s in range(n_sub):
                rk, hit = sub_hits(row0, s)

                @pl.when(hit)
                def _():
                    pick = jnp.where(rk == ids, 1.0, 0.0).astype(BF16)
                    got = jnp.dot(pick, u_ref[s * MOE_SUB:(s + 1) * MOE_SUB, :], preferred_element_type=F32)
                    x_ref[pl.ds(row0, MOE_TILE), :] += got.astype(BF16)
            return carry

        lax.fori_loop(0, n_tiles, body, 0)

    def expert_body(i, carry):
        row0 = pl.multiple_of(i * MOE_TILE, MOE_ROW_ALIGN)
        x = x_ref[pl.ds(row0, MOE_TILE), :]
        hid = _silu(jnp.dot(x, w1_ref[...], preferred_element_type=F32)) * jnp.dot(
            x, w3_ref[...], preferred_element_type=F32)
        y = _dot(hid, w2_ref[...])

        @pl.when(f == 0)
        def _():
            y_ref[pl.ds(row0, MOE_TILE), :] = y

        @pl.when(f > 0)
        def _():
            y_ref[pl.ds(row0, MOE_TILE), :] += y

        return carry

    lax.fori_loop(0, n_tiles, expert_body, 0)

    @pl.when(f == MOE_FCHUNKS - 1)
    def _expand():
        lanes = lax.broadcasted_iota(jnp.int32, (MOE_SUB, LANES), 1)

        def body(i, carry):
            row0 = pl.multiple_of(i * MOE_PUT_TILE, MOE_PUT_TILE)
            yt = y_ref[pl.ds(row0, MOE_PUT_TILE), :].astype(BF16)
            ids = (row0 + lax.broadcasted_iota(jnp.int32, (1, MOE_PUT_TILE), 1)).astype(F32)
            for s in range(n_sub):
                _, hit = sub_hits(row0, s, MOE_PUT_TILE)

                @pl.when(hit)
                def _():
                    rows = slice(s * MOE_SUB, (s + 1) * MOE_SUB)
                    rk_col = jnp.sum(jnp.where(lanes == e, rank_ref[rows, :], 0.0), axis=1, keepdims=True)
                    w_col = jnp.sum(jnp.where(lanes == e, comb_ref[rows, :], 0.0), axis=1, keepdims=True)
                    put = jnp.where(rk_col == ids, 1.0, 0.0).astype(BF16)
                    o_ref[rows, :] += w_col * jnp.dot(put, yt, preferred_element_type=F32)
            return carry

        lax.fori_loop(0, (n_rows + MOE_PUT_TILE - 1) // MOE_PUT_TILE, body, 0)

    @pl.when(jnp.logical_and(e == n_experts - 1, f == MOE_FCHUNKS - 1))
    def _finish():
        for t in range(MOE_BLOCK // ROW_TILE):
            rows = slice(t * ROW_TILE, (t + 1) * ROW_TILE)
            out = h_ref[rows, :] + mod_ref[2:3, :] * o_ref[rows, :]
            if final_norm:
                out = out * lax.rsqrt(jnp.mean(out * out, axis=-1, keepdims=True) + EPS) * fg_ref[...]
            o_ref[rows, :] = out


def _moe(h, mod, norm_g, final_g, router_w, w1, w3, w2, final_norm):
    n_batch, s, d = h.shape
    n_exp, _, f = w1.shape
    assert s % MOE_BLOCK == 0 and f % (MOE_FCHUNKS * LANES) == 0 and n_exp <= 8
    fc = f // MOE_FCHUNKS
    blk = lambda b, j, e, c: (b, j, 0)
    single = pl.Buffered(1)

    return pl.pallas_call(
        functools.partial(_moe_kernel, n_experts=n_exp, final_norm=final_norm),
        grid=(n_batch, s // MOE_BLOCK, n_exp, MOE_FCHUNKS),
        in_specs=[pl.BlockSpec((None, MOE_BLOCK, d), blk, pipeline_mode=single),
                  pl.BlockSpec((None, 3, d), lambda b, j, e, c: (b, 0, 0)),
                  _const_spec((1, d)), _const_spec((1, d)), _const_spec(router_w.shape),
                  pl.BlockSpec((None, d, fc), lambda b, j, e, c: (e, 0, c)),
                  pl.BlockSpec((None, d, fc), lambda b, j, e, c: (e, 0, c)),
                  pl.BlockSpec((None, fc, d), lambda b, j, e, c: (e, c, 0))],
        out_specs=pl.BlockSpec((None, MOE_BLOCK, d), blk, pipeline_mode=single),
        out_shape=jax.ShapeDtypeStruct((n_batch, s, d), F32),
        scratch_shapes=[pltpu.VMEM((MOE_BLOCK, d), BF16),
                        pltpu.VMEM((MOE_ROWS, d), BF16),
                        pltpu.VMEM((MOE_ROWS, d), F32),
                        pltpu.VMEM((MOE_BLOCK, LANES), F32),
                        pltpu.VMEM((LANES, MOE_BLOCK), F32),
                        pltpu.VMEM((MOE_BLOCK, LANES), F32),
                        pltpu.VMEM((HALO, LANES), F32)],
        compiler_params=_cparams(("arbitrary", "arbitrary", "arbitrary", "arbitrary")),
        name="moe_top2",
    )(h, mod, norm_g.reshape(1, d), final_g.reshape(1, d), router_w, w1, w3, w2)


def _block_diag(w):
    g, n, _ = w.shape
    out = jnp.zeros((g * n, g * n), w.dtype)
    for i in range(g):
        out = out.at[i * n:(i + 1) * n, i * n:(i + 1) * n].set(w[i])
    return out


def kernel(x, c, ctx, c_ctx, ada_w, ada_b, norm1_g, norm2_g, w_in, attn_lambda, attn_subln_g, dn_conv_w,
           dn_a_log, dn_dt_bias, dn_norm_g, pool_w, pool_scale, w_branch, w_out, ffn_w1, ffn_w3, ffn_w2,
           router_w, moe_w1, moe_w3, moe_w2, final_norm_g):
    n_batch, seq, d = x.shape
    n_ctx = ctx.shape[1]
    depth = ada_w.shape[0]
    assert n_ctx == ROW_TILE and seq % ROW_TILE == 0 and seq % GRID_W == 0 and d == 1024
    t = n_ctx + seq
    tpb = t // ROW_TILE

    h = None
    c_rows = jnp.concatenate([c, c_ctx[None, :], jnp.zeros((8 - n_batch - 1, d), F32)], axis=0)
    mods = _ada_mod(c_rows, ada_w, ada_b).reshape(depth, 8, 6, d)
    rope_tabs = _rope_tables(n_ctx, seq)

    sizes = (ATT_W, ATT_W, ATT_W, DN_W, DN_W, DN_W, DN_W, 2 * DN_HEADS, 2 * DN_HEADS, POOL_W, 3 * d)
    offs = np.concatenate([[0], np.cumsum(sizes)]).tolist()

    for l in range(depth):
        last = l == depth - 1
        first_tile = 1 if last else 0
        lambda_init = 0.8 - 0.6 * math.exp(-0.3 * l)
        wl = w_in[l].astype(BF16)
        seg = lambda a, b: wl[:, offs[a]:offs[b]]
        w_ab = jnp.concatenate([seg(7, 9), jnp.zeros((d, LANES - 4 * DN_HEADS), BF16)], axis=1)
        weights = [seg(0, 1), seg(1, 2), seg(2, 3), seg(3, 7), w_ab, seg(9, 10), seg(10, 11)]
        mod = mods[l, :n_batch + 1]
        src = (ctx, x) if l == 0 else h.reshape(n_batch * t, d)
        outs = _in_projection(src, mod[:, 0:2], norm1_g[l], rope_tabs, weights, n_batch, tpb)
        q, k, v, dn, ab, pool, gates = outs[:7]
        if l == 0:
            h = outs[7].reshape(n_batch, t, d)
        q, k, v = (a.reshape(ATT_HEADS, n_batch, t, ATT_VD) for a in (q, k, v))
        att, att_ctx = _diff_attention(q, k, v, attn_lambda[l], attn_subln_g[l], lambda_init, n_ctx, not last)
        dn = dn.reshape(n_batch, t, 4 * DN_W)
        prep = _dn_prep(dn, ab.reshape(n_batch, t, LANES), dn_conv_w[l], dn_a_log[l], dn_dt_bias[l], tpb)
        o_f, o_b = _dn_scan(prep, n_ctx)
        wb = w_branch[l].astype(BF16)
        h = _merge(h, mod[:, 2:3], att, att_ctx, o_f, o_b, dn, dn_norm_g[l], pool.reshape(n_batch, t, POOL_W),
                   _block_diag(pool_w[l]).astype(BF16), pool_scale[l], gates.reshape(n_batch, t, 3 * d),
                   wb[:ATT_W], wb[ATT_W:ATT_W + DN_W], wb[ATT_W + DN_W:], w_out[l].astype(BF16),
                   first_tile, n_ctx)
        i = l // 2
        if l % 2 == 0:
            h = _ffn(h, mod[:, 3:6], norm2_g[l], final_norm_g, None, ffn_w1[i][None].astype(BF16),
                     ffn_w3[i][None].astype(BF16), ffn_w2[i][None].astype(BF16), not last, last)
        else:
            rw = jnp.concatenate([router_w[i], jnp.zeros((d, LANES - N_EXPERTS), F32)], axis=1).astype(BF16)
            w1, w3, w2 = moe_w1[i].astype(BF16), moe_w3[i].astype(BF16), moe_w2[i].astype(BF16)
            if last:
                h = _moe(h, mod[:n_batch, 3:6], norm2_g[l], final_norm_g, rw, w1, w3, w2, True)
            else:
                h = _ffn(h, mod[:, 3:6], norm2_g[l], final_norm_g, rw, w1, w3, w2, True, False)
    return h
```

```python
import functools
import math

import numpy as np
import jax
import jax.numpy as jnp
from jax import lax
from jax.experimental import pallas as pl
from jax.experimental.pallas import tpu as pltpu

F32 = jnp.float32
BF16 = jnp.bfloat16

EPS = 1e-6
GRID_W = 64
ROPE_BASE = 10000.0
ATT_HEADS = 4
ATT_DH = 64
ATT_VD = 2 * ATT_DH
ATT_W = ATT_HEADS * ATT_VD
DN_HEADS = 4
DN_DH = 64
DN_W = DN_HEADS * DN_DH
DN_PAIRS = DN_HEADS // 2
DN_CONV = 5
DN_CHUNK = 64
DN_SUB = 16
POOL_WINDOWS = (2, 4, 8, 16)
POOL_GD = 64
POOL_W = len(POOL_WINDOWS) * POOL_GD
N_EXPERTS = 8
TOP_K = 2

ROW_TILE = 256
HALO = 8
LANES = 128
VMEM_LIMIT = 56 * 1024 * 1024


def _cparams(sem):
    return pltpu.CompilerParams(dimension_semantics=sem, vmem_limit_bytes=VMEM_LIMIT)


def _dot(a, b):
    return jnp.dot(a.astype(BF16), b.astype(BF16), preferred_element_type=F32)


def _dot_nt(a, b):
    return lax.dot_general(a.astype(BF16), b.astype(BF16), (((1,), (1,)), ((), ())),
                           preferred_element_type=F32)


def _bmm(a, b):
    return lax.dot_general(a.astype(BF16), b.astype(BF16), (((2,), (1,)), ((0,), (0,))),
                           preferred_element_type=F32)


def _bmm_nt(a, b):
    return lax.dot_general(a.astype(BF16), b.astype(BF16), (((2,), (2,)), ((0,), (0,))),
                           preferred_element_type=F32)


def _sigmoid(x):
    return 1.0 / (1.0 + jnp.exp(-x))


def _silu(x):
    return x * _sigmoid(x)


def _const_spec(shape):
    nd = len(shape)
    return pl.BlockSpec(shape, lambda *_: (0,) * nd)


def _ada_kernel(c_ref, w_ref, b_ref, o_ref):
    o_ref[...] = _dot(_silu(c_ref[...]), w_ref[...]) + b_ref[...]


def _ada_mod(c_rows, ada_w, ada_b):
    depth, d, six_d = ada_w.shape
    n = six_d // d
    rows = c_rows.shape[0]
    return pl.pallas_call(
        _ada_kernel,
        grid=(depth, n),
        in_specs=[pl.BlockSpec((rows, d), lambda l, j: (0, 0)),
                  pl.BlockSpec((None, d, d), lambda l, j: (l, 0, j)),
                  pl.BlockSpec((None, 1, d), lambda l, j: (l, 0, j))],
        out_specs=pl.BlockSpec((None, rows, d), lambda l, j: (l, 0, j)),
        out_shape=jax.ShapeDtypeStruct((depth, rows, six_d), F32),
        compiler_params=_cparams(("parallel", "parallel")),
        name="ada_mod",
    )(c_rows, ada_w, ada_b.reshape(depth, 1, six_d))


def _norm_mod(x, g, shift, scale):
    y = x * lax.rsqrt(jnp.mean(x * x, axis=-1, keepdims=True) + EPS) * g
    return y * (1.0 + scale) + shift


def _inproj_kernel(*refs, tiles_per_batch, split_in):
    if split_in:
        ctx_ref, x_ref = refs[:2]
        (mod_ref, g_ref, cos_ref, su_ref, sd_ref, wq_ref, wk_ref, wv_ref, wdn_ref, wab_ref, wpool_ref, wg_ref,
         q_ref, k_ref, v_ref, dn_ref, ab_ref, pool_ref, gate_ref, h_out_ref) = refs[2:]
        h = jnp.where(pl.program_id(0) % tiles_per_batch == 0, ctx_ref[...], x_ref[...])
        h_out_ref[...] = h
    else:
        (h_ref, mod_ref, g_ref, cos_ref, su_ref, sd_ref, wq_ref, wk_ref, wv_ref, wdn_ref, wab_ref, wpool_ref,
         wg_ref, q_ref, k_ref, v_ref, dn_ref, ab_ref, pool_ref, gate_ref) = refs
        h = h_ref[...]
    u = _norm_mod(h, g_ref[...], mod_ref[0:1, :], mod_ref[1:2, :]).astype(BF16)
    reps = ATT_W // LANES
    cos = jnp.tile(cos_ref[...], (1, reps))
    s_up = jnp.tile(su_ref[...], (1, reps))
    s_dn = jnp.tile(sd_ref[...], (1, reps))
    quarter = ATT_DH // 4

    def rope(t):
        return (t * cos + pltpu.roll(t, ATT_W - quarter, 1) * s_up + pltpu.roll(t, quarter, 1) * s_dn)

    q = rope(jnp.dot(u, wq_ref[...], preferred_element_type=F32))
    k = rope(jnp.dot(u, wk_ref[...], preferred_element_type=F32))
    v = jnp.dot(u, wv_ref[...], preferred_element_type=F32)
    for hd in range(ATT_HEADS):
        cols = slice(hd * ATT_VD, (hd + 1) * ATT_VD)
        q_ref[hd] = (q[:, cols] * (ATT_DH ** -0.5 * LOG2E)).astype(q_ref.dtype)
        k_ref[hd] = k[:, cols].astype(k_ref.dtype)
        v_ref[hd] = v[:, cols].astype(v_ref.dtype)
    dn_ref[...] = jnp.dot(u, wdn_ref[...], preferred_element_type=F32)
    ab_ref[...] = jnp.dot(u, wab_ref[...], preferred_element_type=F32)
    pool_ref[...] = jnp.dot(u, wpool_ref[...], preferred_element_type=F32)
    gate_ref[...] = _sigmoid(jnp.dot(u, wg_ref[...], preferred_element_type=F32)).astype(gate_ref.dtype)


def _in_projection(h, mod, norm_g, rope_tabs, weights, n_batch, tiles_per_batch):
    split_in = isinstance(h, tuple)
    tpb = tiles_per_batch
    d = mod.shape[-1]
    n_tiles = n_batch * tpb
    rows = n_tiles * ROW_TILE
    widths = [w.shape[1] for w in weights]
    out_dtypes = [BF16, BF16, BF16, F32, F32, F32, BF16]

    def mod_idx(i):
        return (jnp.where(i % tpb == 0, n_batch, i // tpb), 0, 0)

    if split_in:
        h_args = list(h)
        in_specs = [pl.BlockSpec((None, ROW_TILE, d), lambda i: (i // tpb, 0, 0)),
                    pl.BlockSpec((None, ROW_TILE, d), lambda i: (i // tpb, jnp.maximum(i % tpb - 1, 0), 0))]
        widths.append(d)
        out_dtypes.append(F32)
    else:
        h_args = [h]
        in_specs = [pl.BlockSpec((ROW_TILE, d), lambda i: (i, 0))]
    in_specs += [pl.BlockSpec((None, 2, d), mod_idx), _const_spec((1, d))]
    in_specs += [pl.BlockSpec((ROW_TILE, LANES), lambda i: (i % tpb, 0)) for _ in range(3)]
    in_specs += [pl.BlockSpec(w.shape, lambda i: (0, 0), pipeline_mode=pl.Buffered(1)) for w in weights]
    out_specs = [pl.BlockSpec((ROW_TILE, w), lambda i: (i, 0)) for w in widths]
    out_shape = [jax.ShapeDtypeStruct((rows, w), dt) for w, dt in zip(widths, out_dtypes)]
    for o in range(3):
        out_specs[o] = pl.BlockSpec((ATT_HEADS, ROW_TILE, ATT_VD), lambda i: (0, i, 0))
        out_shape[o] = jax.ShapeDtypeStruct((ATT_HEADS, rows, ATT_VD), BF16)
    return pl.pallas_call(
        functools.partial(_inproj_kernel, tiles_per_batch=tpb, split_in=split_in),
        grid=(n_tiles,),
        in_specs=in_specs,
        out_specs=out_specs,
        out_shape=out_shape,
        compiler_params=_cparams(("parallel",)),
        name="in_projection",
    )(*h_args, mod, norm_g.reshape(1, d), *rope_tabs, *weights)


def _rope_tables(n_ctx, seq):
    rows = seq // GRID_W
    row = jnp.repeat(jnp.arange(rows), GRID_W).astype(F32)
    col = jnp.tile(jnp.arange(GRID_W), rows).astype(F32)
    n_freq = ATT_DH // 4
    inv = ROPE_BASE ** (-jnp.arange(n_freq, dtype=F32) / n_freq)
    ar = row[:, None] * inv
    ac = col[:, None] * inv
    cr, sr, cc, sc = jnp.cos(ar), jnp.sin(ar), jnp.cos(ac), jnp.sin(ac)
    z = jnp.zeros_like(sr)
    cos = jnp.concatenate([cr, cr, cc, cc], axis=-1)
    s_up = jnp.concatenate([-sr, z, -sc, z], axis=-1)
    s_dn = jnp.concatenate([z, sr, z, sc], axis=-1)

    def full(t, ctx_val):
        t = jnp.tile(t, (1, LANES // ATT_DH))
        return jnp.concatenate([jnp.full((n_ctx, LANES), ctx_val, F32), t], axis=0)

    return full(cos, 1.0), full(s_up, 0.0), full(s_dn, 0.0)


ATT_KEY_TILE = 512
LOG2E = math.log2(math.e)


ATT_STREAMS = 2


class _AttnStream:
    def __init__(self, q, e_ref, k_ref, v_ref):
        lane = lax.broadcasted_iota(jnp.int32, q.shape, 1)
        zero = jnp.zeros_like(q)
        self.qm = (jnp.where(lane < ATT_DH, q, zero), jnp.where(lane >= ATT_DH, q, zero))
        self.e_ref, self.k_ref, self.v_ref = e_ref, k_ref, v_ref
        tq = q.shape[0]
        self.m_part = [jnp.full((tq, LANES), -jnp.inf, F32)] * 2
        self.l_part = [jnp.zeros((tq, LANES), F32)] * 2
        self.acc = jnp.zeros((tq, ATT_VD), F32)

    @staticmethod
    def _lane_fold(x, op):
        out = x[:, 0:LANES]
        for c in range(1, x.shape[1] // LANES):
            out = op(out, x[:, c * LANES:(c + 1) * LANES])
        return out

    def scores(self, k0, w):
        kt = self.k_ref[k0:k0 + w, :]
        for m in range(2):
            s = _dot_nt(self.qm[m], kt)
            self.e_ref[m, :, k0:k0 + w] = s
            self.m_part[m] = jnp.maximum(self.m_part[m], self._lane_fold(s, jnp.maximum))

    def end_scores(self):
        self.mx = [jnp.max(p, axis=-1, keepdims=True) for p in self.m_part]

    def exps(self, k0, w):
        for m in range(2):
            e = jnp.exp2(self.e_ref[m, :, k0:k0 + w] - self.mx[m])
            self.e_ref[m, :, k0:k0 + w] = e
            self.l_part[m] = self.l_part[m] + self._lane_fold(e, jnp.add)

    def end_exps(self, lam):
        self.inv0 = 1.0 / jnp.sum(self.l_part[0], axis=-1, keepdims=True)
        self.inv1 = lam / jnp.sum(self.l_part[1], axis=-1, keepdims=True)

    def values(self, k0, w):
        a = self.e_ref[0, :, k0:k0 + w] * self.inv0 - self.e_ref[1, :, k0:k0 + w] * self.inv1
        self.acc = self.acc + jnp.dot(a.astype(BF16), self.v_ref[k0:k0 + w, :], preferred_element_type=F32)

    def result(self, g, lambda_init):
        y = self.acc * lax.rsqrt(jnp.mean(self.acc * self.acc, axis=-1, keepdims=True) + EPS) * g
        return y * (1.0 - lambda_init)


def _attn_run(streams, tiles, lam, g, lambda_init, outs):
    passes = ("scores", "exps", "values")
    for step in range(len(passes) + len(streams) - 1):
        active = [(st, passes[step - i]) for i, st in enumerate(streams) if 0 <= step - i < len(passes)]
        for k0, w in tiles:
            for st, name in active:
                getattr(st, name)(k0, w)
        for st, name in active:
            if name == "scores":
                st.end_scores()
            elif name == "exps":
                st.end_exps(lam)
    for st, (o_ref, rows) in zip(streams, outs):
        o_ref[rows, :] = st.result(g, lambda_init).astype(o_ref.dtype)


def _attn_kernel(*refs, lambda_init, n_ctx, ctx_out):
    if ctx_out:
        lam_ref, g_ref, q_ref, qc_ref, k_ref, v_ref, o_ref, oc_ref, e_ref = refs
    else:
        lam_ref, g_ref, q_ref, k_ref, v_ref, o_ref, e_ref = refs
    n_keys = k_ref.shape[0]
    ctx_tiles = ((0, n_ctx),)
    all_tiles = ctx_tiles + tuple((k0, ATT_KEY_TILE) for k0 in range(n_ctx, n_keys, ATT_KEY_TILE))
    lp = lam_ref[...]
    lam = (jnp.exp(jnp.sum(lp[0:1] * lp[1:2], keepdims=True))
           - jnp.exp(jnp.sum(lp[2:3] * lp[3:4], keepdims=True)) + lambda_init)
    g = g_ref[...]

    if ctx_out:
        @pl.when(pl.program_id(2) == 0)
        def _():
            st = _AttnStream(qc_ref[...], e_ref.at[0], k_ref, v_ref)
            _attn_run([st], ctx_tiles, lam, g, lambda_init, [(oc_ref, slice(None))])

    streams, outs = [], []
    for i in range(ATT_STREAMS):
        rows = slice(i * ROW_TILE, (i + 1) * ROW_TILE)
        streams.append(_AttnStream(q_ref[0, 0, rows, :], e_ref.at[i], k_ref, v_ref))
        outs.append((o_ref, rows))
    _attn_run(streams, all_tiles, lam, g, lambda_init, outs)


def _diff_attention(q, k, v, lam_params, subln_g, lambda_init, n_ctx, ctx_out):
    _, n_batch, t, _ = q.shape
    seq = t - n_ctx
    tq = ATT_STREAMS * ROW_TILE
    assert seq % ATT_KEY_TILE == 0 and seq % tq == 0
    kv_spec = pl.BlockSpec((None, None, t, ATT_VD), lambda b, h, j: (h, b, 0, 0))
    ctx_spec = pl.BlockSpec((None, None, n_ctx, ATT_VD), lambda b, h, j: (h, b, 0, 0))
    in_specs = [_const_spec(lam_params.shape), _const_spec((1, ATT_VD)),
                pl.BlockSpec((pl.Element(1), pl.Element(1), pl.Element(tq), pl.Element(ATT_VD)),
                             lambda b, h, j: (h, b, pl.multiple_of(n_ctx + j * tq, ROW_TILE), 0))]
    out_specs = [pl.BlockSpec((None, None, tq, ATT_VD), lambda b, h, j: (h, b, j, 0))]
    out_shape = [jax.ShapeDtypeStruct((ATT_HEADS, n_batch, seq, ATT_VD), BF16)]
    args = [lam_params, subln_g.reshape(1, ATT_VD), q]
    if ctx_out:
        in_specs.append(ctx_spec)
        args.append(q)
        out_specs.append(ctx_spec)
        out_shape.append(jax.ShapeDtypeStruct((ATT_HEADS, n_batch, n_ctx, ATT_VD), BF16))
    kern = functools.partial(_attn_kernel, lambda_init=lambda_init, n_ctx=n_ctx, ctx_out=ctx_out)
    outs = pl.pallas_call(
        kern,
        grid=(n_batch, ATT_HEADS, seq // tq),
        in_specs=in_specs + [kv_spec, kv_spec],
        out_specs=out_specs,
        out_shape=out_shape,
        scratch_shapes=[pltpu.VMEM((ATT_STREAMS, 2, ROW_TILE, t), F32)],
        compiler_params=_cparams(("parallel", "parallel", "arbitrary")),
        name="diff_attention",
    )(*args, k, v)
    return outs[0], (outs[1] if ctx_out else None)


def _halo_specs(width, tiles_per_batch):
    per = ROW_TILE // HALO
    n_blocks = None

    def prev_idx(b, j):
        return (b, jnp.maximum(j * per - 1, 0), 0)

    def next_idx(b, j):
        return (b, jnp.minimum((j + 1) * per, tiles_per_batch * per - 1), 0)

    del n_blocks
    return (pl.BlockSpec((None, HALO, width), prev_idx), pl.BlockSpec((None, HALO, width), next_idx))


def _with_halo(prev_ref, cur, next_ref, width, tiles_per_batch, j):
    left_ok = jnp.where(j > 1, 1.0, 0.0)
    right_ok = jnp.where(jnp.logical_and(j > 0, j < tiles_per_batch - 1), 1.0, 0.0)
    return jnp.concatenate([prev_ref[:, 0:width] * left_ok, cur, next_ref[:, 0:width] * right_ok], axis=0)


def _dn_prep_kernel(dn_ref, prev_ref, next_ref, ab_ref, cw_ref, alog_ref, dtb_ref,
                    u_ref, w_ref, qg_ref, qk_ref, kt_ref, gl_ref, *, tiles_per_batch):
    j = pl.program_id(1)
    width = 3 * DN_W
    xe = _with_halo(prev_ref, dn_ref[:, 0:width], next_ref, width, tiles_per_batch, j)
    half = DN_CONV // 2
    acc = None
    for tap in range(DN_CONV):
        lo = HALO - half + tap
        term = xe[lo:lo + ROW_TILE, :] * cw_ref[tap:tap + 1, :]
        acc = term if acc is None else acc + term
    qkv = _silu(acc)

    lane_w = lax.broadcasted_iota(jnp.int32, (DN_W, DN_W), 1)
    row_w = lax.broadcasted_iota(jnp.int32, (DN_W, DN_W), 0)
    head_ones = jnp.where((lane_w >> 6) == (row_w >> 6), 1.0, 0.0).astype(BF16)

    def head_sums(x):
        hi = x.astype(BF16)
        lo = (x - hi.astype(F32)).astype(BF16)
        return (jnp.dot(hi, head_ones, preferred_element_type=F32)
                + jnp.dot(lo, head_ones, preferred_element_type=F32))

    def l2n(x):
        return x * lax.rsqrt(head_sums(x * x) + EPS)

    qn = l2n(qkv[:, 0:DN_W]) * (DN_DH ** -0.5)
    kn = l2n(qkv[:, DN_W:2 * DN_W])
    vv = qkv[:, 2 * DN_W:3 * DN_W]
    kn_t = kn.T

    ab = ab_ref[...]
    x = ab + dtb_ref[...]
    g = -jnp.exp(alog_ref[...]) * (jnp.maximum(x, 0.0) + jnp.log(1.0 + jnp.exp(-jnp.abs(x))))
    n_dh = 2 * DN_HEADS
    beta = _sigmoid(pltpu.roll(ab, LANES - n_dh, 1))
    r = lax.broadcasted_iota(jnp.int32, (ROW_TILE, ROW_TILE), 0)
    c = lax.broadcasted_iota(jnp.int32, (ROW_TILE, ROW_TILE), 1)
    same_chunk = (r >> 6) == (c >> 6)
    tri_f = jnp.where(jnp.logical_and(same_chunk, c <= r), 1.0, 0.0)
    tri_b = jnp.where(jnp.logical_and(same_chunk, c >= r), 1.0, 0.0)
    hp = lax.Precision.HIGHEST
    lane = lax.broadcasted_iota(jnp.int32, g.shape, 1)
    gcum = jnp.where(lane < DN_HEADS,
                     jnp.dot(tri_f, g, precision=hp, preferred_element_type=F32),
                     jnp.dot(tri_b, g, precision=hp, preferred_element_type=F32))
    gcum_t = gcum.T

    n_grp = 2 * DN_PAIRS
    sel_r = lax.broadcasted_iota(jnp.int32, (LANES, n_grp * LANES), 0)
    sel_c = lax.broadcasted_iota(jnp.int32, (LANES, n_grp * LANES), 1)
    grp = sel_c >> 7
    sel = jnp.where(sel_r == (grp >> 1) * DN_HEADS + (grp & 1) * 2 + ((sel_c >> 6) & 1), 1.0, 0.0).astype(BF16)

    def spread(x, pieces):
        out, rest = None, x
        for _ in range(pieces):
            part = rest.astype(BF16)
            rest = rest - part.astype(F32)
            term = jnp.dot(part, sel, preferred_element_type=F32)
            out = term if out is None else out + term
        return out

    g_lanes = spread(gcum, 3)
    b_lanes = spread(beta, 3)

    ri = lax.broadcasted_iota(jnp.int32, (DN_CHUNK, LANES), 0)
    ci = lax.broadcasted_iota(jnp.int32, (DN_CHUNK, LANES), 1) & (DN_DH - 1)
    left = lax.broadcasted_iota(jnp.int32, (DN_CHUNK, LANES), 1) < DN_DH
    eye2 = ri == ci
    same_sub = (ri >> 4) == (ci >> 4)
    n_chunks = ROW_TILE // DN_CHUNK

    def bdiag(x):
        return jnp.concatenate([jnp.where(left, x, 0.0), jnp.where(left, 0.0, x)], axis=-2)

    def pmm(a, b):
        return _bmm(a, bdiag(b))

    m_list, xu_list, xw_list = [], [], []
    for ch in range(n_chunks):
        rows = slice(ch * DN_CHUNK, (ch + 1) * DN_CHUNK)
        gram, qkm = [], []
        for p in range(DN_PAIRS):
            cols = slice(p * LANES, (p + 1) * LANES)
            kp = kn[rows, cols]
            kbd = bdiag(kp)
            gram.append(_dot_nt(kp, kbd))
            qkm.append(_dot_nt(qn[rows, cols], kbd))
        for grp_i in range(n_grp):
            d, p = grp_i // DN_PAIRS, grp_i % DN_PAIRS
            cols = slice(p * LANES, (p + 1) * LANES)
            gcols = slice(grp_i * LANES, (grp_i + 1) * LANES)
            incl = (ci <= ri) if d == 0 else (ci >= ri)
            strict = (ci < ri) if d == 0 else (ci > ri)
            last = DN_CHUNK - 1 if d == 0 else 0
            g_col = g_lanes[rows, gcols]
            b_col = b_lanes[rows, gcols]
            g_row = jnp.sum(jnp.where(eye2, g_col, 0.0), axis=0, keepdims=True)
            decay = jnp.where(incl, jnp.exp(jnp.minimum(g_col - g_row, 0.0)), 0.0)
            eg = jnp.exp(g_col)
            kp = kn[rows, cols]
            m_list.append(jnp.where(strict, gram[p] * decay, 0.0) * b_col)
            xu_list.append(vv[rows, cols] * b_col)
            xw_list.append(kp * (b_col * eg))
            qg_ref[grp_i, rows, :] = (qn[rows, cols] * eg).astype(qg_ref.dtype)
            qk_ref[grp_i, rows, :] = (qkm[p] * decay).astype(qk_ref.dtype)
            fac = []
            for hh in range(2):
                idx = d * DN_HEADS + 2 * p + hh
                gr = gcum_t[idx:idx + 1, rows]
                fac.append(jnp.broadcast_to(jnp.exp(gr[:, last:last + 1] - gr), (DN_DH, DN_CHUNK)))
            kt_ref[grp_i, ch] = (kn_t[cols, rows] * jnp.concatenate(fac, axis=0)).astype(kt_ref.dtype)
            gl_ref[grp_i, ch * HALO:(ch + 1) * HALO, :] = jnp.broadcast_to(eg[last:last + 1, :], (HALO, LANES))
    m = jnp.stack(m_list)
    n_diag = jnp.where(same_sub, m, 0.0)
    m_off = m - n_diag
    xp = -n_diag
    dinv = jnp.where(eye2, 1.0, 0.0) + xp
    for _ in range(3):
        xp = pmm(xp, xp)
        dinv = dinv + pmm(dinv, xp)
    f = pmm(dinv, m_off)
    f2 = pmm(f, f)

    def bdiag2(y):
        return jnp.concatenate([bdiag(y[..., 0:LANES]), bdiag(y[..., LANES:2 * LANES])], axis=-1)

    y = _bmm(dinv, jnp.concatenate([bdiag(jnp.stack(xu_list)), bdiag(jnp.stack(xw_list))], axis=-1))
    y = y - _bmm(f, bdiag2(y))
    sol = y + _bmm(f2, bdiag2(y))
    for ch in range(n_chunks):
        rows = slice(ch * DN_CHUNK, (ch + 1) * DN_CHUNK)
        for grp_i in range(n_grp):
            i = ch * n_grp + grp_i
            u_ref[grp_i, rows, :] = sol[i, :, 0:LANES]
            w_ref[grp_i, rows, :] = sol[i, :, LANES:2 * LANES].astype(w_ref.dtype)


def _dn_prep(dn, ab, conv_w, a_log, dt_bias, tiles_per_batch):
    n_batch, t, dn_cols = dn.shape
    n_dh = 2 * DN_HEADS
    n_grp = 2 * DN_PAIRS
    pad = jnp.zeros((1, LANES - n_dh), F32)
    alog_row = jnp.concatenate([a_log.reshape(1, n_dh).astype(F32), pad], axis=-1)
    dtb_row = jnp.concatenate([dt_bias.reshape(1, n_dh).astype(F32), pad], axis=-1)
    prev_spec, next_spec = _halo_specs(dn_cols, tiles_per_batch)
    tok = lambda b, j: (b, 0, j, 0)
    out_specs = [pl.BlockSpec((None, n_grp, ROW_TILE, LANES), tok)] * 4
    out_specs += [pl.BlockSpec((None, n_grp, ROW_TILE // DN_CHUNK, LANES, DN_CHUNK), lambda b, j: (b, 0, j, 0, 0)),
                  pl.BlockSpec((None, n_grp, HALO * ROW_TILE // DN_CHUNK, LANES), tok)]
    tok_shape = (n_batch, n_grp, t, LANES)
    out_shape = [jax.ShapeDtypeStruct(tok_shape, F32),
                 jax.ShapeDtypeStruct(tok_shape, BF16),
                 jax.ShapeDtypeStruct(tok_shape, BF16),
                 jax.ShapeDtypeStruct(tok_shape, BF16),
                 jax.ShapeDtypeStruct((n_batch, n_grp, t // DN_CHUNK, LANES, DN_CHUNK), BF16),
                 jax.ShapeDtypeStruct((n_batch, n_grp, HALO * t // DN_CHUNK, LANES), F32)]
    return pl.pallas_call(
        functools.partial(_dn_prep_kernel, tiles_per_batch=tiles_per_batch),
        grid=(n_batch, tiles_per_batch),
        in_specs=[pl.BlockSpec((None, ROW_TILE, dn_cols), lambda b, j: (b, j, 0)),
                  prev_spec, next_spec,
                  pl.BlockSpec((None, ROW_TILE, LANES), lambda b, j: (b, j, 0)),
                  _const_spec(conv_w.shape), _const_spec((1, LANES)), _const_spec((1, LANES))],
        out_specs=out_specs,
        out_shape=out_shape,
        compiler_params=_cparams(("parallel", "parallel")),
        name="dn_prep",
    )(dn, dn, dn, ab, conv_w, alog_row, dtb_row)


def _dn_scan_kernel(*refs):
    ins, (of_ref, ob_ref, s_ref) = refs[:12], refs[12:]

    @pl.when(pl.program_id(0) == 0)
    def _():
        s_ref[...] = jnp.zeros_like(s_ref)

    left = lax.broadcasted_iota(jnp.int32, (DN_DH, LANES), 1) < DN_DH

    def bdiag(x):
        return jnp.concatenate([jnp.where(left, x, 0.0), jnp.where(left, 0.0, x)], axis=-2)

    for d, o_ref in enumerate((of_ref, ob_ref)):
        u_ref, w_ref, qg_ref, qk_ref, kt_ref, gl_ref = ins[6 * d:6 * d + 6]
        flat = lambda r: r[...].reshape((-1,) + r.shape[2:])
        s = s_ref[d]
        s_bd = bdiag(s)
        v_new = flat(u_ref) - _bmm(flat(w_ref), s_bd)
        o = _bmm(flat(qg_ref), s_bd) + _bmm(flat(qk_ref), bdiag(v_new))
        o_ref[...] = o.reshape(o_ref.shape)
        full = _bmm(flat(kt_ref), v_new)
        s_ref[d] = s * flat(gl_ref)[:, 0:1, :] + jnp.where(left, full[:, 0:DN_DH, :], full[:, DN_DH:2 * DN_DH, :])


def _dn_scan(prep, n_ctx):
    u, w, qg, qk, kt, gl = prep
    n_batch, _, t, _ = u.shape
    n_steps = t // DN_CHUNK
    ctx_chunks = n_ctx // DN_CHUNK

    def chunk_of(d, i):
        if d == 0:
            return i
        return jnp.where(i < ctx_chunks, ctx_chunks - 1 - i, n_steps - 1 + ctx_chunks - i)

    in_specs = []
    for d in range(2):
        tok = functools.partial(lambda i, d: (0, d, chunk_of(d, i), 0), d=d)
        in_specs += [pl.BlockSpec((n_batch, DN_PAIRS, DN_CHUNK, LANES), tok)] * 4
        in_specs += [pl.BlockSpec((n_batch, DN_PAIRS, None, LANES, DN_CHUNK),
                                  functools.partial(lambda i, d: (0, d, chunk_of(d, i), 0, 0), d=d)),
                     pl.BlockSpec((n_batch, DN_PAIRS, HALO, LANES), tok)]
    out_specs = [pl.BlockSpec((n_batch, DN_PAIRS, DN_CHUNK, LANES),
                              functools.partial(lambda i, d: (0, 0, chunk_of(d, i), 0), d=d))
                 for d in range(2)]
    out_shape = [jax.ShapeDtypeStruct((n_batch, DN_PAIRS, t, LANES), F32)] * 2
    return pl.pallas_call(
        _dn_scan_kernel,
        grid=(n_steps,),
        in_specs=in_specs,
        out_specs=out_specs,
        out_shape=out_shape,
        scratch_shapes=[pltpu.VMEM((2, n_batch * DN_PAIRS, DN_DH, LANES), F32)],
        compiler_params=_cparams(("arbitrary",)),
        name="dn_scan",
    )(u, w, qg, qk, kt, gl, u, w, qg, qk, kt, gl)


def _merge_kernel(h_ref, mod_ref, att_ref, attc_ref, of_ref, ob_ref, z_ref, dng_ref,
                  pool_ref, pprev_ref, pnext_ref, pw_ref, ps_ref, gate_ref,
                  wba_ref, wbd_ref, wbp_ref, wo_ref, o_ref, *, tiles_per_batch, first_tile, seq, n_ctx):
    j = pl.program_id(1) + first_tile
    d = h_ref.shape[-1]
    o = jnp.concatenate([of_ref[p] + ob_ref[p] for p in range(DN_PAIRS)], axis=-1)
    lane_w = lax.broadcasted_iota(jnp.int32, (DN_W, DN_W), 1)
    row_w = lax.broadcasted_iota(jnp.int32, (DN_W, DN_W), 0)
    head_ones = jnp.where((lane_w >> 6) == (row_w >> 6), 1.0, 0.0).astype(BF16)
    sq = o * o
    sq_hi = sq.astype(BF16)
    sq_lo = (sq - sq_hi.astype(F32)).astype(BF16)
    ssq = (jnp.dot(sq_hi, head_ones, preferred_element_type=F32)
           + jnp.dot(sq_lo, head_ones, preferred_element_type=F32))
    dn = o * lax.rsqrt(ssq * (1.0 / DN_DH) + EPS) * dng_ref[...] * _silu(z_ref[...])
    cur = pool_ref[...]
    xe = _with_halo(pprev_ref, cur, pnext_ref, POOL_W, tiles_per_batch, j)
    length = jnp.where(j == 0, n_ctx, seq)
    t0 = jnp.where(j == 0, 0, (j - 1) * ROW_TILE)
    tpos = t0 + lax.broadcasted_iota(jnp.int32, (ROW_TILE, POOL_W), 0)
    lane = lax.broadcasted_iota(jnp.int32, (ROW_TILE, POOL_W), 1)
    n_ext = ROW_TILE + 2 * HALO
    sums = {1: xe}
    win = 1
    while win < max(POOL_WINDOWS):
        prev = sums[win]
        rows = prev.shape[0] - win
        sums[2 * win] = prev[0:rows, :] + prev[win:win + rows, :]
        win *= 2
    del n_ext
    mean = jnp.zeros((ROW_TILE, POOL_W), F32)
    for gi, win in enumerate(POOL_WINDOWS):
        start = HALO - win // 2
        wsum = sums[win][start:start + ROW_TILE, :]
        lo = jnp.clip(tpos - win // 2, 0, length)
        hi = jnp.clip(tpos - win // 2 + win, 0, length)
        mean = jnp.where((lane >> 6) == gi, wsum / (hi - lo).astype(F32), mean)
    pooled = _dot(mean - cur, pw_ref[...]) * ps_ref[...]
    ga = gate_ref[:, 0:d].astype(F32)
    gd = gate_ref[:, d:2 * d].astype(F32)
    gp = gate_ref[:, 2 * d:3 * d].astype(F32)
    att = jnp.concatenate([jnp.where(j == 0, attc_ref[hd], att_ref[hd])
                           for hd in range(ATT_HEADS)], axis=-1)
    mix = (ga * jnp.dot(att, wba_ref[...], preferred_element_type=F32)
           + gd * _dot(dn, wbd_ref[...]) + gp * _dot(pooled, wbp_ref[...]))
    y = _dot(mix, wo_ref[...])
    o_ref[...] = h_ref[...] + mod_ref[0:1, :] * y


def _merge(h, mod_gate, att, att_ctx, o_f, o_b, dn, dn_norm_g, pool, pool_w_bd, pool_scale, gates,
           wb_att, wb_dn, wb_pool, w_out, first_tile, n_ctx):
    n_batch, t, d = h.shape
    tpb = t // ROW_TILE
    n_rows = tpb - first_tile
    seq = t - n_ctx
    ft = first_tile
    prev_spec, next_spec = _halo_specs(POOL_W, tpb)
    shift = lambda f: (lambda b, j: f(b, j + ft))
    prev_spec = pl.BlockSpec(prev_spec.block_shape, shift(prev_spec.index_map))
    next_spec = pl.BlockSpec(next_spec.block_shape, shift(next_spec.index_map))
    row = lambda b, j: (b, j + ft, 0)

    def mod_idx(b, j):
        return (jnp.where(j + ft == 0, n_batch, b), 0, 0)

    weights = [wb_att, wb_dn, wb_pool, w_out]
    kern = functools.partial(_merge_kernel, tiles_per_batch=tpb, first_tile=ft, seq=seq, n_ctx=n_ctx)
    return pl.pallas_call(
        kern,
        grid=(n_batch, n_rows),
        in_specs=[pl.BlockSpec((None, ROW_TILE, d), row),
                  pl.BlockSpec((None, 1, d), mod_idx),
                  pl.BlockSpec((ATT_HEADS, None, ROW_TILE, ATT_VD),
                               lambda b, j: (0, b, jnp.maximum(j + ft - 1, 0), 0)),
                  pl.BlockSpec((ATT_HEADS, None, ROW_TILE, ATT_VD), lambda b, j: (0, b, 0, 0)),
                  pl.BlockSpec((None, DN_PAIRS, ROW_TILE, LANES), lambda b, j: (b, 0, j + ft, 0)),
                  pl.BlockSpec((None, DN_PAIRS, ROW_TILE, LANES), lambda b, j: (b, 0, j + ft, 0)),
                  pl.BlockSpec((None, ROW_TILE, DN_W), lambda b, j: (b, j + ft, 3)),
                  _const_spec((1, DN_W)),
                  pl.BlockSpec((None, ROW_TILE, POOL_W), row),
                  prev_spec, next_spec,
                  _const_spec(pool_w_bd.shape), _const_spec((1, POOL_W)),
                  pl.BlockSpec((None, ROW_TILE, 3 * d), row)]
                 + [_const_spec(w.shape) for w in weights],
        out_specs=pl.BlockSpec((None, ROW_TILE, d), lambda b, j: (b, j, 0)),
        out_shape=jax.ShapeDtypeStruct((n_batch, n_rows * ROW_TILE, d), F32),
        compiler_params=_cparams(("parallel", "parallel")),
        name="merge",
    )(h, mod_gate, att, att if att_ctx is None else att_ctx, o_f, o_b, dn,
      jnp.tile(dn_norm_g.reshape(1, DN_DH), (1, DN_HEADS)),
      pool, pool, pool,
      pool_w_bd, pool_scale.reshape(1, POOL_W), gates, *weights)


def _ffn_kernel(*refs, n_experts, n_fchunks, final_norm):
    if n_experts > 1:
        h_ref, mod_ref, g_ref, fg_ref, rw_ref, w1_ref, w3_ref, w2_ref, o_ref, u_ref, acc_ref, comb_ref = refs
    else:
        h_ref, mod_ref, g_ref, fg_ref, w1_ref, w3_ref, w2_ref, o_ref, u_ref, acc_ref = refs
    step = pl.program_id(2)
    e = step // n_fchunks

    @pl.when(step == 0)
    def _():
        u = _norm_mod(h_ref[...], g_ref[...], mod_ref[0:1, :], mod_ref[1:2, :])
        u_ref[...] = u.astype(BF16)
        acc_ref[...] = jnp.zeros_like(acc_ref)
        if n_experts > 1:
            logits = _dot(u, rw_ref[...])
            lane = lax.broadcasted_iota(jnp.int32, logits.shape, 1).astype(F32)
            neg = jnp.float32(-jnp.inf)
            lg = jnp.where(lane < n_experts, logits, neg)
            v1 = jnp.max(lg, axis=-1, keepdims=True)
            i1 = jnp.min(jnp.where(lg == v1, lane, float(LANES)), axis=-1, keepdims=True)
            lg2 = jnp.where(lane == i1, neg, lg)
            v2 = jnp.max(lg2, axis=-1, keepdims=True)
            i2 = jnp.min(jnp.where(lg2 == v2, lane, float(LANES)), axis=-1, keepdims=True)
            e2 = jnp.exp(v2 - v1)
            den = 1.0 + e2
            comb_ref[...] = jnp.where(lane == i1, 1.0 / den, 0.0) + jnp.where(lane == i2, e2 / den, 0.0)

    u = u_ref[...]
    hid = _silu(jnp.dot(u, w1_ref[...], preferred_element_type=F32)) * jnp.dot(
        u, w3_ref[...], preferred_element_type=F32)
    y = _dot(hid, w2_ref[...])
    if n_experts > 1:
        lane = lax.broadcasted_iota(jnp.int32, comb_ref.shape, 1)
        y = y * jnp.sum(jnp.where(lane == e, comb_ref[...], 0.0), axis=-1, keepdims=True)
    acc_ref[...] += y

    @pl.when(step == n_experts * n_fchunks - 1)
    def _():
        out = h_ref[...] + mod_ref[2:3, :] * acc_ref[...]
        if final_norm:
            out = out * lax.rsqrt(jnp.mean(out * out, axis=-1, keepdims=True) + EPS) * fg_ref[...]
        o_ref[...] = out


def _ffn(h, mod, norm_g, final_g, router_w, w1, w3, w2, has_ctx, final_norm):
    n_batch, t, d = h.shape
    n_exp, _, f = w1.shape
    n_rows = t // ROW_TILE
    row = lambda b, j, e: (b, j, 0)

    def mod_idx(b, j, e):
        return (jnp.where(jnp.logical_and(has_ctx, j == 0), n_batch, b), 0, 0)

    in_specs = [pl.BlockSpec((None, ROW_TILE, d), row),
                pl.BlockSpec((None, 3, d), mod_idx),
                _const_spec((1, d)), _const_spec((1, d))]
    args = [h, mod, norm_g.reshape(1, d), final_g.reshape(1, d)]
    scratch = [pltpu.VMEM((ROW_TILE, d), BF16), pltpu.VMEM((ROW_TILE, d), F32)]
    if n_exp > 1:
        in_specs.append(_const_spec(router_w.shape))
        args.append(router_w)
        scratch.append(pltpu.VMEM((ROW_TILE, LANES), F32))
    nfc = 1 if n_exp == 1 else 2
    fc = f // nfc
    assert fc * nfc == f and fc % LANES == 0
    in_specs += [pl.BlockSpec((None, d, fc), lambda b, j, s: (s // nfc, 0, s % nfc)),
                 pl.BlockSpec((None, d, fc), lambda b, j, s: (s // nfc, 0, s % nfc)),
                 pl.BlockSpec((None, fc, d), lambda b, j, s: (s // nfc, s % nfc, 0))]
    args += [w1, w3, w2]
    return pl.pallas_call(
        functools.partial(_ffn_kernel, n_experts=n_exp, n_fchunks=nfc, final_norm=final_norm),
        grid=(n_batch, n_rows, n_exp * nfc),
        in_specs=in_specs,
        out_specs=pl.BlockSpec((None, ROW_TILE, d), row),
        out_shape=jax.ShapeDtypeStruct((n_batch, t, d), F32),
        scratch_shapes=scratch,
        input_output_aliases={0: 0},
        compiler_params=_cparams(("parallel", "parallel", "arbitrary")),
        name="ffn_all_experts" if n_exp > 1 else "ffn_dense",
    )(*args)


MOE_BLOCK = 2048
MOE_SUB = 512
MOE_TILE = 288
MOE_ROW_ALIGN = 32
MOE_ROWS = -(-MOE_BLOCK // MOE_TILE) * MOE_TILE
MOE_PUT_TILE = 256
MOE_FCHUNKS = 4


def _top2_routing(logits, n_experts):
    lane = lax.broadcasted_iota(jnp.int32, logits.shape, 1).astype(F32)
    neg = jnp.float32(-jnp.inf)
    lg = jnp.where(lane < n_experts, logits, neg)
    v1 = jnp.max(lg, axis=-1, keepdims=True)
    i1 = jnp.min(jnp.where(lg == v1, lane, float(LANES)), axis=-1, keepdims=True)
    lg2 = jnp.where(lane == i1, neg, lg)
    v2 = jnp.max(lg2, axis=-1, keepdims=True)
    i2 = jnp.min(jnp.where(lg2 == v2, lane, float(LANES)), axis=-1, keepdims=True)
    e2 = jnp.exp(v2 - v1)
    den = 1.0 + e2
    first = lane == i1
    second = lane == i2
    sel = jnp.where(jnp.logical_or(first, second), 1.0, 0.0)
    comb = jnp.where(first, 1.0 / den, 0.0) + jnp.where(second, e2 / den, 0.0)
    return sel, comb


def _moe_kernel(h_ref, mod_ref, g_ref, fg_ref, rw_ref, w1_ref, w3_ref, w2_ref, o_ref,
                u_ref, x_ref, y_ref, rank_ref, rank_t_ref, comb_ref, cnt_ref, span_ref, *, n_experts, final_norm):
    e = pl.program_id(2)
    f = pl.program_id(3)
    n_sub = MOE_BLOCK // MOE_SUB

    @pl.when(jnp.logical_and(e == 0, f == 0))
    def _route():
        r = lax.broadcasted_iota(jnp.int32, (ROW_TILE, ROW_TILE), 0)
        c = lax.broadcasted_iota(jnp.int32, (ROW_TILE, ROW_TILE), 1)
        before = jnp.where(c < r, 1.0, 0.0).astype(BF16)
        count = jnp.zeros((1, LANES), F32)
        for t in range(MOE_BLOCK // ROW_TILE):
            rows = slice(t * ROW_TILE, (t + 1) * ROW_TILE)
            u = _norm_mod(h_ref[rows, :], g_ref[...], mod_ref[0:1, :], mod_ref[1:2, :])
            u_ref[rows, :] = u.astype(BF16)
            sel, comb = _top2_routing(_dot(u, rw_ref[...]), n_experts)
            rank = jnp.dot(before, sel.astype(BF16), preferred_element_type=F32) + count
            rank_ref[rows, :] = jnp.where(sel > 0.0, rank, -1.0)
            comb_ref[rows, :] = comb
            count = count + jnp.sum(sel, axis=0, keepdims=True)
        lane = lax.broadcasted_iota(jnp.int32, (1, LANES), 1)
        for ee in range(n_experts):
            cnt_ref[ee] = jnp.sum(jnp.where(lane == ee, count, 0.0)).astype(jnp.int32)
        rank_t_ref[...] = rank_ref[...].T
        o_ref[...] = jnp.zeros_like(o_ref)
        for s in range(n_sub):
            rk = rank_ref[s * MOE_SUB:(s + 1) * MOE_SUB, :]
            lo_v = jnp.min(jnp.where(rk >= 0.0, rk, float(MOE_BLOCK)), axis=0, keepdims=True)
            hi_v = jnp.max(rk, axis=0, keepdims=True)
            for ee in range(n_experts):
                span_ref[2 * (s * n_experts + ee)] = jnp.sum(jnp.where(lane == ee, lo_v, 0.0)).astype(jnp.int32)
                span_ref[2 * (s * n_experts + ee) + 1] = jnp.sum(jnp.where(lane == ee, hi_v, 0.0)).astype(jnp.int32)

        @pl.when(jnp.logical_and(pl.program_id(0) == 0, pl.program_id(1) == 0))
        def _():
            y_ref[...] = jnp.zeros_like(y_ref)

    n_rows = cnt_ref[e]
    n_tiles = (n_rows + MOE_TILE - 1) // MOE_TILE
    rank_row = rank_t_ref[pl.ds(e, 1), :]

    def sub_hits(row0, s, tile=MOE_TILE):
        rk = rank_row[:, s * MOE_SUB:(s + 1) * MOE_SUB]
        lo = span_ref[2 * (s * n_experts + e)]
        hi = span_ref[2 * (s * n_experts + e) + 1]
        return rk, jnp.logical_and(hi >= row0, lo < row0 + tile)

    @pl.when(f == 0)
    def _compact():
        def body(i, carry):
            row0 = pl.multiple_of(i * MOE_TILE, MOE_ROW_ALIGN)
            ids = (row0 + lax.broadcasted_iota(jnp.int32, (MOE_TILE, 1), 0)).astype(F32)
            x_ref[pl.ds(row0, MOE_TILE), :] = jnp.zeros((MOE_TILE, x_ref.shape[1]), BF16)
            for s in range(n_sub):
                rk, hit = sub_hits(row0, s)

                @pl.when(hit)
                def _():
                    pick = jnp.where(rk == ids, 1.0, 0.0).astype(BF16)
                    got = jnp.dot(pick, u_ref[s * MOE_SUB:(s + 1) * MOE_SUB, :], preferred_element_type=F32)
                    x_ref[pl.ds(row0, MOE_TILE), :] += got.astype(BF16)
            return carry

        lax.fori_loop(0, n_tiles, body, 0)

    def expert_body(i, carry):
        row0 = pl.multiple_of(i * MOE_TILE, MOE_ROW_ALIGN)
        x = x_ref[pl.ds(row0, MOE_TILE), :]
        hid = _silu(jnp.dot(x, w1_ref[...], preferred_element_type=F32)) * jnp.dot(
            x, w3_ref[...], preferred_element_type=F32)
        y = _dot(hid, w2_ref[...])

        @pl.when(f == 0)
        def _():
            y_ref[pl.ds(row0, MOE_TILE), :] = y

        @pl.when(f > 0)
        def _():
            y_ref[pl.ds(row0, MOE_TILE), :] += y

        return carry

    lax.fori_loop(0, n_tiles, expert_body, 0)

    @pl.when(f == MOE_FCHUNKS - 1)
    def _expand():
        lanes = lax.broadcasted_iota(jnp.int32, (MOE_SUB, LANES), 1)

        def body(i, carry):
            row0 = pl.multiple_of(i * MOE_PUT_TILE, MOE_PUT_TILE)
            yt = y_ref[pl.ds(row0, MOE_PUT_TILE), :].astype(BF16)
            ids = (row0 + lax.broadcasted_iota(jnp.int32, (1, MOE_PUT_TILE), 1)).astype(F32)
            for s in range(n_sub):
                _, hit = sub_hits(row0, s, MOE_PUT_TILE)

                @pl.when(hit)
                def _():
                    rows = slice(s * MOE_SUB, (s + 1) * MOE_SUB)
                    rk_col = jnp.sum(jnp.where(lanes == e, rank_ref[rows, :], 0.0), axis=1, keepdims=True)
                    w_col = jnp.sum(jnp.where(lanes == e, comb_ref[rows, :], 0.0), axis=1, keepdims=True)
                    put = jnp.where(rk_col == ids, 1.0, 0.0).astype(BF16)
                    o_ref[rows, :] += w_col * jnp.dot(put, yt, preferred_element_type=F32)
            return carry

        lax.fori_loop(0, (n_rows + MOE_PUT_TILE - 1) // MOE_PUT_TILE, body, 0)

    @pl.when(jnp.logical_and(e == n_experts - 1, f == MOE_FCHUNKS - 1))
    def _finish():
        for t in range(MOE_BLOCK // ROW_TILE):
            rows = slice(t * ROW_TILE, (t + 1) * ROW_TILE)
            out = h_ref[rows, :] + mod_ref[2:3, :] * o_ref[rows, :]
            if final_norm:
                out = out * lax.rsqrt(jnp.mean(out * out, axis=-1, keepdims=True) + EPS) * fg_ref[...]
            o_ref[rows, :] = out


def _moe(h, mod, norm_g, final_g, router_w, w1, w3, w2, final_norm):
    n_batch, s, d = h.shape
    n_exp, _, f = w1.shape
    assert s % MOE_BLOCK == 0 and f % (MOE_FCHUNKS * LANES) == 0 and n_exp <= 8
    fc = f // MOE_FCHUNKS
    blk = lambda b, j, e, c: (b, j, 0)
    single = pl.Buffered(1)

    return pl.pallas_call(
        functools.partial(_moe_kernel, n_experts=n_exp, final_norm=final_norm),
        grid=(n_batch, s // MOE_BLOCK, n_exp, MOE_FCHUNKS),
        in_specs=[pl.BlockSpec((None, MOE_BLOCK, d), blk, pipeline_mode=single),
                  pl.BlockSpec((None, 3, d), lambda b, j, e, c: (b, 0, 0)),
                  _const_spec((1, d)), _const_spec((1, d)), _const_spec(router_w.shape),
                  pl.BlockSpec((None, d, fc), lambda b, j, e, c: (e, 0, c)),
                  pl.BlockSpec((None, d, fc), lambda b, j, e, c: (e, 0, c)),
                  pl.BlockSpec((None, fc, d), lambda b, j, e, c: (e, c, 0))],
        out_specs=pl.BlockSpec((None, MOE_BLOCK, d), blk, pipeline_mode=single),
        out_shape=jax.ShapeDtypeStruct((n_batch, s, d), F32),
        scratch_shapes=[pltpu.VMEM((MOE_BLOCK, d), BF16),
                        pltpu.VMEM((MOE_ROWS, d), BF16),
                        pltpu.VMEM((MOE_ROWS, d), F32),
                        pltpu.VMEM((MOE_BLOCK, LANES), F32),
                        pltpu.VMEM((LANES, MOE_BLOCK), F32),
                        pltpu.VMEM((MOE_BLOCK, LANES), F32),
                        pltpu.SMEM((HALO,), jnp.int32),
                        pltpu.SMEM((2 * (MOE_BLOCK // MOE_SUB) * HALO,), jnp.int32)],
        compiler_params=_cparams(("arbitrary", "arbitrary", "arbitrary", "arbitrary")),
        name="moe_top2",
    )(h, mod, norm_g.reshape(1, d), final_g.reshape(1, d), router_w, w1, w3, w2)


def _block_diag(w):
    g, n, _ = w.shape
    out = jnp.zeros((g * n, g * n), w.dtype)
    for i in range(g):
        out = out.at[i * n:(i + 1) * n, i * n:(i + 1) * n].set(w[i])
    return out


def kernel(x, c, ctx, c_ctx, ada_w, ada_b, norm1_g, norm2_g, w_in, attn_lambda, attn_subln_g, dn_conv_w,
           dn_a_log, dn_dt_bias, dn_norm_g, pool_w, pool_scale, w_branch, w_out, ffn_w1, ffn_w3, ffn_w2,
           router_w, moe_w1, moe_w3, moe_w2, final_norm_g):
    n_batch, seq, d = x.shape
    n_ctx = ctx.shape[1]
    depth = ada_w.shape[0]
    assert n_ctx == ROW_TILE and seq % ROW_TILE == 0 and seq % GRID_W == 0 and d == 1024
    t = n_ctx + seq
    tpb = t // ROW_TILE

    h = None
    c_rows = jnp.concatenate([c, c_ctx[None, :], jnp.zeros((8 - n_batch - 1, d), F32)], axis=0)
    mods = _ada_mod(c_rows, ada_w, ada_b).reshape(depth, 8, 6, d)
    rope_tabs = _rope_tables(n_ctx, seq)

    sizes = (ATT_W, ATT_W, ATT_W, DN_W, DN_W, DN_W, DN_W, 2 * DN_HEADS, 2 * DN_HEADS, POOL_W, 3 * d)
    offs = np.concatenate([[0], np.cumsum(sizes)]).tolist()

    for l in range(depth):
        last = l == depth - 1
        first_tile = 1 if last else 0
        lambda_init = 0.8 - 0.6 * math.exp(-0.3 * l)
        wl = w_in[l].astype(BF16)
        seg = lambda a, b: wl[:, offs[a]:offs[b]]
        w_ab = jnp.concatenate([seg(7, 9), jnp.zeros((d, LANES - 4 * DN_HEADS), BF16)], axis=1)
        weights = [seg(0, 1), seg(1, 2), seg(2, 3), seg(3, 7), w_ab, seg(9, 10), seg(10, 11)]
        mod = mods[l, :n_batch + 1]
        src = (ctx, x) if l == 0 else h.reshape(n_batch * t, d)
        outs = _in_projection(src, mod[:, 0:2], norm1_g[l], rope_tabs, weights, n_batch, tpb)
        q, k, v, dn, ab, pool, gates = outs[:7]
        if l == 0:
            h = outs[7].reshape(n_batch, t, d)
        q, k, v = (a.reshape(ATT_HEADS, n_batch, t, ATT_VD) for a in (q, k, v))
        att, att_ctx = _diff_attention(q, k, v, attn_lambda[l], attn_subln_g[l], lambda_init, n_ctx, not last)
        dn = dn.reshape(n_batch, t, 4 * DN_W)
        prep = _dn_prep(dn, ab.reshape(n_batch, t, LANES), dn_conv_w[l], dn_a_log[l], dn_dt_bias[l], tpb)
        o_f, o_b = _dn_scan(prep, n_ctx)
        wb = w_branch[l].astype(BF16)
        h = _merge(h, mod[:, 2:3], att, att_ctx, o_f, o_b, dn, dn_norm_g[l], pool.reshape(n_batch, t, POOL_W),
                   _block_diag(pool_w[l]).astype(BF16), pool_scale[l], gates.reshape(n_batch, t, 3 * d),
                   wb[:ATT_W], wb[ATT_W:ATT_W + DN_W], wb[ATT_W + DN_W:], w_out[l].astype(BF16),
                   first_tile, n_ctx)
        i = l // 2
        if l % 2 == 0:
            h = _ffn(h, mod[:, 3:6], norm2_g[l], final_norm_g, None, ffn_w1[i][None].astype(BF16),
                     ffn_w3[i][None].astype(BF16), ffn_w2[i][None].astype(BF16), not last, last)
        else:
            rw = jnp.concatenate([router_w[i], jnp.zeros((d, LANES - N_EXPERTS), F32)], axis=1).astype(BF16)
            w1, w3, w2 = moe_w1[i].astype(BF16), moe_w3[i].astype(BF16), moe_w2[i].astype(BF16)
            if last:
                h = _moe(h, mod[:n_batch, 3:6], norm2_g[l], final_norm_g, rw, w1, w3, w2, True)
            else:
                h = _ffn(h, mod[:, 3:6], norm2_g[l], final_norm_g, rw, w1, w3, w2, True, False)
    return h
```

```python
import functools
import math

import numpy as np
import jax
import jax.numpy as jnp
from jax import lax
from jax.experimental import pallas as pl
from jax.experimental.pallas import tpu as pltpu

F32 = jnp.float32
BF16 = jnp.bfloat16

EPS = 1e-6
GRID_W = 64
ROPE_BASE = 10000.0
ATT_HEADS = 4
ATT_DH = 64
ATT_VD = 2 * ATT_DH
ATT_W = ATT_HEADS * ATT_VD
DN_HEADS = 4
DN_DH = 64
DN_W = DN_HEADS * DN_DH
DN_PAIRS = DN_HEADS // 2
DN_CONV = 5
DN_CHUNK = 64
DN_SUB = 16
POOL_WINDOWS = (2, 4, 8, 16)
POOL_GD = 64
POOL_W = len(POOL_WINDOWS) * POOL_GD
N_EXPERTS = 8
TOP_K = 2

ROW_TILE = 256
HALO = 8
LANES = 128
VMEM_LIMIT = 56 * 1024 * 1024


def _cparams(sem):
    return pltpu.CompilerParams(dimension_semantics=sem, vmem_limit_bytes=VMEM_LIMIT)


def _dot(a, b):
    return jnp.dot(a.astype(BF16), b.astype(BF16), preferred_element_type=F32)


def _dot_nt(a, b):
    return lax.dot_general(a.astype(BF16), b.astype(BF16), (((1,), (1,)), ((), ())),
                           preferred_element_type=F32)


def _bmm(a, b):
    return lax.dot_general(a.astype(BF16), b.astype(BF16), (((2,), (1,)), ((0,), (0,))),
                           preferred_element_type=F32)


def _bmm_nt(a, b):
    return lax.dot_general(a.astype(BF16), b.astype(BF16), (((2,), (2,)), ((0,), (0,))),
                           preferred_element_type=F32)


def _sigmoid(x):
    return 1.0 / (1.0 + jnp.exp(-x))


def _silu(x):
    return x * _sigmoid(x)


def _const_spec(shape):
    nd = len(shape)
    return pl.BlockSpec(shape, lambda *_: (0,) * nd)


def _ada_kernel(c_ref, w_ref, b_ref, o_ref):
    o_ref[...] = _dot(_silu(c_ref[...]), w_ref[...]) + b_ref[...]


def _ada_mod(c_rows, ada_w, ada_b):
    depth, d, six_d = ada_w.shape
    n = six_d // d
    rows = c_rows.shape[0]
    return pl.pallas_call(
        _ada_kernel,
        grid=(depth, n),
        in_specs=[pl.BlockSpec((rows, d), lambda l, j: (0, 0)),
                  pl.BlockSpec((None, d, d), lambda l, j: (l, 0, j)),
                  pl.BlockSpec((None, 1, d), lambda l, j: (l, 0, j))],
        out_specs=pl.BlockSpec((None, rows, d), lambda l, j: (l, 0, j)),
        out_shape=jax.ShapeDtypeStruct((depth, rows, six_d), F32),
        compiler_params=_cparams(("parallel", "parallel")),
        name="ada_mod",
    )(c_rows, ada_w, ada_b.reshape(depth, 1, six_d))


def _norm_mod(x, g, shift, scale):
    y = x * lax.rsqrt(jnp.mean(x * x, axis=-1, keepdims=True) + EPS) * g
    return y * (1.0 + scale) + shift


def _inproj_kernel(*refs, tiles_per_batch, split_in):
    if split_in:
        ctx_ref, x_ref = refs[:2]
        (mod_ref, g_ref, cos_ref, su_ref, sd_ref, wq_ref, wk_ref, wv_ref, wdn_ref, wab_ref, wpool_ref, wg_ref,
         q_ref, k_ref, v_ref, dn_ref, ab_ref, pool_ref, gate_ref, h_out_ref) = refs[2:]
        h = jnp.where(pl.program_id(0) % tiles_per_batch == 0, ctx_ref[...], x_ref[...])
        h_out_ref[...] = h
    else:
        (h_ref, mod_ref, g_ref, cos_ref, su_ref, sd_ref, wq_ref, wk_ref, wv_ref, wdn_ref, wab_ref, wpool_ref,
         wg_ref, q_ref, k_ref, v_ref, dn_ref, ab_ref, pool_ref, gate_ref) = refs
        h = h_ref[...]
    u = _norm_mod(h, g_ref[...], mod_ref[0:1, :], mod_ref[1:2, :]).astype(BF16)
    reps = ATT_W // LANES
    cos = jnp.tile(cos_ref[...], (1, reps))
    s_up = jnp.tile(su_ref[...], (1, reps))
    s_dn = jnp.tile(sd_ref[...], (1, reps))
    quarter = ATT_DH // 4

    def rope(t):
        return (t * cos + pltpu.roll(t, ATT_W - quarter, 1) * s_up + pltpu.roll(t, quarter, 1) * s_dn)

    q = rope(jnp.dot(u, wq_ref[...], preferred_element_type=F32))
    k = rope(jnp.dot(u, wk_ref[...], preferred_element_type=F32))
    v = jnp.dot(u, wv_ref[...], preferred_element_type=F32)
    for hd in range(ATT_HEADS):
        cols = slice(hd * ATT_VD, (hd + 1) * ATT_VD)
        q_ref[hd] = (q[:, cols] * (ATT_DH ** -0.5 * LOG2E)).astype(q_ref.dtype)
        k_ref[hd] = k[:, cols].astype(k_ref.dtype)
        v_ref[hd] = v[:, cols].astype(v_ref.dtype)
    dn_ref[...] = jnp.dot(u, wdn_ref[...], preferred_element_type=F32)
    ab_ref[...] = jnp.dot(u, wab_ref[...], preferred_element_type=F32)
    pool_ref[...] = jnp.dot(u, wpool_ref[...], preferred_element_type=F32)
    gate_ref[...] = _sigmoid(jnp.dot(u, wg_ref[...], preferred_element_type=F32)).astype(gate_ref.dtype)


def _in_projection(h, mod, norm_g, rope_tabs, weights, n_batch, tiles_per_batch):
    split_in = isinstance(h, tuple)
    tpb = tiles_per_batch
    d = mod.shape[-1]
    n_tiles = n_batch * tpb
    rows = n_tiles * ROW_TILE
    widths = [w.shape[1] for w in weights]
    out_dtypes = [BF16, BF16, BF16, F32, F32, F32, BF16]

    def mod_idx(i):
        return (jnp.where(i % tpb == 0, n_batch, i // tpb), 0, 0)

    if split_in:
        h_args = list(h)
        in_specs = [pl.BlockSpec((None, ROW_TILE, d), lambda i: (i // tpb, 0, 0)),
                    pl.BlockSpec((None, ROW_TILE, d), lambda i: (i // tpb, jnp.maximum(i % tpb - 1, 0), 0))]
        widths.append(d)
        out_dtypes.append(F32)
    else:
        h_args = [h]
        in_specs = [pl.BlockSpec((ROW_TILE, d), lambda i: (i, 0))]
    in_specs += [pl.BlockSpec((None, 2, d), mod_idx), _const_spec((1, d))]
    in_specs += [pl.BlockSpec((ROW_TILE, LANES), lambda i: (i % tpb, 0)) for _ in range(3)]
    in_specs += [pl.BlockSpec(w.shape, lambda i: (0, 0), pipeline_mode=pl.Buffered(1)) for w in weights]
    out_specs = [pl.BlockSpec((ROW_TILE, w), lambda i: (i, 0)) for w in widths]
    out_shape = [jax.ShapeDtypeStruct((rows, w), dt) for w, dt in zip(widths, out_dtypes)]
    for o in range(3):
        out_specs[o] = pl.BlockSpec((ATT_HEADS, ROW_TILE, ATT_VD), lambda i: (0, i, 0))
        out_shape[o] = jax.ShapeDtypeStruct((ATT_HEADS, rows, ATT_VD), BF16)
    return pl.pallas_call(
        functools.partial(_inproj_kernel, tiles_per_batch=tpb, split_in=split_in),
        grid=(n_tiles,),
        in_specs=in_specs,
        out_specs=out_specs,
        out_shape=out_shape,
        compiler_params=_cparams(("parallel",)),
        name="in_projection",
    )(*h_args, mod, norm_g.reshape(1, d), *rope_tabs, *weights)


def _rope_tables(n_ctx, seq):
    rows = seq // GRID_W
    row = jnp.repeat(jnp.arange(rows), GRID_W).astype(F32)
    col = jnp.tile(jnp.arange(GRID_W), rows).astype(F32)
    n_freq = ATT_DH // 4
    inv = ROPE_BASE ** (-jnp.arange(n_freq, dtype=F32) / n_freq)
    ar = row[:, None] * inv
    ac = col[:, None] * inv
    cr, sr, cc, sc = jnp.cos(ar), jnp.sin(ar), jnp.cos(ac), jnp.sin(ac)
    z = jnp.zeros_like(sr)
    cos = jnp.concatenate([cr, cr, cc, cc], axis=-1)
    s_up = jnp.concatenate([-sr, z, -sc, z], axis=-1)
    s_dn = jnp.concatenate([z, sr, z, sc], axis=-1)

    def full(t, ctx_val):
        t = jnp.tile(t, (1, LANES // ATT_DH))
        return jnp.concatenate([jnp.full((n_ctx, LANES), ctx_val, F32), t], axis=0)

    return full(cos, 1.0), full(s_up, 0.0), full(s_dn, 0.0)


ATT_KEY_TILE = 1024
LOG2E = math.log2(math.e)


ATT_STREAMS = 2


class _AttnStream:
    def __init__(self, q, e_ref, k_ref, v_ref):
        lane = lax.broadcasted_iota(jnp.int32, q.shape, 1)
        zero = jnp.zeros_like(q)
        self.qm = (jnp.where(lane < ATT_DH, q, zero), jnp.where(lane >= ATT_DH, q, zero))
        self.e_ref, self.k_ref, self.v_ref = e_ref, k_ref, v_ref
        tq = q.shape[0]
        self.m_part = [jnp.full((tq, LANES), -jnp.inf, F32)] * 2
        self.l_part = [jnp.zeros((tq, LANES), F32)] * 2
        self.acc = jnp.zeros((tq, ATT_VD), F32)

    @staticmethod
    def _lane_fold(x, op):
        out = x[:, 0:LANES]
        for c in range(1, x.shape[1] // LANES):
            out = op(out, x[:, c * LANES:(c + 1) * LANES])
        return out

    def scores(self, k0, w):
        kt = self.k_ref[k0:k0 + w, :]
        for m in range(2):
            s = _dot_nt(self.qm[m], kt)
            self.e_ref[m, :, k0:k0 + w] = s
            self.m_part[m] = jnp.maximum(self.m_part[m], self._lane_fold(s, jnp.maximum))

    def end_scores(self):
        self.mx = [jnp.max(p, axis=-1, keepdims=True) for p in self.m_part]

    def exps(self, k0, w):
        for m in range(2):
            e = jnp.exp2(self.e_ref[m, :, k0:k0 + w] - self.mx[m])
            self.e_ref[m, :, k0:k0 + w] = e
            self.l_part[m] = self.l_part[m] + self._lane_fold(e, jnp.add)

    def end_exps(self, lam):
        self.inv0 = 1.0 / jnp.sum(self.l_part[0], axis=-1, keepdims=True)
        self.inv1 = lam / jnp.sum(self.l_part[1], axis=-1, keepdims=True)

    def values(self, k0, w):
        a = self.e_ref[0, :, k0:k0 + w] * self.inv0 - self.e_ref[1, :, k0:k0 + w] * self.inv1
        self.acc = self.acc + jnp.dot(a.astype(BF16), self.v_ref[k0:k0 + w, :], preferred_element_type=F32)

    def result(self, g, lambda_init):
        y = self.acc * lax.rsqrt(jnp.mean(self.acc * self.acc, axis=-1, keepdims=True) + EPS) * g
        return y * (1.0 - lambda_init)


def _attn_run(streams, tiles, lam, g, lambda_init, outs):
    passes = ("scores", "exps", "values")
    for step in range(len(passes) + len(streams) - 1):
        active = [(st, passes[step - i]) for i, st in enumerate(streams) if 0 <= step - i < len(passes)]
        for k0, w in tiles:
            for st, name in active:
                getattr(st, name)(k0, w)
        for st, name in active:
            if name == "scores":
                st.end_scores()
            elif name == "exps":
                st.end_exps(lam)
    for st, (o_ref, rows) in zip(streams, outs):
        o_ref[rows, :] = st.result(g, lambda_init).astype(o_ref.dtype)


def _attn_kernel(*refs, lambda_init, n_ctx, ctx_out):
    if ctx_out:
        lam_ref, g_ref, q_ref, qc_ref, k_ref, v_ref, o_ref, oc_ref, e_ref = refs
    else:
        lam_ref, g_ref, q_ref, k_ref, v_ref, o_ref, e_ref = refs
    n_keys = k_ref.shape[0]
    ctx_tiles = ((0, n_ctx),)
    all_tiles = ctx_tiles + tuple((k0, ATT_KEY_TILE) for k0 in range(n_ctx, n_keys, ATT_KEY_TILE))
    lp = lam_ref[...]
    lam = (jnp.exp(jnp.sum(lp[0:1] * lp[1:2], keepdims=True))
           - jnp.exp(jnp.sum(lp[2:3] * lp[3:4], keepdims=True)) + lambda_init)
    g = g_ref[...]

    if ctx_out:
        @pl.when(pl.program_id(2) == 0)
        def _():
            st = _AttnStream(qc_ref[...], e_ref.at[0], k_ref, v_ref)
            _attn_run([st], ctx_tiles, lam, g, lambda_init, [(oc_ref, slice(None))])

    streams, outs = [], []
    for i in range(ATT_STREAMS):
        rows = slice(i * ROW_TILE, (i + 1) * ROW_TILE)
        streams.append(_AttnStream(q_ref[0, 0, rows, :], e_ref.at[i], k_ref, v_ref))
        outs.append((o_ref, rows))
    _attn_run(streams, all_tiles, lam, g, lambda_init, outs)


def _diff_attention(q, k, v, lam_params, subln_g, lambda_init, n_ctx, ctx_out):
    _, n_batch, t, _ = q.shape
    seq = t - n_ctx
    tq = ATT_STREAMS * ROW_TILE
    assert seq % ATT_KEY_TILE == 0 and seq % tq == 0
    kv_spec = pl.BlockSpec((None, None, t, ATT_VD), lambda b, h, j: (h, b, 0, 0))
    ctx_spec = pl.BlockSpec((None, None, n_ctx, ATT_VD), lambda b, h, j: (h, b, 0, 0))
    in_specs = [_const_spec(lam_params.shape), _const_spec((1, ATT_VD)),
                pl.BlockSpec((pl.Element(1), pl.Element(1), pl.Element(tq), pl.Element(ATT_VD)),
                             lambda b, h, j: (h, b, pl.multiple_of(n_ctx + j * tq, ROW_TILE), 0))]
    out_specs = [pl.BlockSpec((None, None, tq, ATT_VD), lambda b, h, j: (h, b, j, 0))]
    out_shape = [jax.ShapeDtypeStruct((ATT_HEADS, n_batch, seq, ATT_VD), BF16)]
    args = [lam_params, subln_g.reshape(1, ATT_VD), q]
    if ctx_out:
        in_specs.append(ctx_spec)
        args.append(q)
        out_specs.append(ctx_spec)
        out_shape.append(jax.ShapeDtypeStruct((ATT_HEADS, n_batch, n_ctx, ATT_VD), BF16))
    kern = functools.partial(_attn_kernel, lambda_init=lambda_init, n_ctx=n_ctx, ctx_out=ctx_out)
    outs = pl.pallas_call(
        kern,
        grid=(n_batch, ATT_HEADS, seq // tq),
        in_specs=in_specs + [kv_spec, kv_spec],
        out_specs=out_specs,
        out_shape=out_shape,
        scratch_shapes=[pltpu.VMEM((ATT_STREAMS, 2, ROW_TILE, t), F32)],
        compiler_params=_cparams(("parallel", "parallel", "arbitrary")),
        name="diff_attention",
    )(*args, k, v)
    return outs[0], (outs[1] if ctx_out else None)


def _halo_specs(width, tiles_per_batch):
    per = ROW_TILE // HALO
    n_blocks = None

    def prev_idx(b, j):
        return (b, jnp.maximum(j * per - 1, 0), 0)

    def next_idx(b, j):
        return (b, jnp.minimum((j + 1) * per, tiles_per_batch * per - 1), 0)

    del n_blocks
    return (pl.BlockSpec((None, HALO, width), prev_idx), pl.BlockSpec((None, HALO, width), next_idx))


def _with_halo(prev_ref, cur, next_ref, width, tiles_per_batch, j):
    left_ok = jnp.where(j > 1, 1.0, 0.0)
    right_ok = jnp.where(jnp.logical_and(j > 0, j < tiles_per_batch - 1), 1.0, 0.0)
    return jnp.concatenate([prev_ref[:, 0:width] * left_ok, cur, next_ref[:, 0:width] * right_ok], axis=0)


def _dn_prep_kernel(dn_ref, prev_ref, next_ref, ab_ref, cw_ref, alog_ref, dtb_ref,
                    u_ref, w_ref, qg_ref, qk_ref, kt_ref, gl_ref, *, tiles_per_batch):
    j = pl.program_id(1)
    width = 3 * DN_W
    xe = _with_halo(prev_ref, dn_ref[:, 0:width], next_ref, width, tiles_per_batch, j)
    half = DN_CONV // 2
    acc = None
    for tap in range(DN_CONV):
        lo = HALO - half + tap
        term = xe[lo:lo + ROW_TILE, :] * cw_ref[tap:tap + 1, :]
        acc = term if acc is None else acc + term
    qkv = _silu(acc)

    lane_w = lax.broadcasted_iota(jnp.int32, (DN_W, DN_W), 1)
    row_w = lax.broadcasted_iota(jnp.int32, (DN_W, DN_W), 0)
    head_ones = jnp.where((lane_w >> 6) == (row_w >> 6), 1.0, 0.0).astype(BF16)

    def head_sums(x):
        hi = x.astype(BF16)
        lo = (x - hi.astype(F32)).astype(BF16)
        return (jnp.dot(hi, head_ones, preferred_element_type=F32)
                + jnp.dot(lo, head_ones, preferred_element_type=F32))

    def l2n(x):
        return x * lax.rsqrt(head_sums(x * x) + EPS)

    qn = l2n(qkv[:, 0:DN_W]) * (DN_DH ** -0.5)
    kn = l2n(qkv[:, DN_W:2 * DN_W])
    vv = qkv[:, 2 * DN_W:3 * DN_W]
    kn_t = kn.T

    ab = ab_ref[...]
    x = ab + dtb_ref[...]
    g = -jnp.exp(alog_ref[...]) * (jnp.maximum(x, 0.0) + jnp.log(1.0 + jnp.exp(-jnp.abs(x))))
    n_dh = 2 * DN_HEADS
    beta = _sigmoid(pltpu.roll(ab, LANES - n_dh, 1))
    r = lax.broadcasted_iota(jnp.int32, (ROW_TILE, ROW_TILE), 0)
    c = lax.broadcasted_iota(jnp.int32, (ROW_TILE, ROW_TILE), 1)
    same_chunk = (r >> 6) == (c >> 6)
    tri_f = jnp.where(jnp.logical_and(same_chunk, c <= r), 1.0, 0.0)
    tri_b = jnp.where(jnp.logical_and(same_chunk, c >= r), 1.0, 0.0)
    hp = lax.Precision.HIGHEST
    lane = lax.broadcasted_iota(jnp.int32, g.shape, 1)
    gcum = jnp.where(lane < DN_HEADS,
                     jnp.dot(tri_f, g, precision=hp, preferred_element_type=F32),
                     jnp.dot(tri_b, g, precision=hp, preferred_element_type=F32))
    gcum_t = gcum.T

    n_grp = 2 * DN_PAIRS
    sel_r = lax.broadcasted_iota(jnp.int32, (LANES, n_grp * LANES), 0)
    sel_c = lax.broadcasted_iota(jnp.int32, (LANES, n_grp * LANES), 1)
    grp = sel_c >> 7
    sel = jnp.where(sel_r == (grp >> 1) * DN_HEADS + (grp & 1) * 2 + ((sel_c >> 6) & 1), 1.0, 0.0).astype(BF16)

    def spread(x, pieces):
        out, rest = None, x
        for _ in range(pieces):
            part = rest.astype(BF16)
            rest = rest - part.astype(F32)
            term = jnp.dot(part, sel, preferred_element_type=F32)
            out = term if out is None else out + term
        return out

    g_lanes = spread(gcum, 3)
    b_lanes = spread(beta, 3)

    ri = lax.broadcasted_iota(jnp.int32, (DN_CHUNK, LANES), 0)
    ci = lax.broadcasted_iota(jnp.int32, (DN_CHUNK, LANES), 1) & (DN_DH - 1)
    left = lax.broadcasted_iota(jnp.int32, (DN_CHUNK, LANES), 1) < DN_DH
    eye2 = ri == ci
    same_sub = (ri >> 4) == (ci >> 4)
    n_chunks = ROW_TILE // DN_CHUNK

    def bdiag(x):
        return jnp.concatenate([jnp.where(left, x, 0.0), jnp.where(left, 0.0, x)], axis=-2)

    def pmm(a, b):
        return _bmm(a, bdiag(b))

    m_list, xu_list, xw_list = [], [], []
    for ch in range(n_chunks):
        rows = slice(ch * DN_CHUNK, (ch + 1) * DN_CHUNK)
        gram, qkm = [], []
        for p in range(DN_PAIRS):
            cols = slice(p * LANES, (p + 1) * LANES)
            kp = kn[rows, cols]
            kbd = bdiag(kp)
            gram.append(_dot_nt(kp, kbd))
            qkm.append(_dot_nt(qn[rows, cols], kbd))
        for grp_i in range(n_grp):
            d, p = grp_i // DN_PAIRS, grp_i % DN_PAIRS
            cols = slice(p * LANES, (p + 1) * LANES)
            gcols = slice(grp_i * LANES, (grp_i + 1) * LANES)
            incl = (ci <= ri) if d == 0 else (ci >= ri)
            strict = (ci < ri) if d == 0 else (ci > ri)
            last = DN_CHUNK - 1 if d == 0 else 0
            g_col = g_lanes[rows, gcols]
            b_col = b_lanes[rows, gcols]
            g_row = jnp.sum(jnp.where(eye2, g_col, 0.0), axis=0, keepdims=True)
            decay = jnp.where(incl, jnp.exp(jnp.minimum(g_col - g_row, 0.0)), 0.0)
            eg = jnp.exp(g_col)
            kp = kn[rows, cols]
            m_list.append(jnp.where(strict, gram[p] * decay, 0.0) * b_col)
            xu_list.append(vv[rows, cols] * b_col)
            xw_list.append(kp * (b_col * eg))
            qg_ref[grp_i, rows, :] = (qn[rows, cols] * eg).astype(qg_ref.dtype)
            qk_ref[grp_i, rows, :] = (qkm[p] * decay).astype(qk_ref.dtype)
            fac = []
            for hh in range(2):
                idx = d * DN_HEADS + 2 * p + hh
                gr = gcum_t[idx:idx + 1, rows]
                fac.append(jnp.broadcast_to(jnp.exp(gr[:, last:last + 1] - gr), (DN_DH, DN_CHUNK)))
            kt_ref[grp_i, ch] = (kn_t[cols, rows] * jnp.concatenate(fac, axis=0)).astype(kt_ref.dtype)
            gl_ref[grp_i, ch * HALO:(ch + 1) * HALO, :] = jnp.broadcast_to(eg[last:last + 1, :], (HALO, LANES))
    m = jnp.stack(m_list)
    n_diag = jnp.where(same_sub, m, 0.0)
    m_off = m - n_diag
    xp = -n_diag
    dinv = jnp.where(eye2, 1.0, 0.0) + xp
    for _ in range(3):
        xp = pmm(xp, xp)
        dinv = dinv + pmm(dinv, xp)
    f = pmm(dinv, m_off)
    f2 = pmm(f, f)

    def bdiag2(y):
        return jnp.concatenate([bdiag(y[..., 0:LANES]), bdiag(y[..., LANES:2 * LANES])], axis=-1)

    y = _bmm(dinv, jnp.concatenate([bdiag(jnp.stack(xu_list)), bdiag(jnp.stack(xw_list))], axis=-1))
    y = y - _bmm(f, bdiag2(y))
    sol = y + _bmm(f2, bdiag2(y))
    for ch in range(n_chunks):
        rows = slice(ch * DN_CHUNK, (ch + 1) * DN_CHUNK)
        for grp_i in range(n_grp):
            i = ch * n_grp + grp_i
            u_ref[grp_i, rows, :] = sol[i, :, 0:LANES]
            w_ref[grp_i, rows, :] = sol[i, :, LANES:2 * LANES].astype(w_ref.dtype)


def _dn_prep(dn, ab, conv_w, a_log, dt_bias, tiles_per_batch):
    n_batch, t, dn_cols = dn.shape
    n_dh = 2 * DN_HEADS
    n_grp = 2 * DN_PAIRS
    pad = jnp.zeros((1, LANES - n_dh), F32)
    alog_row = jnp.concatenate([a_log.reshape(1, n_dh).astype(F32), pad], axis=-1)
    dtb_row = jnp.concatenate([dt_bias.reshape(1, n_dh).astype(F32), pad], axis=-1)
    prev_spec, next_spec = _halo_specs(dn_cols, tiles_per_batch)
    tok = lambda b, j: (b, 0, j, 0)
    out_specs = [pl.BlockSpec((None, n_grp, ROW_TILE, LANES), tok)] * 4
    out_specs += [pl.BlockSpec((None, n_grp, ROW_TILE // DN_CHUNK, LANES, DN_CHUNK), lambda b, j: (b, 0, j, 0, 0)),
                  pl.BlockSpec((None, n_grp, HALO * ROW_TILE // DN_CHUNK, LANES), tok)]
    tok_shape = (n_batch, n_grp, t, LANES)
    out_shape = [jax.ShapeDtypeStruct(tok_shape, F32),
                 jax.ShapeDtypeStruct(tok_shape, BF16),
                 jax.ShapeDtypeStruct(tok_shape, BF16),
                 jax.ShapeDtypeStruct(tok_shape, BF16),
                 jax.ShapeDtypeStruct((n_batch, n_grp, t // DN_CHUNK, LANES, DN_CHUNK), BF16),
                 jax.ShapeDtypeStruct((n_batch, n_grp, HALO * t // DN_CHUNK, LANES), F32)]
    return pl.pallas_call(
        functools.partial(_dn_prep_kernel, tiles_per_batch=tiles_per_batch),
        grid=(n_batch, tiles_per_batch),
        in_specs=[pl.BlockSpec((None, ROW_TILE, dn_cols), lambda b, j: (b, j, 0)),
                  prev_spec, next_spec,
                  pl.BlockSpec((None, ROW_TILE, LANES), lambda b, j: (b, j, 0)),
                  _const_spec(conv_w.shape), _const_spec((1, LANES)), _const_spec((1, LANES))],
        out_specs=out_specs,
        out_shape=out_shape,
        compiler_params=_cparams(("parallel", "parallel")),
        name="dn_prep",
    )(dn, dn, dn, ab, conv_w, alog_row, dtb_row)


def _dn_scan_kernel(*refs):
    ins, (of_ref, ob_ref, s_ref) = refs[:12], refs[12:]

    @pl.when(pl.program_id(0) == 0)
    def _():
        s_ref[...] = jnp.zeros_like(s_ref)

    left = lax.broadcasted_iota(jnp.int32, (DN_DH, LANES), 1) < DN_DH

    def bdiag(x):
        return jnp.concatenate([jnp.where(left, x, 0.0), jnp.where(left, 0.0, x)], axis=-2)

    for d, o_ref in enumerate((of_ref, ob_ref)):
        u_ref, w_ref, qg_ref, qk_ref, kt_ref, gl_ref = ins[6 * d:6 * d + 6]
        flat = lambda r: r[...].reshape((-1,) + r.shape[2:])
        s = s_ref[d]
        s_bd = bdiag(s)
        both = _bmm(jnp.concatenate([flat(w_ref), flat(qg_ref)], axis=1), s_bd)
        v_new = flat(u_ref) - both[:, 0:DN_CHUNK, :]
        o = both[:, DN_CHUNK:2 * DN_CHUNK, :] + _bmm(flat(qk_ref), bdiag(v_new))
        o_ref[...] = o.reshape(o_ref.shape)
        full = _bmm(flat(kt_ref), v_new)
        s_ref[d] = s * flat(gl_ref)[:, 0:1, :] + jnp.where(left, full[:, 0:DN_DH, :], full[:, DN_DH:2 * DN_DH, :])


def _dn_scan(prep, n_ctx):
    u, w, qg, qk, kt, gl = prep
    n_batch, _, t, _ = u.shape
    n_steps = t // DN_CHUNK
    ctx_chunks = n_ctx // DN_CHUNK

    def chunk_of(d, i):
        if d == 0:
            return i
        return jnp.where(i < ctx_chunks, ctx_chunks - 1 - i, n_steps - 1 + ctx_chunks - i)

    in_specs = []
    for d in range(2):
        tok = functools.partial(lambda i, d: (0, d, chunk_of(d, i), 0), d=d)
        in_specs += [pl.BlockSpec((n_batch, DN_PAIRS, DN_CHUNK, LANES), tok)] * 4
        in_specs += [pl.BlockSpec((n_batch, DN_PAIRS, None, LANES, DN_CHUNK),
                                  functools.partial(lambda i, d: (0, d, chunk_of(d, i), 0, 0), d=d)),
                     pl.BlockSpec((n_batch, DN_PAIRS, HALO, LANES), tok)]
    out_specs = [pl.BlockSpec((n_batch, DN_PAIRS, DN_CHUNK, LANES),
                              functools.partial(lambda i, d: (0, 0, chunk_of(d, i), 0), d=d))
                 for d in range(2)]
    out_shape = [jax.ShapeDtypeStruct((n_batch, DN_PAIRS, t, LANES), F32)] * 2
    return pl.pallas_call(
        _dn_scan_kernel,
        grid=(n_steps,),
        in_specs=in_specs,
        out_specs=out_specs,
        out_shape=out_shape,
        scratch_shapes=[pltpu.VMEM((2, n_batch * DN_PAIRS, DN_DH, LANES), F32)],
        compiler_params=_cparams(("arbitrary",)),
        name="dn_scan",
    )(u, w, qg, qk, kt, gl, u, w, qg, qk, kt, gl)


def _merge_kernel(h_ref, mod_ref, att_ref, attc_ref, of_ref, ob_ref, z_ref, dng_ref,
                  pool_ref, pprev_ref, pnext_ref, pw_ref, ps_ref, gate_ref,
                  wba_ref, wbd_ref, wbp_ref, wo_ref, o_ref, *, tiles_per_batch, first_tile, seq, n_ctx):
    j = pl.program_id(1) + first_tile
    d = h_ref.shape[-1]
    o = jnp.concatenate([of_ref[p] + ob_ref[p] for p in range(DN_PAIRS)], axis=-1)
    lane_w = lax.broadcasted_iota(jnp.int32, (DN_W, DN_W), 1)
    row_w = lax.broadcasted_iota(jnp.int32, (DN_W, DN_W), 0)
    head_ones = jnp.where((lane_w >> 6) == (row_w >> 6), 1.0, 0.0).astype(BF16)
    sq = o * o
    sq_hi = sq.astype(BF16)
    sq_lo = (sq - sq_hi.astype(F32)).astype(BF16)
    ssq = (jnp.dot(sq_hi, head_ones, preferred_element_type=F32)
           + jnp.dot(sq_lo, head_ones, preferred_element_type=F32))
    dn = o * lax.rsqrt(ssq * (1.0 / DN_DH) + EPS) * dng_ref[...] * _silu(z_ref[...])
    cur = pool_ref[...]
    xe = _with_halo(pprev_ref, cur, pnext_ref, POOL_W, tiles_per_batch, j)
    length = jnp.where(j == 0, n_ctx, seq)
    t0 = jnp.where(j == 0, 0, (j - 1) * ROW_TILE)
    tpos = t0 + lax.broadcasted_iota(jnp.int32, (ROW_TILE, POOL_W), 0)
    lane = lax.broadcasted_iota(jnp.int32, (ROW_TILE, POOL_W), 1)
    n_ext = ROW_TILE + 2 * HALO
    sums = {1: xe}
    win = 1
    while win < max(POOL_WINDOWS):
        prev = sums[win]
        rows = prev.shape[0] - win
        sums[2 * win] = prev[0:rows, :] + prev[win:win + rows, :]
        win *= 2
    del n_ext
    mean = jnp.zeros((ROW_TILE, POOL_W), F32)
    for gi, win in enumerate(POOL_WINDOWS):
        start = HALO - win // 2
        wsum = sums[win][start:start + ROW_TILE, :]
        lo = jnp.clip(tpos - win // 2, 0, length)
        hi = jnp.clip(tpos - win // 2 + win, 0, length)
        mean = jnp.where((lane >> 6) == gi, wsum / (hi - lo).astype(F32), mean)
    pooled = _dot(mean - cur, pw_ref[...]) * ps_ref[...]
    ga = gate_ref[:, 0:d].astype(F32)
    gd = gate_ref[:, d:2 * d].astype(F32)
    gp = gate_ref[:, 2 * d:3 * d].astype(F32)
    att = jnp.concatenate([jnp.where(j == 0, attc_ref[hd], att_ref[hd])
                           for hd in range(ATT_HEADS)], axis=-1)
    mix = (ga * jnp.dot(att, wba_ref[...], preferred_element_type=F32)
           + gd * _dot(dn, wbd_ref[...]) + gp * _dot(pooled, wbp_ref[...]))
    y = _dot(mix, wo_ref[...])
    o_ref[...] = h_ref[...] + mod_ref[0:1, :] * y


def _merge(h, mod_gate, att, att_ctx, o_f, o_b, dn, dn_norm_g, pool, pool_w_bd, pool_scale, gates,
           wb_att, wb_dn, wb_pool, w_out, first_tile, n_ctx):
    n_batch, t, d = h.shape
    tpb = t // ROW_TILE
    n_rows = tpb - first_tile
    seq = t - n_ctx
    ft = first_tile
    prev_spec, next_spec = _halo_specs(POOL_W, tpb)
    shift = lambda f: (lambda b, j: f(b, j + ft))
    prev_spec = pl.BlockSpec(prev_spec.block_shape, shift(prev_spec.index_map))
    next_spec = pl.BlockSpec(next_spec.block_shape, shift(next_spec.index_map))
    row = lambda b, j: (b, j + ft, 0)

    def mod_idx(b, j):
        return (jnp.where(j + ft == 0, n_batch, b), 0, 0)

    weights = [wb_att, wb_dn, wb_pool, w_out]
    kern = functools.partial(_merge_kernel, tiles_per_batch=tpb, first_tile=ft, seq=seq, n_ctx=n_ctx)
    return pl.pallas_call(
        kern,
        grid=(n_batch, n_rows),
        in_specs=[pl.BlockSpec((None, ROW_TILE, d), row),
                  pl.BlockSpec((None, 1, d), mod_idx),
                  pl.BlockSpec((ATT_HEADS, None, ROW_TILE, ATT_VD),
                               lambda b, j: (0, b, jnp.maximum(j + ft - 1, 0), 0)),
                  pl.BlockSpec((ATT_HEADS, None, ROW_TILE, ATT_VD), lambda b, j: (0, b, 0, 0)),
                  pl.BlockSpec((None, DN_PAIRS, ROW_TILE, LANES), lambda b, j: (b, 0, j + ft, 0)),
                  pl.BlockSpec((None, DN_PAIRS, ROW_TILE, LANES), lambda b, j: (b, 0, j + ft, 0)),
                  pl.BlockSpec((None, ROW_TILE, DN_W), lambda b, j: (b, j + ft, 3)),
                  _const_spec((1, DN_W)),
                  pl.BlockSpec((None, ROW_TILE, POOL_W), row),
                  prev_spec, next_spec,
                  _const_spec(pool_w_bd.shape), _const_spec((1, POOL_W)),
                  pl.BlockSpec((None, ROW_TILE, 3 * d), row)]
                 + [_const_spec(w.shape) for w in weights],
        out_specs=pl.BlockSpec((None, ROW_TILE, d), lambda b, j: (b, j, 0)),
        out_shape=jax.ShapeDtypeStruct((n_batch, n_rows * ROW_TILE, d), F32),
        compiler_params=_cparams(("parallel", "parallel")),
        name="merge",
    )(h, mod_gate, att, att if att_ctx is None else att_ctx, o_f, o_b, dn,
      jnp.tile(dn_norm_g.reshape(1, DN_DH), (1, DN_HEADS)),
      pool, pool, pool,
      pool_w_bd, pool_scale.reshape(1, POOL_W), gates, *weights)


def _ffn_kernel(*refs, n_experts, n_fchunks, final_norm):
    if n_experts > 1:
        h_ref, mod_ref, g_ref, fg_ref, rw_ref, w1_ref, w3_ref, w2_ref, o_ref, u_ref, acc_ref, comb_ref = refs
    else:
        h_ref, mod_ref, g_ref, fg_ref, w1_ref, w3_ref, w2_ref, o_ref, u_ref, acc_ref = refs
    step = pl.program_id(2)
    e = step // n_fchunks

    @pl.when(step == 0)
    def _():
        u = _norm_mod(h_ref[...], g_ref[...], mod_ref[0:1, :], mod_ref[1:2, :])
        u_ref[...] = u.astype(BF16)
        acc_ref[...] = jnp.zeros_like(acc_ref)
        if n_experts > 1:
            logits = _dot(u, rw_ref[...])
            lane = lax.broadcasted_iota(jnp.int32, logits.shape, 1).astype(F32)
            neg = jnp.float32(-jnp.inf)
            lg = jnp.where(lane < n_experts, logits, neg)
            v1 = jnp.max(lg, axis=-1, keepdims=True)
            i1 = jnp.min(jnp.where(lg == v1, lane, float(LANES)), axis=-1, keepdims=True)
            lg2 = jnp.where(lane == i1, neg, lg)
            v2 = jnp.max(lg2, axis=-1, keepdims=True)
            i2 = jnp.min(jnp.where(lg2 == v2, lane, float(LANES)), axis=-1, keepdims=True)
            e2 = jnp.exp(v2 - v1)
            den = 1.0 + e2
            comb_ref[...] = jnp.where(lane == i1, 1.0 / den, 0.0) + jnp.where(lane == i2, e2 / den, 0.0)

    u = u_ref[...]
    hid = _silu(jnp.dot(u, w1_ref[...], preferred_element_type=F32)) * jnp.dot(
        u, w3_ref[...], preferred_element_type=F32)
    y = _dot(hid, w2_ref[...])
    if n_experts > 1:
        lane = lax.broadcasted_iota(jnp.int32, comb_ref.shape, 1)
        y = y * jnp.sum(jnp.where(lane == e, comb_ref[...], 0.0), axis=-1, keepdims=True)
    acc_ref[...] += y

    @pl.when(step == n_experts * n_fchunks - 1)
    def _():
        out = h_ref[...] + mod_ref[2:3, :] * acc_ref[...]
        if final_norm:
            out = out * lax.rsqrt(jnp.mean(out * out, axis=-1, keepdims=True) + EPS) * fg_ref[...]
        o_ref[...] = out


def _ffn(h, mod, norm_g, final_g, router_w, w1, w3, w2, has_ctx, final_norm):
    n_batch, t, d = h.shape
    n_exp, _, f = w1.shape
    n_rows = t // ROW_TILE
    row = lambda b, j, e: (b, j, 0)

    def mod_idx(b, j, e):
        return (jnp.where(jnp.logical_and(has_ctx, j == 0), n_batch, b), 0, 0)

    in_specs = [pl.BlockSpec((None, ROW_TILE, d), row),
                pl.BlockSpec((None, 3, d), mod_idx),
                _const_spec((1, d)), _const_spec((1, d))]
    args = [h, mod, norm_g.reshape(1, d), final_g.reshape(1, d)]
    scratch = [pltpu.VMEM((ROW_TILE, d), BF16), pltpu.VMEM((ROW_TILE, d), F32)]
    if n_exp > 1:
        in_specs.append(_const_spec(router_w.shape))
        args.append(router_w)
        scratch.append(pltpu.VMEM((ROW_TILE, LANES), F32))
    nfc = 1 if n_exp == 1 else 2
    fc = f // nfc
    assert fc * nfc == f and fc % LANES == 0
    in_specs += [pl.BlockSpec((None, d, fc), lambda b, j, s: (s // nfc, 0, s % nfc)),
                 pl.BlockSpec((None, d, fc), lambda b, j, s: (s // nfc, 0, s % nfc)),
                 pl.BlockSpec((None, fc, d), lambda b, j, s: (s // nfc, s % nfc, 0))]
    args += [w1, w3, w2]
    return pl.pallas_call(
        functools.partial(_ffn_kernel, n_experts=n_exp, n_fchunks=nfc, final_norm=final_norm),
        grid=(n_batch, n_rows, n_exp * nfc),
        in_specs=in_specs,
        out_specs=pl.BlockSpec((None, ROW_TILE, d), row),
        out_shape=jax.ShapeDtypeStruct((n_batch, t, d), F32),
        scratch_shapes=scratch,
        input_output_aliases={0: 0},
        compiler_params=_cparams(("parallel", "parallel", "arbitrary")),
        name="ffn_all_experts" if n_exp > 1 else "ffn_dense",
    )(*args)


MOE_BLOCK = 2048
MOE_SUB = 512
MOE_TILE = 288
MOE_ROW_ALIGN = 32
MOE_ROWS = -(-MOE_BLOCK // MOE_TILE) * MOE_TILE
MOE_PUT_TILE = 256
MOE_FCHUNKS = 4


def _top2_routing(logits, n_experts):
    lane = lax.broadcasted_iota(jnp.int32, logits.shape, 1).astype(F32)
    neg = jnp.float32(-jnp.inf)
    lg = jnp.where(lane < n_experts, logits, neg)
    v1 = jnp.max(lg, axis=-1, keepdims=True)
    i1 = jnp.min(jnp.where(lg == v1, lane, float(LANES)), axis=-1, keepdims=True)
    lg2 = jnp.where(lane == i1, neg, lg)
    v2 = jnp.max(lg2, axis=-1, keepdims=True)
    i2 = jnp.min(jnp.where(lg2 == v2, lane, float(LANES)), axis=-1, keepdims=True)
    e2 = jnp.exp(v2 - v1)
    den = 1.0 + e2
    first = lane == i1
    second = lane == i2
    sel = jnp.where(jnp.logical_or(first, second), 1.0, 0.0)
    comb = jnp.where(first, 1.0 / den, 0.0) + jnp.where(second, e2 / den, 0.0)
    return sel, comb


def _moe_kernel(h_ref, mod_ref, g_ref, fg_ref, rw_ref, w1_ref, w3_ref, w2_ref, o_ref,
                u_ref, x_ref, y_ref, rank_ref, rank_t_ref, comb_ref, cnt_ref, span_ref, *, n_experts, final_norm):
    e = pl.program_id(2)
    f = pl.program_id(3)
    n_sub = MOE_BLOCK // MOE_SUB

    @pl.when(jnp.logical_and(e == 0, f == 0))
    def _route():
        r = lax.broadcasted_iota(jnp.int32, (ROW_TILE, ROW_TILE), 0)
        c = lax.broadcasted_iota(jnp.int32, (ROW_TILE, ROW_TILE), 1)
        before = jnp.where(c < r, 1.0, 0.0).astype(BF16)
        count = jnp.zeros((1, LANES), F32)
        for t in range(MOE_BLOCK // ROW_TILE):
            rows = slice(t * ROW_TILE, (t + 1) * ROW_TILE)
            u = _norm_mod(h_ref[rows, :], g_ref[...], mod_ref[0:1, :], mod_ref[1:2, :])
            u_ref[rows, :] = u.astype(BF16)
            sel, comb = _top2_routing(_dot(u, rw_ref[...]), n_experts)
            rank = jnp.dot(before, sel.astype(BF16), preferred_element_type=F32) + count
            rank_ref[rows, :] = jnp.where(sel > 0.0, rank, -1.0)
            comb_ref[rows, :] = comb
            count = count + jnp.sum(sel, axis=0, keepdims=True)
        lane = lax.broadcasted_iota(jnp.int32, (1, LANES), 1)
        for ee in range(n_experts):
            cnt_ref[ee] = jnp.sum(jnp.where(lane == ee, count, 0.0)).astype(jnp.int32)
        rank_t_ref[...] = rank_ref[...].T
        o_ref[...] = jnp.zeros_like(o_ref)
        for s in range(n_sub):
            rk = rank_ref[s * MOE_SUB:(s + 1) * MOE_SUB, :]
            lo_v = jnp.min(jnp.where(rk >= 0.0, rk, float(MOE_BLOCK)), axis=0, keepdims=True)
            hi_v = jnp.max(rk, axis=0, keepdims=True)
            for ee in range(n_experts):
                span_ref[2 * (s * n_experts + ee)] = jnp.sum(jnp.where(lane == ee, lo_v, 0.0)).astype(jnp.int32)
                span_ref[2 * (s * n_experts + ee) + 1] = jnp.sum(jnp.where(lane == ee, hi_v, 0.0)).astype(jnp.int32)

        @pl.when(jnp.logical_and(pl.program_id(0) == 0, pl.program_id(1) == 0))
        def _():
            y_ref[...] = jnp.zeros_like(y_ref)

    n_rows = cnt_ref[e]
    n_tiles = (n_rows + MOE_TILE - 1) // MOE_TILE
    rank_row = rank_t_ref[pl.ds(e, 1), :]

    def sub_hits(row0, s, tile=MOE_TILE):
        rk = rank_row[:, s * MOE_SUB:(s + 1) * MOE_SUB]
        lo = span_ref[2 * (s * n_experts + e)]
        hi = span_ref[2 * (s * n_experts + e) + 1]
        return rk, jnp.logical_and(hi >= row0, lo < row0 + tile)

    @pl.when(f == 0)
    def _compact():
        def body(i, carry):
            row0 = pl.multiple_of(i * MOE_TILE, MOE_ROW_ALIGN)
            ids = (row0 + lax.broadcasted_iota(jnp.int32, (MOE_TILE, 1), 0)).astype(F32)
            x_ref[pl.ds(row0, MOE_TILE), :] = jnp.zeros((MOE_TILE, x_ref.shape[1]), BF16)
            for s in range(n_sub):
                rk, hit = sub_hits(row0, s)

                @pl.when(hit)
                def _():
                    pick = jnp.where(rk == ids, 1.0, 0.0).astype(BF16)
                    got = jnp.dot(pick, u_ref[s * MOE_SUB:(s + 1) * MOE_SUB, :], preferred_element_type=F32)
                    x_ref[pl.ds(row0, MOE_TILE), :] += got.astype(BF16)
            return carry

        lax.fori_loop(0, n_tiles, body, 0)

    def expert_body(i, carry):
        row0 = pl.multiple_of(i * MOE_TILE, MOE_ROW_ALIGN)
        x = x_ref[pl.ds(row0, MOE_TILE), :]
        hid = _silu(jnp.dot(x, w1_ref[...], preferred_element_type=F32)) * jnp.dot(
            x, w3_ref[...], preferred_element_type=F32)
        y = _dot(hid, w2_ref[...])

        @pl.when(f == 0)
        def _():
            y_ref[pl.ds(row0, MOE_TILE), :] = y

        @pl.when(f > 0)
        def _():
            y_ref[pl.ds(row0, MOE_TILE), :] += y

        return carry

    lax.fori_loop(0, n_tiles, expert_body, 0)

    @pl.when(f == MOE_FCHUNKS - 1)
    def _expand():
        lanes = lax.broadcasted_iota(jnp.int32, (MOE_SUB, LANES), 1)

        def body(i, carry):
            row0 = pl.multiple_of(i * MOE_PUT_TILE, MOE_PUT_TILE)
            yt = y_ref[pl.ds(row0, MOE_PUT_TILE), :].astype(BF16)
            ids = (row0 + lax.broadcasted_iota(jnp.int32, (1, MOE_PUT_TILE), 1)).astype(F32)
            for s in range(n_sub):
                _, hit = sub_hits(row0, s, MOE_PUT_TILE)

                @pl.when(hit)
                def _():
                    rows = slice(s * MOE_SUB, (s + 1) * MOE_SUB)
                    rk_col = jnp.sum(jnp.where(lanes == e, rank_ref[rows, :], 0.0), axis=1, keepdims=True)
                    w_col = jnp.sum(jnp.where(lanes == e, comb_ref[rows, :], 0.0), axis=1, keepdims=True)
                    put = jnp.where(rk_col == ids, 1.0, 0.0).astype(BF16)
                    o_ref[rows, :] += w_col * jnp.dot(put, yt, preferred_element_type=F32)
            return carry

        lax.fori_loop(0, (n_rows + MOE_PUT_TILE - 1) // MOE_PUT_TILE, body, 0)

    @pl.when(jnp.logical_and(e == n_experts - 1, f == MOE_FCHUNKS - 1))
    def _finish():
        for t in range(MOE_BLOCK // ROW_TILE):
            rows = slice(t * ROW_TILE, (t + 1) * ROW_TILE)
            out = h_ref[rows, :] + mod_ref[2:3, :] * o_ref[rows, :]
            if final_norm:
                out = out * lax.rsqrt(jnp.mean(out * out, axis=-1, keepdims=True) + EPS) * fg_ref[...]
            o_ref[rows, :] = out


def _moe(h, mod, norm_g, final_g, router_w, w1, w3, w2, final_norm):
    n_batch, s, d = h.shape
    n_exp, _, f = w1.shape
    assert s % MOE_BLOCK == 0 and f % (MOE_FCHUNKS * LANES) == 0 and n_exp <= 8
    fc = f // MOE_FCHUNKS
    blk = lambda b, j, e, c: (b, j, 0)
    single = pl.Buffered(1)

    return pl.pallas_call(
        functools.partial(_moe_kernel, n_experts=n_exp, final_norm=final_norm),
        grid=(n_batch, s // MOE_BLOCK, n_exp, MOE_FCHUNKS),
        in_specs=[pl.BlockSpec((None, MOE_BLOCK, d), blk, pipeline_mode=single),
                  pl.BlockSpec((None, 3, d), lambda b, j, e, c: (b, 0, 0)),
                  _const_spec((1, d)), _const_spec((1, d)), _const_spec(router_w.shape),
                  pl.BlockSpec((None, d, fc), lambda b, j, e, c: (e, 0, c)),
                  pl.BlockSpec((None, d, fc), lambda b, j, e, c: (e, 0, c)),
                  pl.BlockSpec((None, fc, d), lambda b, j, e, c: (e, c, 0))],
        out_specs=pl.BlockSpec((None, MOE_BLOCK, d), blk, pipeline_mode=single),
        out_shape=jax.ShapeDtypeStruct((n_batch, s, d), F32),
        scratch_shapes=[pltpu.VMEM((MOE_BLOCK, d), BF16),
                        pltpu.VMEM((MOE_ROWS, d), BF16),
                        pltpu.VMEM((MOE_ROWS, d), F32),
                        pltpu.VMEM((MOE_BLOCK, LANES), F32),
                        pltpu.VMEM((LANES, MOE_BLOCK), F32),
                        pltpu.VMEM((MOE_BLOCK, LANES), F32),
                        pltpu.SMEM((HALO,), jnp.int32),
                        pltpu.SMEM((2 * (MOE_BLOCK // MOE_SUB) * HALO,), jnp.int32)],
        compiler_params=_cparams(("arbitrary", "arbitrary", "arbitrary", "arbitrary")),
        name="moe_top2",
    )(h, mod, norm_g.reshape(1, d), final_g.reshape(1, d), router_w, w1, w3, w2)


def _block_diag(w):
    g, n, _ = w.shape
    out = jnp.zeros((g * n, g * n), w.dtype)
    for i in range(g):
        out = out.at[i * n:(i + 1) * n, i * n:(i + 1) * n].set(w[i])
    return out


def kernel(x, c, ctx, c_ctx, ada_w, ada_b, norm1_g, norm2_g, w_in, attn_lambda, attn_subln_g, dn_conv_w,
           dn_a_log, dn_dt_bias, dn_norm_g, pool_w, pool_scale, w_branch, w_out, ffn_w1, ffn_w3, ffn_w2,
           router_w, moe_w1, moe_w3, moe_w2, final_norm_g):
    n_batch, seq, d = x.shape
    n_ctx = ctx.shape[1]
    depth = ada_w.shape[0]
    assert n_ctx == ROW_TILE and seq % ROW_TILE == 0 and seq % GRID_W == 0 and d == 1024
    t = n_ctx + seq
    tpb = t // ROW_TILE

    h = None
    c_rows = jnp.concatenate([c, c_ctx[None, :], jnp.zeros((8 - n_batch - 1, d), F32)], axis=0)
    mods = _ada_mod(c_rows, ada_w, ada_b).reshape(depth, 8, 6, d)
    rope_tabs = _rope_tables(n_ctx, seq)

    sizes = (ATT_W, ATT_W, ATT_W, DN_W, DN_W, DN_W, DN_W, 2 * DN_HEADS, 2 * DN_HEADS, POOL_W, 3 * d)
    offs = np.concatenate([[0], np.cumsum(sizes)]).tolist()

    for l in range(depth):
        last = l == depth - 1
        first_tile = 1 if last else 0
        lambda_init = 0.8 - 0.6 * math.exp(-0.3 * l)
        wl = w_in[l].astype(BF16)
        seg = lambda a, b: wl[:, offs[a]:offs[b]]
        w_ab = jnp.concatenate([seg(7, 9), jnp.zeros((d, LANES - 4 * DN_HEADS), BF16)], axis=1)
        weights = [seg(0, 1), seg(1, 2), seg(2, 3), seg(3, 7), w_ab, seg(9, 10), seg(10, 11)]
        mod = mods[l, :n_batch + 1]
        src = (ctx, x) if l == 0 else h.reshape(n_batch * t, d)
        outs = _in_projection(src, mod[:, 0:2], norm1_g[l], rope_tabs, weights, n_batch, tpb)
        q, k, v, dn, ab, pool, gates = outs[:7]
        if l == 0:
            h = outs[7].reshape(n_batch, t, d)
        q, k, v = (a.reshape(ATT_HEADS, n_batch, t, ATT_VD) for a in (q, k, v))
        att, att_ctx = _diff_attention(q, k, v, attn_lambda[l], attn_subln_g[l], lambda_init, n_ctx, not last)
        dn = dn.reshape(n_batch, t, 4 * DN_W)
        prep = _dn_prep(dn, ab.reshape(n_batch, t, LANES), dn_conv_w[l], dn_a_log[l], dn_dt_bias[l], tpb)
        o_f, o_b = _dn_scan(prep, n_ctx)
        wb = w_branch[l].astype(BF16)
        h = _merge(h, mod[:, 2:3], att, att_ctx, o_f, o_b, dn, dn_norm_g[l], pool.reshape(n_batch, t, POOL_W),
                   _block_diag(pool_w[l]).astype(BF16), pool_scale[l], gates.reshape(n_batch, t, 3 * d),
                   wb[:ATT_W], wb[ATT_W:ATT_W + DN_W], wb[ATT_W + DN_W:], w_out[l].astype(BF16),
                   first_tile, n_ctx)
        i = l // 2
        if l % 2 == 0:
            h = _ffn(h, mod[:, 3:6], norm2_g[l], final_norm_g, None, ffn_w1[i][None].astype(BF16),
                     ffn_w3[i][None].astype(BF16), ffn_w2[i][None].astype(BF16), not last, last)
        else:
            rw = jnp.concatenate([router_w[i], jnp.zeros((d, LANES - N_EXPERTS), F32)], axis=1).astype(BF16)
            w1, w3, w2 = moe_w1[i].astype(BF16), moe_w3[i].astype(BF16), moe_w2[i].astype(BF16)
            if last:
                h = _moe(h, mod[:n_batch, 3:6], norm2_g[l], final_norm_g, rw, w1, w3, w2, True)
            else:
                h = _ffn(h, mod[:, 3:6], norm2_g[l], final_norm_g, rw, w1, w3, w2, True, False)
    return h
```

```python
import functools
import math

import numpy as np
import jax
import jax.numpy as jnp
from jax import lax
from jax.experimental import pallas as pl
from jax.experimental.pallas import tpu as pltpu

F32 = jnp.float32
BF16 = jnp.bfloat16

EPS = 1e-6
GRID_W = 64
ROPE_BASE = 10000.0
ATT_HEADS = 4
ATT_DH = 64
ATT_VD = 2 * ATT_DH
ATT_W = ATT_HEADS * ATT_VD
DN_HEADS = 4
DN_DH = 64
DN_W = DN_HEADS * DN_DH
DN_PAIRS = DN_HEADS // 2
DN_CONV = 5
DN_CHUNK = 64
DN_SUB = 16
POOL_WINDOWS = (2, 4, 8, 16)
POOL_GD = 64
POOL_W = len(POOL_WINDOWS) * POOL_GD
N_EXPERTS = 8
TOP_K = 2

ROW_TILE = 256
HALO = 8
LANES = 128
VMEM_LIMIT = 56 * 1024 * 1024


def _cparams(sem):
    return pltpu.CompilerParams(dimension_semantics=sem, vmem_limit_bytes=VMEM_LIMIT)


def _dot(a, b):
    return jnp.dot(a.astype(BF16), b.astype(BF16), preferred_element_type=F32)


def _dot_nt(a, b):
    return lax.dot_general(a.astype(BF16), b.astype(BF16), (((1,), (1,)), ((), ())),
                           preferred_element_type=F32)


def _bmm(a, b):
    return lax.dot_general(a.astype(BF16), b.astype(BF16), (((2,), (1,)), ((0,), (0,))),
                           preferred_element_type=F32)


def _bmm_nt(a, b):
    return lax.dot_general(a.astype(BF16), b.astype(BF16), (((2,), (2,)), ((0,), (0,))),
                           preferred_element_type=F32)


def _sigmoid(x):
    return 1.0 / (1.0 + jnp.exp(-x))


def _silu(x):
    return x * _sigmoid(x)


def _const_spec(shape):
    nd = len(shape)
    return pl.BlockSpec(shape, lambda *_: (0,) * nd)


def _ada_kernel(c_ref, w_ref, b_ref, o_ref):
    o_ref[...] = _dot(_silu(c_ref[...]), w_ref[...]) + b_ref[...]


def _ada_mod(c_rows, ada_w, ada_b):
    depth, d, six_d = ada_w.shape
    n = six_d // d
    rows = c_rows.shape[0]
    return pl.pallas_call(
        _ada_kernel,
        grid=(depth, n),
        in_specs=[pl.BlockSpec((rows, d), lambda l, j: (0, 0)),
                  pl.BlockSpec((None, d, d), lambda l, j: (l, 0, j)),
                  pl.BlockSpec((None, 1, d), lambda l, j: (l, 0, j))],
        out_specs=pl.BlockSpec((None, rows, d), lambda l, j: (l, 0, j)),
        out_shape=jax.ShapeDtypeStruct((depth, rows, six_d), F32),
        compiler_params=_cparams(("parallel", "parallel")),
        name="ada_mod",
    )(c_rows, ada_w, ada_b.reshape(depth, 1, six_d))


def _norm_mod(x, g, shift, scale):
    y = x * lax.rsqrt(jnp.mean(x * x, axis=-1, keepdims=True) + EPS) * g
    return y * (1.0 + scale) + shift


def _inproj_kernel(*refs, tiles_per_batch, split_in):
    if split_in:
        ctx_ref, x_ref = refs[:2]
        (mod_ref, g_ref, cos_ref, su_ref, sd_ref, wq_ref, wk_ref, wv_ref, wdn_ref, wab_ref, wpool_ref, wg_ref,
         q_ref, k_ref, v_ref, dn_ref, ab_ref, pool_ref, gate_ref, h_out_ref) = refs[2:]
        h = jnp.where(pl.program_id(0) % tiles_per_batch == 0, ctx_ref[...], x_ref[...])
        h_out_ref[...] = h
    else:
        (h_ref, mod_ref, g_ref, cos_ref, su_ref, sd_ref, wq_ref, wk_ref, wv_ref, wdn_ref, wab_ref, wpool_ref,
         wg_ref, q_ref, k_ref, v_ref, dn_ref, ab_ref, pool_ref, gate_ref) = refs
        h = h_ref[...]
    u = _norm_mod(h, g_ref[...], mod_ref[0:1, :], mod_ref[1:2, :]).astype(BF16)
    reps = ATT_W // LANES
    cos = jnp.tile(cos_ref[...], (1, reps))
    s_up = jnp.tile(su_ref[...], (1, reps))
    s_dn = jnp.tile(sd_ref[...], (1, reps))
    quarter = ATT_DH // 4

    def rope(t):
        return (t * cos + pltpu.roll(t, ATT_W - quarter, 1) * s_up + pltpu.roll(t, quarter, 1) * s_dn)

    q = rope(jnp.dot(u, wq_ref[...], preferred_element_type=F32))
    k = rope(jnp.dot(u, wk_ref[...], preferred_element_type=F32))
    v = jnp.dot(u, wv_ref[...], preferred_element_type=F32)
    for hd in range(ATT_HEADS):
        cols = slice(hd * ATT_VD, (hd + 1) * ATT_VD)
        q_ref[hd] = (q[:, cols] * (ATT_DH ** -0.5 * LOG2E)).astype(q_ref.dtype)
        k_ref[hd] = k[:, cols].astype(k_ref.dtype)
        v_ref[hd] = v[:, cols].astype(v_ref.dtype)
    dn_ref[...] = jnp.dot(u, wdn_ref[...], preferred_element_type=F32)
    ab_ref[...] = jnp.dot(u, wab_ref[...], preferred_element_type=F32)
    pool_ref[...] = jnp.dot(u, wpool_ref[...], preferred_element_type=F32)
    gate_ref[...] = _sigmoid(jnp.dot(u, wg_ref[...], preferred_element_type=F32)).astype(gate_ref.dtype)


def _in_projection(h, mod, norm_g, rope_tabs, weights, n_batch, tiles_per_batch):
    split_in = isinstance(h, tuple)
    tpb = tiles_per_batch
    d = mod.shape[-1]
    n_tiles = n_batch * tpb
    rows = n_tiles * ROW_TILE
    widths = [w.shape[1] for w in weights]
    out_dtypes = [BF16, BF16, BF16, F32, F32, F32, BF16]

    def mod_idx(i):
        return (jnp.where(i % tpb == 0, n_batch, i // tpb), 0, 0)

    if split_in:
        h_args = list(h)
        in_specs = [pl.BlockSpec((None, ROW_TILE, d), lambda i: (i // tpb, 0, 0)),
                    pl.BlockSpec((None, ROW_TILE, d), lambda i: (i // tpb, jnp.maximum(i % tpb - 1, 0), 0))]
        widths.append(d)
        out_dtypes.append(F32)
    else:
        h_args = [h]
        in_specs = [pl.BlockSpec((ROW_TILE, d), lambda i: (i, 0))]
    in_specs += [pl.BlockSpec((None, 2, d), mod_idx), _const_spec((1, d))]
    in_specs += [pl.BlockSpec((ROW_TILE, LANES), lambda i: (i % tpb, 0)) for _ in range(3)]
    in_specs += [pl.BlockSpec(w.shape, lambda i: (0, 0), pipeline_mode=pl.Buffered(1)) for w in weights]
    out_specs = [pl.BlockSpec((ROW_TILE, w), lambda i: (i, 0)) for w in widths]
    out_shape = [jax.ShapeDtypeStruct((rows, w), dt) for w, dt in zip(widths, out_dtypes)]
    for o in range(3):
        out_specs[o] = pl.BlockSpec((ATT_HEADS, ROW_TILE, ATT_VD), lambda i: (0, i, 0))
        out_shape[o] = jax.ShapeDtypeStruct((ATT_HEADS, rows, ATT_VD), BF16)
    return pl.pallas_call(
        functools.partial(_inproj_kernel, tiles_per_batch=tpb, split_in=split_in),
        grid=(n_tiles,),
        in_specs=in_specs,
        out_specs=out_specs,
        out_shape=out_shape,
        compiler_params=_cparams(("parallel",)),
        name="in_projection",
    )(*h_args, mod, norm_g.reshape(1, d), *rope_tabs, *weights)


def _rope_tables(n_ctx, seq):
    rows = seq // GRID_W
    row = jnp.repeat(jnp.arange(rows), GRID_W).astype(F32)
    col = jnp.tile(jnp.arange(GRID_W), rows).astype(F32)
    n_freq = ATT_DH // 4
    inv = ROPE_BASE ** (-jnp.arange(n_freq, dtype=F32) / n_freq)
    ar = row[:, None] * inv
    ac = col[:, None] * inv
    cr, sr, cc, sc = jnp.cos(ar), jnp.sin(ar), jnp.cos(ac), jnp.sin(ac)
    z = jnp.zeros_like(sr)
    cos = jnp.concatenate([cr, cr, cc, cc], axis=-1)
    s_up = jnp.concatenate([-sr, z, -sc, z], axis=-1)
    s_dn = jnp.concatenate([z, sr, z, sc], axis=-1)

    def full(t, ctx_val):
        t = jnp.tile(t, (1, LANES // ATT_DH))
        return jnp.concatenate([jnp.full((n_ctx, LANES), ctx_val, F32), t], axis=0)

    return full(cos, 1.0), full(s_up, 0.0), full(s_dn, 0.0)


ATT_KEY_TILE = 1024
LOG2E = math.log2(math.e)


ATT_STREAMS = 2


class _AttnStream:
    def __init__(self, q, e_ref, k_ref, v_ref):
        lane = lax.broadcasted_iota(jnp.int32, q.shape, 1)
        zero = jnp.zeros_like(q)
        self.qm = (jnp.where(lane < ATT_DH, q, zero), jnp.where(lane >= ATT_DH, q, zero))
        self.e_ref, self.k_ref, self.v_ref = e_ref, k_ref, v_ref
        tq = q.shape[0]
        self.m_part = [jnp.full((tq, LANES), -jnp.inf, F32)] * 2
        self.l_part = [jnp.zeros((tq, LANES), F32)] * 2
        self.acc = jnp.zeros((tq, ATT_VD), F32)

    @staticmethod
    def _lane_fold(x, op):
        out = x[:, 0:LANES]
        for c in range(1, x.shape[1] // LANES):
            out = op(out, x[:, c * LANES:(c + 1) * LANES])
        return out

    def scores(self, k0, w):
        kt = self.k_ref[k0:k0 + w, :]
        for m in range(2):
            s = _dot_nt(self.qm[m], kt)
            self.e_ref[m, :, k0:k0 + w] = s
            self.m_part[m] = jnp.maximum(self.m_part[m], self._lane_fold(s, jnp.maximum))

    def end_scores(self):
        self.mx = [jnp.max(p, axis=-1, keepdims=True) for p in self.m_part]

    def exps(self, k0, w):
        for m in range(2):
            e = jnp.exp2(self.e_ref[m, :, k0:k0 + w] - self.mx[m])
            self.e_ref[m, :, k0:k0 + w] = e
            self.l_part[m] = self.l_part[m] + self._lane_fold(e, jnp.add)

    def end_exps(self, lam):
        self.inv0 = 1.0 / jnp.sum(self.l_part[0], axis=-1, keepdims=True)
        self.inv1 = lam / jnp.sum(self.l_part[1], axis=-1, keepdims=True)

    def values(self, k0, w):
        a = self.e_ref[0, :, k0:k0 + w] * self.inv0 - self.e_ref[1, :, k0:k0 + w] * self.inv1
        self.acc = self.acc + jnp.dot(a.astype(BF16), self.v_ref[k0:k0 + w, :], preferred_element_type=F32)

    def result(self, g, lambda_init):
        y = self.acc * lax.rsqrt(jnp.mean(self.acc * self.acc, axis=-1, keepdims=True) + EPS) * g
        return y * (1.0 - lambda_init)


def _attn_run(streams, tiles, lam, g, lambda_init, outs):
    passes = ("scores", "exps", "values")
    for step in range(len(passes) + len(streams) - 1):
        active = [(st, passes[step - i]) for i, st in enumerate(streams) if 0 <= step - i < len(passes)]
        for k0, w in tiles:
            for st, name in active:
                getattr(st, name)(k0, w)
        for st, name in active:
            if name == "scores":
                st.end_scores()
            elif name == "exps":
                st.end_exps(lam)
    for st, (o_ref, rows) in zip(streams, outs):
        o_ref[rows, :] = st.result(g, lambda_init).astype(o_ref.dtype)


def _attn_kernel(*refs, lambda_init, n_ctx, ctx_out):
    if ctx_out:
        lam_ref, g_ref, q_ref, qc_ref, k_ref, v_ref, o_ref, oc_ref, e_ref = refs
    else:
        lam_ref, g_ref, q_ref, k_ref, v_ref, o_ref, e_ref = refs
    n_keys = k_ref.shape[0]
    ctx_tiles = ((0, n_ctx),)
    all_tiles = ctx_tiles + tuple((k0, ATT_KEY_TILE) for k0 in range(n_ctx, n_keys, ATT_KEY_TILE))
    lp = lam_ref[...]
    lam = (jnp.exp(jnp.sum(lp[0:1] * lp[1:2], keepdims=True))
           - jnp.exp(jnp.sum(lp[2:3] * lp[3:4], keepdims=True)) + lambda_init)
    g = g_ref[...]

    if ctx_out:
        @pl.when(pl.program_id(2) == 0)
        def _():
            st = _AttnStream(qc_ref[...], e_ref.at[0], k_ref, v_ref)
            _attn_run([st], ctx_tiles, lam, g, lambda_init, [(oc_ref, slice(None))])

    streams, outs = [], []
    for i in range(ATT_STREAMS):
        rows = slice(i * ROW_TILE, (i + 1) * ROW_TILE)
        streams.append(_AttnStream(q_ref[0, 0, rows, :], e_ref.at[i], k_ref, v_ref))
        outs.append((o_ref, rows))
    _attn_run(streams, all_tiles, lam, g, lambda_init, outs)


def _diff_attention(q, k, v, lam_params, subln_g, lambda_init, n_ctx, ctx_out):
    _, n_batch, t, _ = q.shape
    seq = t - n_ctx
    tq = ATT_STREAMS * ROW_TILE
    assert seq % ATT_KEY_TILE == 0 and seq % tq == 0
    kv_spec = pl.BlockSpec((None, None, t, ATT_VD), lambda b, h, j: (h, b, 0, 0))
    ctx_spec = pl.BlockSpec((None, None, n_ctx, ATT_VD), lambda b, h, j: (h, b, 0, 0))
    in_specs = [_const_spec(lam_params.shape), _const_spec((1, ATT_VD)),
                pl.BlockSpec((pl.Element(1), pl.Element(1), pl.Element(tq), pl.Element(ATT_VD)),
                             lambda b, h, j: (h, b, pl.multiple_of(n_ctx + j * tq, ROW_TILE), 0))]
    out_specs = [pl.BlockSpec((None, None, tq, ATT_VD), lambda b, h, j: (h, b, j, 0))]
    out_shape = [jax.ShapeDtypeStruct((ATT_HEADS, n_batch, seq, ATT_VD), BF16)]
    args = [lam_params, subln_g.reshape(1, ATT_VD), q]
    if ctx_out:
        in_specs.append(ctx_spec)
        args.append(q)
        out_specs.append(ctx_spec)
        out_shape.append(jax.ShapeDtypeStruct((ATT_HEADS, n_batch, n_ctx, ATT_VD), BF16))
    kern = functools.partial(_attn_kernel, lambda_init=lambda_init, n_ctx=n_ctx, ctx_out=ctx_out)
    outs = pl.pallas_call(
        kern,
        grid=(n_batch, ATT_HEADS, seq // tq),
        in_specs=in_specs + [kv_spec, kv_spec],
        out_specs=out_specs,
        out_shape=out_shape,
        scratch_shapes=[pltpu.VMEM((ATT_STREAMS, 2, ROW_TILE, t), F32)],
        compiler_params=_cparams(("parallel", "parallel", "arbitrary")),
        name="diff_attention",
    )(*args, k, v)
    return outs[0], (outs[1] if ctx_out else None)


def _halo_specs(width, tiles_per_batch):
    per = ROW_TILE // HALO
    n_blocks = None

    def prev_idx(b, j):
        return (b, jnp.maximum(j * per - 1, 0), 0)

    def next_idx(b, j):
        return (b, jnp.minimum((j + 1) * per, tiles_per_batch * per - 1), 0)

    del n_blocks
    return (pl.BlockSpec((None, HALO, width), prev_idx), pl.BlockSpec((None, HALO, width), next_idx))


def _with_halo(prev_ref, cur, next_ref, width, tiles_per_batch, j):
    left_ok = jnp.where(j > 1, 1.0, 0.0)
    right_ok = jnp.where(jnp.logical_and(j > 0, j < tiles_per_batch - 1), 1.0, 0.0)
    return jnp.concatenate([prev_ref[:, 0:width] * left_ok, cur, next_ref[:, 0:width] * right_ok], axis=0)


def _dn_prep_kernel(dn_ref, prev_ref, next_ref, ab_ref, cw_ref, alog_ref, dtb_ref,
                    u_ref, w_ref, qg_ref, qk_ref, kt_ref, gl_ref, *, tiles_per_batch):
    j = pl.program_id(1)
    width = 3 * DN_W
    xe = _with_halo(prev_ref, dn_ref[:, 0:width], next_ref, width, tiles_per_batch, j)
    half = DN_CONV // 2
    acc = None
    n_ext = ROW_TILE + 2 * HALO
    for tap in range(DN_CONV):
        shifted = xe if tap == half else pltpu.roll(xe, (half - tap) % n_ext, 0)
        term = shifted[HALO:HALO + ROW_TILE, :] * cw_ref[tap:tap + 1, :]
        acc = term if acc is None else acc + term
    qkv = _silu(acc)

    lane_w = lax.broadcasted_iota(jnp.int32, (DN_W, DN_W), 1)
    row_w = lax.broadcasted_iota(jnp.int32, (DN_W, DN_W), 0)
    head_ones = jnp.where((lane_w >> 6) == (row_w >> 6), 1.0, 0.0).astype(BF16)

    def head_sums(x):
        hi = x.astype(BF16)
        lo = (x - hi.astype(F32)).astype(BF16)
        return (jnp.dot(hi, head_ones, preferred_element_type=F32)
                + jnp.dot(lo, head_ones, preferred_element_type=F32))

    def l2n(x):
        return x * lax.rsqrt(head_sums(x * x) + EPS)

    qn = l2n(qkv[:, 0:DN_W]) * (DN_DH ** -0.5)
    kn = l2n(qkv[:, DN_W:2 * DN_W])
    vv = qkv[:, 2 * DN_W:3 * DN_W]
    kn_t = kn.T

    ab = ab_ref[...]
    x = ab + dtb_ref[...]
    g = -jnp.exp(alog_ref[...]) * (jnp.maximum(x, 0.0) + jnp.log(1.0 + jnp.exp(-jnp.abs(x))))
    n_dh = 2 * DN_HEADS
    beta = _sigmoid(pltpu.roll(ab, LANES - n_dh, 1))
    r = lax.broadcasted_iota(jnp.int32, (ROW_TILE, ROW_TILE), 0)
    c = lax.broadcasted_iota(jnp.int32, (ROW_TILE, ROW_TILE), 1)
    same_chunk = (r >> 6) == (c >> 6)
    tri_f = jnp.where(jnp.logical_and(same_chunk, c <= r), 1.0, 0.0)
    tri_b = jnp.where(jnp.logical_and(same_chunk, c >= r), 1.0, 0.0)
    tri = jnp.concatenate([tri_f, tri_b], axis=0).astype(BF16)
    both, rest = None, g
    for _ in range(3):
        piece = rest.astype(BF16)
        rest = rest - piece.astype(F32)
        term = jnp.dot(tri, piece, preferred_element_type=F32)
        both = term if both is None else both + term
    lane = lax.broadcasted_iota(jnp.int32, g.shape, 1)
    gcum = jnp.where(lane < DN_HEADS, both[0:ROW_TILE, :], both[ROW_TILE:2 * ROW_TILE, :])
    gcum_t = gcum.T

    n_grp = 2 * DN_PAIRS
    sel_r = lax.broadcasted_iota(jnp.int32, (LANES, n_grp * LANES), 0)
    sel_c = lax.broadcasted_iota(jnp.int32, (LANES, n_grp * LANES), 1)
    grp = sel_c >> 7
    sel = jnp.where(sel_r == (grp >> 1) * DN_HEADS + (grp & 1) * 2 + ((sel_c >> 6) & 1), 1.0, 0.0).astype(BF16)

    def spread(x, pieces):
        out, rest = None, x
        for _ in range(pieces):
            part = rest.astype(BF16)
            rest = rest - part.astype(F32)
            term = jnp.dot(part, sel, preferred_element_type=F32)
            out = term if out is None else out + term
        return out

    g_lanes = spread(gcum, 3)
    b_lanes = spread(beta, 3)

    ri = lax.broadcasted_iota(jnp.int32, (DN_CHUNK, LANES), 0)
    ci = lax.broadcasted_iota(jnp.int32, (DN_CHUNK, LANES), 1) & (DN_DH - 1)
    left = lax.broadcasted_iota(jnp.int32, (DN_CHUNK, LANES), 1) < DN_DH
    eye2 = ri == ci
    same_sub = (ri >> 4) == (ci >> 4)
    n_chunks = ROW_TILE // DN_CHUNK

    def bdiag(x):
        return jnp.concatenate([jnp.where(left, x, 0.0), jnp.where(left, 0.0, x)], axis=-2)

    def pmm(a, b):
        return _bmm(a, bdiag(b))

    m_list, xu_list, xw_list = [], [], []
    for ch in range(n_chunks):
        rows = slice(ch * DN_CHUNK, (ch + 1) * DN_CHUNK)
        gram, qkm = [], []
        for p in range(DN_PAIRS):
            cols = slice(p * LANES, (p + 1) * LANES)
            kp = kn[rows, cols]
            kbd = bdiag(kp)
            gram.append(_dot_nt(kp, kbd))
            qkm.append(_dot_nt(qn[rows, cols], kbd))
        for grp_i in range(n_grp):
            d, p = grp_i // DN_PAIRS, grp_i % DN_PAIRS
            cols = slice(p * LANES, (p + 1) * LANES)
            gcols = slice(grp_i * LANES, (grp_i + 1) * LANES)
            incl = (ci <= ri) if d == 0 else (ci >= ri)
            strict = (ci < ri) if d == 0 else (ci > ri)
            last = DN_CHUNK - 1 if d == 0 else 0
            g_col = g_lanes[rows, gcols]
            b_col = b_lanes[rows, gcols]
            g_row = jnp.sum(jnp.where(eye2, g_col, 0.0), axis=0, keepdims=True)
            decay = jnp.where(incl, jnp.exp(jnp.minimum(g_col - g_row, 0.0)), 0.0)
            eg = jnp.exp(g_col)
            kp = kn[rows, cols]
            m_list.append(jnp.where(strict, gram[p] * decay, 0.0) * b_col)
            xu_list.append(vv[rows, cols] * b_col)
            xw_list.append(kp * (b_col * eg))
            qg_ref[grp_i, rows, :] = (qn[rows, cols] * eg).astype(qg_ref.dtype)
            qk_ref[grp_i, rows, :] = (qkm[p] * decay).astype(qk_ref.dtype)
            fac = []
            for hh in range(2):
                idx = d * DN_HEADS + 2 * p + hh
                gr = gcum_t[idx:idx + 1, rows]
                fac.append(jnp.broadcast_to(jnp.exp(gr[:, last:last + 1] - gr), (DN_DH, DN_CHUNK)))
            kt_ref[grp_i, ch] = (kn_t[cols, rows] * jnp.concatenate(fac, axis=0)).astype(kt_ref.dtype)
            gl_ref[grp_i, ch * HALO:(ch + 1) * HALO, :] = jnp.broadcast_to(eg[last:last + 1, :], (HALO, LANES))
    m = jnp.stack(m_list)
    n_diag = jnp.where(same_sub, m, 0.0)
    m_off = m - n_diag
    xp = -n_diag
    dinv = jnp.where(eye2, 1.0, 0.0) + xp
    for _ in range(3):
        xp = pmm(xp, xp)
        dinv = dinv + pmm(dinv, xp)
    f = pmm(dinv, m_off)
    f2 = pmm(f, f)

    def bdiag2(y):
        return jnp.concatenate([bdiag(y[..., 0:LANES]), bdiag(y[..., LANES:2 * LANES])], axis=-1)

    y = _bmm(dinv, jnp.concatenate([bdiag(jnp.stack(xu_list)), bdiag(jnp.stack(xw_list))], axis=-1))
    y = y - _bmm(f, bdiag2(y))
    sol = y + _bmm(f2, bdiag2(y))
    for ch in range(n_chunks):
        rows = slice(ch * DN_CHUNK, (ch + 1) * DN_CHUNK)
        for grp_i in range(n_grp):
            i = ch * n_grp + grp_i
            u_ref[grp_i, rows, :] = sol[i, :, 0:LANES]
            w_ref[grp_i, rows, :] = sol[i, :, LANES:2 * LANES].astype(w_ref.dtype)


def _dn_prep(dn, ab, conv_w, a_log, dt_bias, tiles_per_batch):
    n_batch, t, dn_cols = dn.shape
    n_dh = 2 * DN_HEADS
    n_grp = 2 * DN_PAIRS
    pad = jnp.zeros((1, LANES - n_dh), F32)
    alog_row = jnp.concatenate([a_log.reshape(1, n_dh).astype(F32), pad], axis=-1)
    dtb_row = jnp.concatenate([dt_bias.reshape(1, n_dh).astype(F32), pad], axis=-1)
    prev_spec, next_spec = _halo_specs(dn_cols, tiles_per_batch)
    tok = lambda b, j: (b, 0, j, 0)
    out_specs = [pl.BlockSpec((None, n_grp, ROW_TILE, LANES), tok)] * 4
    out_specs += [pl.BlockSpec((None, n_grp, ROW_TILE // DN_CHUNK, LANES, DN_CHUNK), lambda b, j: (b, 0, j, 0, 0)),
                  pl.BlockSpec((None, n_grp, HALO * ROW_TILE // DN_CHUNK, LANES), tok)]
    tok_shape = (n_batch, n_grp, t, LANES)
    out_shape = [jax.ShapeDtypeStruct(tok_shape, F32),
                 jax.ShapeDtypeStruct(tok_shape, BF16),
                 jax.ShapeDtypeStruct(tok_shape, BF16),
                 jax.ShapeDtypeStruct(tok_shape, BF16),
                 jax.ShapeDtypeStruct((n_batch, n_grp, t // DN_CHUNK, LANES, DN_CHUNK), BF16),
                 jax.ShapeDtypeStruct((n_batch, n_grp, HALO * t // DN_CHUNK, LANES), F32)]
    return pl.pallas_call(
        functools.partial(_dn_prep_kernel, tiles_per_batch=tiles_per_batch),
        grid=(n_batch, tiles_per_batch),
        in_specs=[pl.BlockSpec((None, ROW_TILE, dn_cols), lambda b, j: (b, j, 0)),
                  prev_spec, next_spec,
                  pl.BlockSpec((None, ROW_TILE, LANES), lambda b, j: (b, j, 0)),
                  _const_spec(conv_w.shape), _const_spec((1, LANES)), _const_spec((1, LANES))],
        out_specs=out_specs,
        out_shape=out_shape,
        compiler_params=_cparams(("parallel", "parallel")),
        name="dn_prep",
    )(dn, dn, dn, ab, conv_w, alog_row, dtb_row)


def _dn_scan_kernel(*refs):
    ins, (of_ref, ob_ref, s_ref) = refs[:12], refs[12:]

    @pl.when(pl.program_id(0) == 0)
    def _():
        s_ref[...] = jnp.zeros_like(s_ref)

    left = lax.broadcasted_iota(jnp.int32, (DN_DH, LANES), 1) < DN_DH

    def bdiag(x):
        return jnp.concatenate([jnp.where(left, x, 0.0), jnp.where(left, 0.0, x)], axis=-2)

    for d, o_ref in enumerate((of_ref, ob_ref)):
        u_ref, w_ref, qg_ref, qk_ref, kt_ref, gl_ref = ins[6 * d:6 * d + 6]
        flat = lambda r: r[...].reshape((-1,) + r.shape[2:])
        s = s_ref[d]
        s_bd = bdiag(s)
        both = _bmm(jnp.concatenate([flat(w_ref), flat(qg_ref)], axis=1), s_bd)
        v_new = flat(u_ref) - both[:, 0:DN_CHUNK, :]
        o = both[:, DN_CHUNK:2 * DN_CHUNK, :] + _bmm(flat(qk_ref), bdiag(v_new))
        o_ref[...] = o.reshape(o_ref.shape)
        full = _bmm(flat(kt_ref), v_new)
        s_ref[d] = s * flat(gl_ref)[:, 0:1, :] + jnp.where(left, full[:, 0:DN_DH, :], full[:, DN_DH:2 * DN_DH, :])


def _dn_scan(prep, n_ctx):
    u, w, qg, qk, kt, gl = prep
    n_batch, _, t, _ = u.shape
    n_steps = t // DN_CHUNK
    ctx_chunks = n_ctx // DN_CHUNK

    def chunk_of(d, i):
        if d == 0:
            return i
        return jnp.where(i < ctx_chunks, ctx_chunks - 1 - i, n_steps - 1 + ctx_chunks - i)

    in_specs = []
    for d in range(2):
        tok = functools.partial(lambda i, d: (0, d, chunk_of(d, i), 0), d=d)
        in_specs += [pl.BlockSpec((n_batch, DN_PAIRS, DN_CHUNK, LANES), tok)] * 4
        in_specs += [pl.BlockSpec((n_batch, DN_PAIRS, None, LANES, DN_CHUNK),
                                  functools.partial(lambda i, d: (0, d, chunk_of(d, i), 0, 0), d=d)),
                     pl.BlockSpec((n_batch, DN_PAIRS, HALO, LANES), tok)]
    out_specs = [pl.BlockSpec((n_batch, DN_PAIRS, DN_CHUNK, LANES),
                              functools.partial(lambda i, d: (0, 0, chunk_of(d, i), 0), d=d))
                 for d in range(2)]
    out_shape = [jax.ShapeDtypeStruct((n_batch, DN_PAIRS, t, LANES), F32)] * 2
    return pl.pallas_call(
        _dn_scan_kernel,
        grid=(n_steps,),
        in_specs=in_specs,
        out_specs=out_specs,
        out_shape=out_shape,
        scratch_shapes=[pltpu.VMEM((2, n_batch * DN_PAIRS, DN_DH, LANES), F32)],
        compiler_params=_cparams(("arbitrary",)),
        name="dn_scan",
    )(u, w, qg, qk, kt, gl, u, w, qg, qk, kt, gl)


def _merge_kernel(h_ref, mod_ref, att_ref, attc_ref, of_ref, ob_ref, z_ref, dng_ref,
                  pool_ref, pprev_ref, pnext_ref, pw_ref, ps_ref, gate_ref,
                  wba_ref, wbd_ref, wbp_ref, wo_ref, o_ref, *, tiles_per_batch, first_tile, seq, n_ctx):
    j = pl.program_id(1) + first_tile
    d = h_ref.shape[-1]
    o = jnp.concatenate([of_ref[p] + ob_ref[p] for p in range(DN_PAIRS)], axis=-1)
    lane_w = lax.broadcasted_iota(jnp.int32, (DN_W, DN_W), 1)
    row_w = lax.broadcasted_iota(jnp.int32, (DN_W, DN_W), 0)
    head_ones = jnp.where((lane_w >> 6) == (row_w >> 6), 1.0, 0.0).astype(BF16)
    sq = o * o
    sq_hi = sq.astype(BF16)
    sq_lo = (sq - sq_hi.astype(F32)).astype(BF16)
    ssq = (jnp.dot(sq_hi, head_ones, preferred_element_type=F32)
           + jnp.dot(sq_lo, head_ones, preferred_element_type=F32))
    dn = o * lax.rsqrt(ssq * (1.0 / DN_DH) + EPS) * dng_ref[...] * _silu(z_ref[...])
    cur = pool_ref[...]
    xe = _with_halo(pprev_ref, cur, pnext_ref, POOL_W, tiles_per_batch, j)
    length = jnp.where(j == 0, n_ctx, seq)
    t0 = jnp.where(j == 0, 0, (j - 1) * ROW_TILE)
    tpos = t0 + lax.broadcasted_iota(jnp.int32, (ROW_TILE, POOL_W), 0)
    lane = lax.broadcasted_iota(jnp.int32, (ROW_TILE, POOL_W), 1)
    n_ext = ROW_TILE + 2 * HALO
    sums = {1: xe}
    win = 1
    while win < max(POOL_WINDOWS):
        prev = sums[win]
        sums[2 * win] = prev + pltpu.roll(prev, n_ext - win, 0)
        win *= 2
    mean = jnp.zeros((ROW_TILE, POOL_W), F32)
    for gi, win in enumerate(POOL_WINDOWS):
        wsum = pltpu.roll(sums[win], win // 2, 0)[HALO:HALO + ROW_TILE, :]
        lo = jnp.clip(tpos - win // 2, 0, length)
        hi = jnp.clip(tpos - win // 2 + win, 0, length)
        mean = jnp.where((lane >> 6) == gi, wsum / (hi - lo).astype(F32), mean)
    pooled = _dot(mean - cur, pw_ref[...]) * ps_ref[...]
    ga = gate_ref[:, 0:d].astype(F32)
    gd = gate_ref[:, d:2 * d].astype(F32)
    gp = gate_ref[:, 2 * d:3 * d].astype(F32)
    att = jnp.concatenate([jnp.where(j == 0, attc_ref[hd], att_ref[hd])
                           for hd in range(ATT_HEADS)], axis=-1)
    mix = (ga * jnp.dot(att, wba_ref[...], preferred_element_type=F32)
           + gd * _dot(dn, wbd_ref[...]) + gp * _dot(pooled, wbp_ref[...]))
    y = _dot(mix, wo_ref[...])
    o_ref[...] = h_ref[...] + mod_ref[0:1, :] * y


def _merge(h, mod_gate, att, att_ctx, o_f, o_b, dn, dn_norm_g, pool, pool_w_bd, pool_scale, gates,
           wb_att, wb_dn, wb_pool, w_out, first_tile, n_ctx):
    n_batch, t, d = h.shape
    tpb = t // ROW_TILE
    n_rows = tpb - first_tile
    seq = t - n_ctx
    ft = first_tile
    prev_spec, next_spec = _halo_specs(POOL_W, tpb)
    shift = lambda f: (lambda b, j: f(b, j + ft))
    prev_spec = pl.BlockSpec(prev_spec.block_shape, shift(prev_spec.index_map))
    next_spec = pl.BlockSpec(next_spec.block_shape, shift(next_spec.index_map))
    row = lambda b, j: (b, j + ft, 0)

    def mod_idx(b, j):
        return (jnp.where(j + ft == 0, n_batch, b), 0, 0)

    weights = [wb_att, wb_dn, wb_pool, w_out]
    kern = functools.partial(_merge_kernel, tiles_per_batch=tpb, first_tile=ft, seq=seq, n_ctx=n_ctx)
    return pl.pallas_call(
        kern,
        grid=(n_batch, n_rows),
        in_specs=[pl.BlockSpec((None, ROW_TILE, d), row),
                  pl.BlockSpec((None, 1, d), mod_idx),
                  pl.BlockSpec((ATT_HEADS, None, ROW_TILE, ATT_VD),
                               lambda b, j: (0, b, jnp.maximum(j + ft - 1, 0), 0)),
                  pl.BlockSpec((ATT_HEADS, None, ROW_TILE, ATT_VD), lambda b, j: (0, b, 0, 0)),
                  pl.BlockSpec((None, DN_PAIRS, ROW_TILE, LANES), lambda b, j: (b, 0, j + ft, 0)),
                  pl.BlockSpec((None, DN_PAIRS, ROW_TILE, LANES), lambda b, j: (b, 0, j + ft, 0)),
                  pl.BlockSpec((None, ROW_TILE, DN_W), lambda b, j: (b, j + ft, 3)),
                  _const_spec((1, DN_W)),
                  pl.BlockSpec((None, ROW_TILE, POOL_W), row),
                  prev_spec, next_spec,
                  _const_spec(pool_w_bd.shape), _const_spec((1, POOL_W)),
                  pl.BlockSpec((None, ROW_TILE, 3 * d), row)]
                 + [_const_spec(w.shape) for w in weights],
        out_specs=pl.BlockSpec((None, ROW_TILE, d), lambda b, j: (b, j, 0)),
        out_shape=jax.ShapeDtypeStruct((n_batch, n_rows * ROW_TILE, d), F32),
        compiler_params=_cparams(("parallel", "parallel")),
        name="merge",
    )(h, mod_gate, att, att if att_ctx is None else att_ctx, o_f, o_b, dn,
      jnp.tile(dn_norm_g.reshape(1, DN_DH), (1, DN_HEADS)),
      pool, pool, pool,
      pool_w_bd, pool_scale.reshape(1, POOL_W), gates, *weights)


def _ffn_kernel(*refs, n_experts, n_fchunks, final_norm):
    if n_experts > 1:
        h_ref, mod_ref, g_ref, fg_ref, rw_ref, w1_ref, w3_ref, w2_ref, o_ref, u_ref, acc_ref, comb_ref = refs
    else:
        h_ref, mod_ref, g_ref, fg_ref, w1_ref, w3_ref, w2_ref, o_ref, u_ref, acc_ref = refs
    step = pl.program_id(2)
    e = step // n_fchunks

    @pl.when(step == 0)
    def _():
        u = _norm_mod(h_ref[...], g_ref[...], mod_ref[0:1, :], mod_ref[1:2, :])
        u_ref[...] = u.astype(BF16)
        acc_ref[...] = jnp.zeros_like(acc_ref)
        if n_experts > 1:
            logits = _dot(u, rw_ref[...])
            lane = lax.broadcasted_iota(jnp.int32, logits.shape, 1).astype(F32)
            neg = jnp.float32(-jnp.inf)
            lg = jnp.where(lane < n_experts, logits, neg)
            v1 = jnp.max(lg, axis=-1, keepdims=True)
            i1 = jnp.min(jnp.where(lg == v1, lane, float(LANES)), axis=-1, keepdims=True)
            lg2 = jnp.where(lane == i1, neg, lg)
            v2 = jnp.max(lg2, axis=-1, keepdims=True)
            i2 = jnp.min(jnp.where(lg2 == v2, lane, float(LANES)), axis=-1, keepdims=True)
            e2 = jnp.exp(v2 - v1)
            den = 1.0 + e2
            comb_ref[...] = jnp.where(lane == i1, 1.0 / den, 0.0) + jnp.where(lane == i2, e2 / den, 0.0)

    u = u_ref[...]
    hid = _silu(jnp.dot(u, w1_ref[...], preferred_element_type=F32)) * jnp.dot(
        u, w3_ref[...], preferred_element_type=F32)
    y = _dot(hid, w2_ref[...])
    if n_experts > 1:
        lane = lax.broadcasted_iota(jnp.int32, comb_ref.shape, 1)
        y = y * jnp.sum(jnp.where(lane == e, comb_ref[...], 0.0), axis=-1, keepdims=True)
    acc_ref[...] += y

    @pl.when(step == n_experts * n_fchunks - 1)
    def _():
        out = h_ref[...] + mod_ref[2:3, :] * acc_ref[...]
        if final_norm:
            out = out * lax.rsqrt(jnp.mean(out * out, axis=-1, keepdims=True) + EPS) * fg_ref[...]
        o_ref[...] = out


def _ffn(h, mod, norm_g, final_g, router_w, w1, w3, w2, has_ctx, final_norm):
    n_batch, t, d = h.shape
    n_exp, _, f = w1.shape
    n_rows = t // ROW_TILE
    row = lambda b, j, e: (b, j, 0)

    def mod_idx(b, j, e):
        return (jnp.where(jnp.logical_and(has_ctx, j == 0), n_batch, b), 0, 0)

    in_specs = [pl.BlockSpec((None, ROW_TILE, d), row),
                pl.BlockSpec((None, 3, d), mod_idx),
                _const_spec((1, d)), _const_spec((1, d))]
    args = [h, mod, norm_g.reshape(1, d), final_g.reshape(1, d)]
    scratch = [pltpu.VMEM((ROW_TILE, d), BF16), pltpu.VMEM((ROW_TILE, d), F32)]
    if n_exp > 1:
        in_specs.append(_const_spec(router_w.shape))
        args.append(router_w)
        scratch.append(pltpu.VMEM((ROW_TILE, LANES), F32))
    nfc = 1 if n_exp == 1 else 2
    fc = f // nfc
    assert fc * nfc == f and fc % LANES == 0
    in_specs += [pl.BlockSpec((None, d, fc), lambda b, j, s: (s // nfc, 0, s % nfc)),
                 pl.BlockSpec((None, d, fc), lambda b, j, s: (s // nfc, 0, s % nfc)),
                 pl.BlockSpec((None, fc, d), lambda b, j, s: (s // nfc, s % nfc, 0))]
    args += [w1, w3, w2]
    return pl.pallas_call(
        functools.partial(_ffn_kernel, n_experts=n_exp, n_fchunks=nfc, final_norm=final_norm),
        grid=(n_batch, n_rows, n_exp * nfc),
        in_specs=in_specs,
        out_specs=pl.BlockSpec((None, ROW_TILE, d), row),
        out_shape=jax.ShapeDtypeStruct((n_batch, t, d), F32),
        scratch_shapes=scratch,
        input_output_aliases={0: 0},
        compiler_params=_cparams(("parallel", "parallel", "arbitrary")),
        name="ffn_all_experts" if n_exp > 1 else "ffn_dense",
    )(*args)


MOE_BLOCK = 2048
MOE_SUB = 512
MOE_TILE = 288
MOE_ROW_ALIGN = 32
MOE_ROWS = -(-MOE_BLOCK // MOE_TILE) * MOE_TILE
MOE_PUT_TILE = 256
MOE_FCHUNKS = 4


def _top2_routing(logits, n_experts):
    lane = lax.broadcasted_iota(jnp.int32, logits.shape, 1).astype(F32)
    neg = jnp.float32(-jnp.inf)
    lg = jnp.where(lane < n_experts, logits, neg)
    v1 = jnp.max(lg, axis=-1, keepdims=True)
    i1 = jnp.min(jnp.where(lg == v1, lane, float(LANES)), axis=-1, keepdims=True)
    lg2 = jnp.where(lane == i1, neg, lg)
    v2 = jnp.max(lg2, axis=-1, keepdims=True)
    i2 = jnp.min(jnp.where(lg2 == v2, lane, float(LANES)), axis=-1, keepdims=True)
    e2 = jnp.exp(v2 - v1)
    den = 1.0 + e2
    first = lane == i1
    second = lane == i2
    sel = jnp.where(jnp.logical_or(first, second), 1.0, 0.0)
    comb = jnp.where(first, 1.0 / den, 0.0) + jnp.where(second, e2 / den, 0.0)
    return sel, comb


def _moe_kernel(h_ref, mod_ref, g_ref, fg_ref, rw_ref, w1_ref, w3_ref, w2_ref, o_ref,
                u_ref, x_ref, y_ref, rank_ref, rank_t_ref, comb_ref, cnt_ref, span_ref, *, n_experts, final_norm):
    e = pl.program_id(2)
    f = pl.program_id(3)
    n_sub = MOE_BLOCK // MOE_SUB

    @pl.when(jnp.logical_and(e == 0, f == 0))
    def _route():
        r = lax.broadcasted_iota(jnp.int32, (ROW_TILE, ROW_TILE), 0)
        c = lax.broadcasted_iota(jnp.int32, (ROW_TILE, ROW_TILE), 1)
        before = jnp.where(c < r, 1.0, 0.0).astype(BF16)
        count = jnp.zeros((1, LANES), F32)
        for t in range(MOE_BLOCK // ROW_TILE):
            rows = slice(t * ROW_TILE, (t + 1) * ROW_TILE)
            u = _norm_mod(h_ref[rows, :], g_ref[...], mod_ref[0:1, :], mod_ref[1:2, :])
            u_ref[rows, :] = u.astype(BF16)
            sel, comb = _top2_routing(_dot(u, rw_ref[...]), n_experts)
            rank = jnp.dot(before, sel.astype(BF16), preferred_element_type=F32) + count
            rank_ref[rows, :] = jnp.where(sel > 0.0, rank, -1.0)
            comb_ref[rows, :] = comb
            count = count + jnp.sum(sel, axis=0, keepdims=True)
        lane = lax.broadcasted_iota(jnp.int32, (1, LANES), 1)
        for ee in range(n_experts):
            cnt_ref[ee] = jnp.sum(jnp.where(lane == ee, count, 0.0)).astype(jnp.int32)
        rank_t_ref[...] = rank_ref[...].T
        o_ref[...] = jnp.zeros_like(o_ref)
        for s in range(n_sub):
            rk = rank_ref[s * MOE_SUB:(s + 1) * MOE_SUB, :]
            lo_v = jnp.min(jnp.where(rk >= 0.0, rk, float(MOE_BLOCK)), axis=0, keepdims=True)
            hi_v = jnp.max(rk, axis=0, keepdims=True)
            for ee in range(n_experts):
                span_ref[2 * (s * n_experts + ee)] = jnp.sum(jnp.where(lane == ee, lo_v, 0.0)).astype(jnp.int32)
                span_ref[2 * (s * n_experts + ee) + 1] = jnp.sum(jnp.where(lane == ee, hi_v, 0.0)).astype(jnp.int32)

        @pl.when(jnp.logical_and(pl.program_id(0) == 0, pl.program_id(1) == 0))
        def _():
            y_ref[...] = jnp.zeros_like(y_ref)

    n_rows = cnt_ref[e]
    n_tiles = (n_rows + MOE_TILE - 1) // MOE_TILE
    rank_row = rank_t_ref[pl.ds(e, 1), :]

    def sub_hits(row0, s, tile=MOE_TILE):
        rk = rank_row[:, s * MOE_SUB:(s + 1) * MOE_SUB]
        lo = span_ref[2 * (s * n_experts + e)]
        hi = span_ref[2 * (s * n_experts + e) + 1]
        return rk, jnp.logical_and(hi >= row0, lo < row0 + tile)

    @pl.when(f == 0)
    def _compact():
        def body(i, carry):
            row0 = pl.multiple_of(i * MOE_TILE, MOE_ROW_ALIGN)
            ids = (row0 + lax.broadcasted_iota(jnp.int32, (MOE_TILE, 1), 0)).astype(F32)
            x_ref[pl.ds(row0, MOE_TILE), :] = jnp.zeros((MOE_TILE, x_ref.shape[1]), BF16)
            for s in range(n_sub):
                rk, hit = sub_hits(row0, s)

                @pl.when(hit)
                def _():
                    pick = jnp.where(rk == ids, 1.0, 0.0).astype(BF16)
                    got = jnp.dot(pick, u_ref[s * MOE_SUB:(s + 1) * MOE_SUB, :], preferred_element_type=F32)
                    x_ref[pl.ds(row0, MOE_TILE), :] += got.astype(BF16)
            return carry

        lax.fori_loop(0, n_tiles, body, 0)

    def expert_body(i, carry):
        row0 = pl.multiple_of(i * MOE_TILE, MOE_ROW_ALIGN)
        x = x_ref[pl.ds(row0, MOE_TILE), :]
        hid = _silu(jnp.dot(x, w1_ref[...], preferred_element_type=F32)) * jnp.dot(
            x, w3_ref[...], preferred_element_type=F32)
        y = _dot(hid, w2_ref[...])

        @pl.when(f == 0)
        def _():
            y_ref[pl.ds(row0, MOE_TILE), :] = y

        @pl.when(f > 0)
        def _():
            y_ref[pl.ds(row0, MOE_TILE), :] += y

        return carry

    lax.fori_loop(0, n_tiles, expert_body, 0)

    @pl.when(f == MOE_FCHUNKS - 1)
    def _expand():
        lanes = lax.broadcasted_iota(jnp.int32, (MOE_SUB, LANES), 1)

        def body(i, carry):
            row0 = pl.multiple_of(i * MOE_PUT_TILE, MOE_PUT_TILE)
            yt = y_ref[pl.ds(row0, MOE_PUT_TILE), :].astype(BF16)
            ids = (row0 + lax.broadcasted_iota(jnp.int32, (1, MOE_PUT_TILE), 1)).astype(F32)
            for s in range(n_sub):
                _, hit = sub_hits(row0, s, MOE_PUT_TILE)

                @pl.when(hit)
                def _():
                    rows = slice(s * MOE_SUB, (s + 1) * MOE_SUB)
                    rk_col = jnp.sum(jnp.where(lanes == e, rank_ref[rows, :], 0.0), axis=1, keepdims=True)
                    w_col = jnp.sum(jnp.where(lanes == e, comb_ref[rows, :], 0.0), axis=1, keepdims=True)
                    put = jnp.where(rk_col == ids, 1.0, 0.0).astype(BF16)
                    o_ref[rows, :] += w_col * jnp.dot(put, yt, preferred_element_type=F32)
            return carry

        lax.fori_loop(0, (n_rows + MOE_PUT_TILE - 1) // MOE_PUT_TILE, body, 0)

    @pl.when(jnp.logical_and(e == n_experts - 1, f == MOE_FCHUNKS - 1))
    def _finish():
        for t in range(MOE_BLOCK // ROW_TILE):
            rows = slice(t * ROW_TILE, (t + 1) * ROW_TILE)
            out = h_ref[rows, :] + mod_ref[2:3, :] * o_ref[rows, :]
            if final_norm:
                out = out * lax.rsqrt(jnp.mean(out * out, axis=-1, keepdims=True) + EPS) * fg_ref[...]
            o_ref[rows, :] = out


def _moe(h, mod, norm_g, final_g, router_w, w1, w3, w2, final_norm):
    n_batch, s, d = h.shape
    n_exp, _, f = w1.shape
    assert s % MOE_BLOCK == 0 and f % (MOE_FCHUNKS * LANES) == 0 and n_exp <= 8
    fc = f // MOE_FCHUNKS
    blk = lambda b, j, e, c: (b, j, 0)
    single = pl.Buffered(1)

    return pl.pallas_call(
        functools.partial(_moe_kernel, n_experts=n_exp, final_norm=final_norm),
        grid=(n_batch, s // MOE_BLOCK, n_exp, MOE_FCHUNKS),
        in_specs=[pl.BlockSpec((None, MOE_BLOCK, d), blk, pipeline_mode=single),
                  pl.BlockSpec((None, 3, d), lambda b, j, e, c: (b, 0, 0)),
                  _const_spec((1, d)), _const_spec((1, d)), _const_spec(router_w.shape),
                  pl.BlockSpec((None, d, fc), lambda b, j, e, c: (e, 0, c)),
                  pl.BlockSpec((None, d, fc), lambda b, j, e, c: (e, 0, c)),
                  pl.BlockSpec((None, fc, d), lambda b, j, e, c: (e, c, 0))],
        out_specs=pl.BlockSpec((None, MOE_BLOCK, d), blk, pipeline_mode=single),
        out_shape=jax.ShapeDtypeStruct((n_batch, s, d), F32),
        scratch_shapes=[pltpu.VMEM((MOE_BLOCK, d), BF16),
                        pltpu.VMEM((MOE_ROWS, d), BF16),
                        pltpu.VMEM((MOE_ROWS, d), F32),
                        pltpu.VMEM((MOE_BLOCK, LANES), F32),
                        pltpu.VMEM((LANES, MOE_BLOCK), F32),
                        pltpu.VMEM((MOE_BLOCK, LANES), F32),
                        pltpu.SMEM((HALO,), jnp.int32),
                        pltpu.SMEM((2 * (MOE_BLOCK // MOE_SUB) * HALO,), jnp.int32)],
        compiler_params=_cparams(("arbitrary", "arbitrary", "arbitrary", "arbitrary")),
        name="moe_top2",
    )(h, mod, norm_g.reshape(1, d), final_g.reshape(1, d), router_w, w1, w3, w2)


def _block_diag(w):
    g, n, _ = w.shape
    out = jnp.zeros((g * n, g * n), w.dtype)
    for i in range(g):
        out = out.at[i * n:(i + 1) * n, i * n:(i + 1) * n].set(w[i])
    return out


def kernel(x, c, ctx, c_ctx, ada_w, ada_b, norm1_g, norm2_g, w_in, attn_lambda, attn_subln_g, dn_conv_w,
           dn_a_log, dn_dt_bias, dn_norm_g, pool_w, pool_scale, w_branch, w_out, ffn_w1, ffn_w3, ffn_w2,
           router_w, moe_w1, moe_w3, moe_w2, final_norm_g):
    n_batch, seq, d = x.shape
    n_ctx = ctx.shape[1]
    depth = ada_w.shape[0]
    assert n_ctx == ROW_TILE and seq % ROW_TILE == 0 and seq % GRID_W == 0 and d == 1024
    t = n_ctx + seq
    tpb = t // ROW_TILE

    h = None
    c_rows = jnp.concatenate([c, c_ctx[None, :], jnp.zeros((8 - n_batch - 1, d), F32)], axis=0)
    mods = _ada_mod(c_rows, ada_w, ada_b).reshape(depth, 8, 6, d)
    rope_tabs = _rope_tables(n_ctx, seq)

    sizes = (ATT_W, ATT_W, ATT_W, DN_W, DN_W, DN_W, DN_W, 2 * DN_HEADS, 2 * DN_HEADS, POOL_W, 3 * d)
    offs = np.concatenate([[0], np.cumsum(sizes)]).tolist()

    for l in range(depth):
        last = l == depth - 1
        first_tile = 1 if last else 0
        lambda_init = 0.8 - 0.6 * math.exp(-0.3 * l)
        wl = w_in[l].astype(BF16)
        seg = lambda a, b: wl[:, offs[a]:offs[b]]
        w_ab = jnp.concatenate([seg(7, 9), jnp.zeros((d, LANES - 4 * DN_HEADS), BF16)], axis=1)
        weights = [seg(0, 1), seg(1, 2), seg(2, 3), seg(3, 7), w_ab, seg(9, 10), seg(10, 11)]
        mod = mods[l, :n_batch + 1]
        src = (ctx, x) if l == 0 else h.reshape(n_batch * t, d)
        outs = _in_projection(src, mod[:, 0:2], norm1_g[l], rope_tabs, weights, n_batch, tpb)
        q, k, v, dn, ab, pool, gates = outs[:7]
        if l == 0:
            h = outs[7].reshape(n_batch, t, d)
        q, k, v = (a.reshape(ATT_HEADS, n_batch, t, ATT_VD) for a in (q, k, v))
        att, att_ctx = _diff_attention(q, k, v, attn_lambda[l], attn_subln_g[l], lambda_init, n_ctx, not last)
        dn = dn.reshape(n_batch, t, 4 * DN_W)
        prep = _dn_prep(dn, ab.reshape(n_batch, t, LANES), dn_conv_w[l], dn_a_log[l], dn_dt_bias[l], tpb)
        o_f, o_b = _dn_scan(prep, n_ctx)
        wb = w_branch[l].astype(BF16)
        h = _merge(h, mod[:, 2:3], att, att_ctx, o_f, o_b, dn, dn_norm_g[l], pool.reshape(n_batch, t, POOL_W),
                   _block_diag(pool_w[l]).astype(BF16), pool_scale[l], gates.reshape(n_batch, t, 3 * d),
                   wb[:ATT_W], wb[ATT_W:ATT_W + DN_W], wb[ATT_W + DN_W:], w_out[l].astype(BF16),
                   first_tile, n_ctx)
        i = l // 2
        if l % 2 == 0:
            h = _ffn(h, mod[:, 3:6], norm2_g[l], final_norm_g, None, ffn_w1[i][None].astype(BF16),
                     ffn_w3[i][None].astype(BF16), ffn_w2[i][None].astype(BF16), not last, last)
        else:
            rw = jnp.concatenate([router_w[i], jnp.zeros((d, LANES - N_EXPERTS), F32)], axis=1).astype(BF16)
            w1, w3, w2 = moe_w1[i].astype(BF16), moe_w3[i].astype(BF16), moe_w2[i].astype(BF16)
            if last:
                h = _moe(h, mod[:n_batch, 3:6], norm2_g[l], final_norm_g, rw, w1, w3, w2, True)
            else:
                h = _ffn(h, mod[:, 3:6], norm2_g[l], final_norm_g, rw, w1, w3, w2, True, False)
    return h
```

```python
import functools
import math

import numpy as np
import jax
import jax.numpy as jnp
from jax import lax
from jax.experimental import pallas as pl
from jax.experimental.pallas import tpu as pltpu

F32 = jnp.float32
BF16 = jnp.bfloat16

EPS = 1e-6
GRID_W = 64
ROPE_BASE = 10000.0
ATT_HEADS = 4
ATT_DH = 64
ATT_VD = 2 * ATT_DH
ATT_W = ATT_HEADS * ATT_VD
DN_HEADS = 4
DN_DH = 64
DN_W = DN_HEADS * DN_DH
DN_PAIRS = DN_HEADS // 2
DN_CONV = 5
DN_CHUNK = 64
DN_SUB = 16
POOL_WINDOWS = (2, 4, 8, 16)
POOL_GD = 64
POOL_W = len(POOL_WINDOWS) * POOL_GD
N_EXPERTS = 8

ROW_TILE = 256
HALO = 8
LANES = 128
VMEM_LIMIT = 56 * 1024 * 1024
DH_SHIFT = DN_DH.bit_length() - 1
CHUNK_SHIFT = DN_CHUNK.bit_length() - 1
SUB_SHIFT = DN_SUB.bit_length() - 1
LANE_SHIFT = LANES.bit_length() - 1
assert POOL_GD == DN_DH == 1 << DH_SHIFT and DN_CHUNK == 1 << CHUNK_SHIFT and DN_SUB == 1 << SUB_SHIFT


def _cparams(sem):
    return pltpu.CompilerParams(dimension_semantics=sem, vmem_limit_bytes=VMEM_LIMIT)


def _dot(a, b):
    return jnp.dot(a.astype(BF16), b.astype(BF16), preferred_element_type=F32)


def _dot_nt(a, b):
    return lax.dot_general(a.astype(BF16), b.astype(BF16), (((1,), (1,)), ((), ())),
                           preferred_element_type=F32)


def _bmm(a, b):
    return lax.dot_general(a.astype(BF16), b.astype(BF16), (((2,), (1,)), ((0,), (0,))),
                           preferred_element_type=F32)


def _sigmoid(x):
    return 1.0 / (1.0 + jnp.exp(-x))


def _silu(x):
    return x * _sigmoid(x)


def _const_spec(shape):
    nd = len(shape)
    return pl.BlockSpec(shape, lambda *_: (0,) * nd)


def _ada_kernel(c_ref, w_ref, b_ref, o_ref):
    o_ref[...] = _dot(_silu(c_ref[...]), w_ref[...]) + b_ref[...]


def _ada_mod(c_rows, ada_w, ada_b):
    depth, d, six_d = ada_w.shape
    n = six_d // d
    rows = c_rows.shape[0]
    return pl.pallas_call(
        _ada_kernel,
        grid=(depth, n),
        in_specs=[pl.BlockSpec((rows, d), lambda l, j: (0, 0)),
                  pl.BlockSpec((None, d, d), lambda l, j: (l, 0, j)),
                  pl.BlockSpec((None, 1, d), lambda l, j: (l, 0, j))],
        out_specs=pl.BlockSpec((None, rows, d), lambda l, j: (l, 0, j)),
        out_shape=jax.ShapeDtypeStruct((depth, rows, six_d), F32),
        compiler_params=_cparams(("parallel", "parallel")),
        name="ada_mod",
    )(c_rows, ada_w, ada_b.reshape(depth, 1, six_d))


def _norm_mod(x, g, shift, scale):
    y = x * lax.rsqrt(jnp.mean(x * x, axis=-1, keepdims=True) + EPS) * g
    return y * (1.0 + scale) + shift


def _inproj_kernel(*refs, tiles_per_batch, split_in):
    if split_in:
        ctx_ref, x_ref = refs[:2]
        (mod_ref, g_ref, cos_ref, su_ref, sd_ref, wq_ref, wk_ref, wv_ref, wdn_ref, wab_ref, wpool_ref, wg_ref,
         q_ref, k_ref, v_ref, dn_ref, ab_ref, pool_ref, gate_ref, h_out_ref) = refs[2:]
        h = jnp.where(pl.program_id(0) % tiles_per_batch == 0, ctx_ref[...], x_ref[...])
        h_out_ref[...] = h
    else:
        (h_ref, mod_ref, g_ref, cos_ref, su_ref, sd_ref, wq_ref, wk_ref, wv_ref, wdn_ref, wab_ref, wpool_ref,
         wg_ref, q_ref, k_ref, v_ref, dn_ref, ab_ref, pool_ref, gate_ref) = refs
        h = h_ref[...]
    u = _norm_mod(h, g_ref[...], mod_ref[0:1, :], mod_ref[1:2, :]).astype(BF16)
    reps = ATT_W // LANES
    cos = jnp.tile(cos_ref[...], (1, reps))
    s_up = jnp.tile(su_ref[...], (1, reps))
    s_dn = jnp.tile(sd_ref[...], (1, reps))
    quarter = ATT_DH // 4

    def rope(t):
        return (t * cos + pltpu.roll(t, ATT_W - quarter, 1) * s_up + pltpu.roll(t, quarter, 1) * s_dn)

    q = rope(jnp.dot(u, wq_ref[...], preferred_element_type=F32))
    k = rope(jnp.dot(u, wk_ref[...], preferred_element_type=F32))
    v = jnp.dot(u, wv_ref[...], preferred_element_type=F32)
    for hd in range(ATT_HEADS):
        cols = slice(hd * ATT_VD, (hd + 1) * ATT_VD)
        q_ref[hd] = (q[:, cols] * (ATT_DH ** -0.5 * LOG2E)).astype(q_ref.dtype)
        k_ref[hd] = k[:, cols].astype(k_ref.dtype)
        v_ref[hd] = v[:, cols].astype(v_ref.dtype)
    dn_ref[...] = jnp.dot(u, wdn_ref[...], preferred_element_type=F32)
    ab_ref[...] = jnp.dot(u, wab_ref[...], preferred_element_type=F32)
    pool_ref[...] = jnp.dot(u, wpool_ref[...], preferred_element_type=F32)
    gate_ref[...] = _sigmoid(jnp.dot(u, wg_ref[...], preferred_element_type=F32)).astype(gate_ref.dtype)


def _in_projection(h, mod, norm_g, rope_tabs, weights, n_batch, tiles_per_batch):
    split_in = isinstance(h, tuple)
    tpb = tiles_per_batch
    d = mod.shape[-1]
    n_tiles = n_batch * tpb
    rows = n_tiles * ROW_TILE
    widths = [w.shape[1] for w in weights]
    out_dtypes = [BF16, BF16, BF16, F32, F32, F32, BF16]

    def mod_idx(i):
        return (jnp.where(i % tpb == 0, n_batch, i // tpb), 0, 0)

    if split_in:
        h_args = list(h)
        in_specs = [pl.BlockSpec((None, ROW_TILE, d), lambda i: (i // tpb, 0, 0)),
                    pl.BlockSpec((None, ROW_TILE, d), lambda i: (i // tpb, jnp.maximum(i % tpb - 1, 0), 0))]
        widths.append(d)
        out_dtypes.append(F32)
    else:
        h_args = [h]
        in_specs = [pl.BlockSpec((ROW_TILE, d), lambda i: (i, 0))]
    in_specs += [pl.BlockSpec((None, 2, d), mod_idx), _const_spec((1, d))]
    in_specs += [pl.BlockSpec((ROW_TILE, LANES), lambda i: (i % tpb, 0)) for _ in range(3)]
    in_specs += [pl.BlockSpec(w.shape, lambda i: (0, 0), pipeline_mode=pl.Buffered(1)) for w in weights]
    out_specs = [pl.BlockSpec((ROW_TILE, w), lambda i: (i, 0)) for w in widths]
    out_shape = [jax.ShapeDtypeStruct((rows, w), dt) for w, dt in zip(widths, out_dtypes)]
    for o in range(3):
        out_specs[o] = pl.BlockSpec((ATT_HEADS, ROW_TILE, ATT_VD), lambda i: (0, i, 0))
        out_shape[o] = jax.ShapeDtypeStruct((ATT_HEADS, rows, ATT_VD), BF16)
    return pl.pallas_call(
        functools.partial(_inproj_kernel, tiles_per_batch=tpb, split_in=split_in),
        grid=(n_tiles,),
        in_specs=in_specs,
        out_specs=out_specs,
        out_shape=out_shape,
        compiler_params=_cparams(("parallel",)),
        name="in_projection",
    )(*h_args, mod, norm_g.reshape(1, d), *rope_tabs, *weights)


def _rope_tables(n_ctx, seq):
    rows = seq // GRID_W
    row = jnp.repeat(jnp.arange(rows), GRID_W).astype(F32)
    col = jnp.tile(jnp.arange(GRID_W), rows).astype(F32)
    n_freq = ATT_DH // 4
    inv = ROPE_BASE ** (-jnp.arange(n_freq, dtype=F32) / n_freq)
    ar = row[:, None] * inv
    ac = col[:, None] * inv
    cr, sr, cc, sc = jnp.cos(ar), jnp.sin(ar), jnp.cos(ac), jnp.sin(ac)
    z = jnp.zeros_like(sr)
    cos = jnp.concatenate([cr, cr, cc, cc], axis=-1)
    s_up = jnp.concatenate([-sr, z, -sc, z], axis=-1)
    s_dn = jnp.concatenate([z, sr, z, sc], axis=-1)

    def full(t, ctx_val):
        t = jnp.tile(t, (1, LANES // ATT_DH))
        return jnp.concatenate([jnp.full((n_ctx, LANES), ctx_val, F32), t], axis=0)

    return full(cos, 1.0), full(s_up, 0.0), full(s_dn, 0.0)


ATT_KEY_TILE = 1024
LOG2E = math.log2(math.e)


ATT_STREAMS = 2


class _AttnStream:
    def __init__(self, q, e_ref, k_ref, v_ref):
        lane = lax.broadcasted_iota(jnp.int32, q.shape, 1)
        zero = jnp.zeros_like(q)
        self.qm = (jnp.where(lane < ATT_DH, q, zero), jnp.where(lane >= ATT_DH, q, zero))
        self.e_ref, self.k_ref, self.v_ref = e_ref, k_ref, v_ref
        tq = q.shape[0]
        self.m_part = [jnp.full((tq, LANES), -jnp.inf, F32)] * 2
        self.l_part = [jnp.zeros((tq, LANES), F32)] * 2
        self.acc = jnp.zeros((tq, ATT_VD), F32)

    @staticmethod
    def _lane_fold(x, op):
        out = x[:, 0:LANES]
        for c in range(1, x.shape[1] // LANES):
            out = op(out, x[:, c * LANES:(c + 1) * LANES])
        return out

    def scores(self, k0, w):
        kt = self.k_ref[k0:k0 + w, :]
        for m in range(2):
            s = _dot_nt(self.qm[m], kt)
            self.e_ref[m, :, k0:k0 + w] = s
            self.m_part[m] = jnp.maximum(self.m_part[m], self._lane_fold(s, jnp.maximum))

    def end_scores(self):
        self.mx = [jnp.max(p, axis=-1, keepdims=True) for p in self.m_part]

    def exps(self, k0, w):
        for m in range(2):
            e = jnp.exp2(self.e_ref[m, :, k0:k0 + w] - self.mx[m])
            self.e_ref[m, :, k0:k0 + w] = e
            self.l_part[m] = self.l_part[m] + self._lane_fold(e, jnp.add)

    def end_exps(self, lam):
        self.inv0 = 1.0 / jnp.sum(self.l_part[0], axis=-1, keepdims=True)
        self.inv1 = lam / jnp.sum(self.l_part[1], axis=-1, keepdims=True)

    def values(self, k0, w):
        a = self.e_ref[0, :, k0:k0 + w] * self.inv0 - self.e_ref[1, :, k0:k0 + w] * self.inv1
        self.acc = self.acc + jnp.dot(a.astype(BF16), self.v_ref[k0:k0 + w, :], preferred_element_type=F32)

    def result(self, g, lambda_init):
        y = self.acc * lax.rsqrt(jnp.mean(self.acc * self.acc, axis=-1, keepdims=True) + EPS) * g
        return y * (1.0 - lambda_init)


def _attn_run(streams, tiles, lam, g, lambda_init, outs):
    passes = ("scores", "exps", "values")
    for step in range(len(passes) + len(streams) - 1):
        active = [(st, passes[step - i]) for i, st in enumerate(streams) if 0 <= step - i < len(passes)]
        for k0, w in tiles:
            for st, name in active:
                getattr(st, name)(k0, w)
        for st, name in active:
            if name == "scores":
                st.end_scores()
            elif name == "exps":
                st.end_exps(lam)
    for st, (o_ref, rows) in zip(streams, outs):
        o_ref[rows, :] = st.result(g, lambda_init).astype(o_ref.dtype)


def _attn_kernel(*refs, lambda_init, n_ctx, ctx_out):
    if ctx_out:
        lam_ref, g_ref, q_ref, qc_ref, k_ref, v_ref, o_ref, oc_ref, e_ref = refs
    else:
        lam_ref, g_ref, q_ref, k_ref, v_ref, o_ref, e_ref = refs
    n_keys = k_ref.shape[0]
    ctx_tiles = ((0, n_ctx),)
    all_tiles = ctx_tiles + tuple((k0, ATT_KEY_TILE) for k0 in range(n_ctx, n_keys, ATT_KEY_TILE))
    lp = lam_ref[...]
    lam = (jnp.exp(jnp.sum(lp[0:1] * lp[1:2], keepdims=True))
           - jnp.exp(jnp.sum(lp[2:3] * lp[3:4], keepdims=True)) + lambda_init)
    g = g_ref[...]

    if ctx_out:
        @pl.when(pl.program_id(2) == 0)
        def _():
            st = _AttnStream(qc_ref[...], e_ref.at[0], k_ref, v_ref)
            _attn_run([st], ctx_tiles, lam, g, lambda_init, [(oc_ref, slice(None))])

    streams, outs = [], []
    for i in range(ATT_STREAMS):
        rows = slice(i * ROW_TILE, (i + 1) * ROW_TILE)
        streams.append(_AttnStream(q_ref[0, 0, rows, :], e_ref.at[i], k_ref, v_ref))
        outs.append((o_ref, rows))
    _attn_run(streams, all_tiles, lam, g, lambda_init, outs)


def _diff_attention(q, k, v, lam_params, subln_g, lambda_init, n_ctx, ctx_out):
    _, n_batch, t, _ = q.shape
    seq = t - n_ctx
    tq = ATT_STREAMS * ROW_TILE
    assert seq % ATT_KEY_TILE == 0 and seq % tq == 0
    kv_spec = pl.BlockSpec((None, None, t, ATT_VD), lambda b, h, j: (h, b, 0, 0))
    ctx_spec = pl.BlockSpec((None, None, n_ctx, ATT_VD), lambda b, h, j: (h, b, 0, 0))
    in_specs = [_const_spec(lam_params.shape), _const_spec((1, ATT_VD)),
                pl.BlockSpec((pl.Element(1), pl.Element(1), pl.Element(tq), pl.Element(ATT_VD)),
                             lambda b, h, j: (h, b, pl.multiple_of(n_ctx + j * tq, ROW_TILE), 0))]
    out_specs = [pl.BlockSpec((None, None, tq, ATT_VD), lambda b, h, j: (h, b, j, 0))]
    out_shape = [jax.ShapeDtypeStruct((ATT_HEADS, n_batch, seq, ATT_VD), BF16)]
    args = [lam_params, subln_g.reshape(1, ATT_VD), q]
    if ctx_out:
        in_specs.append(ctx_spec)
        args.append(q)
        out_specs.append(ctx_spec)
        out_shape.append(jax.ShapeDtypeStruct((ATT_HEADS, n_batch, n_ctx, ATT_VD), BF16))
    kern = functools.partial(_attn_kernel, lambda_init=lambda_init, n_ctx=n_ctx, ctx_out=ctx_out)
    outs = pl.pallas_call(
        kern,
        grid=(n_batch, ATT_HEADS, seq // tq),
        in_specs=in_specs + [kv_spec, kv_spec],
        out_specs=out_specs,
        out_shape=out_shape,
        scratch_shapes=[pltpu.VMEM((ATT_STREAMS, 2, ROW_TILE, t), F32)],
        compiler_params=_cparams(("parallel", "parallel", "arbitrary")),
        name="diff_attention",
    )(*args, k, v)
    return outs[0], (outs[1] if ctx_out else None)


def _halo_specs(width, tiles_per_batch):
    per = ROW_TILE // HALO

    def prev_idx(b, j):
        return (b, jnp.maximum(j * per - 1, 0), 0)

    def next_idx(b, j):
        return (b, jnp.minimum((j + 1) * per, tiles_per_batch * per - 1), 0)

    return (pl.BlockSpec((None, HALO, width), prev_idx), pl.BlockSpec((None, HALO, width), next_idx))


def _with_halo(prev_ref, cur, next_ref, width, tiles_per_batch, j):
    left_ok = jnp.where(j > 1, 1.0, 0.0)
    right_ok = jnp.where(jnp.logical_and(j > 0, j < tiles_per_batch - 1), 1.0, 0.0)
    return jnp.concatenate([prev_ref[:, 0:width] * left_ok, cur, next_ref[:, 0:width] * right_ok], axis=0)


def _dn_prep_kernel(dn_ref, prev_ref, next_ref, ab_ref, cw_ref, alog_ref, dtb_ref,
                    u_ref, w_ref, qg_ref, qk_ref, kt_ref, gl_ref, *, tiles_per_batch):
    j = pl.program_id(1)
    width = 3 * DN_W
    xe = _with_halo(prev_ref, dn_ref[:, 0:width], next_ref, width, tiles_per_batch, j)
    half = DN_CONV // 2
    acc = None
    n_ext = ROW_TILE + 2 * HALO
    for tap in range(DN_CONV):
        shifted = xe if tap == half else pltpu.roll(xe, (half - tap) % n_ext, 0)
        term = shifted[HALO:HALO + ROW_TILE, :] * cw_ref[tap:tap + 1, :]
        acc = term if acc is None else acc + term
    qkv = _silu(acc)

    lane_w = lax.broadcasted_iota(jnp.int32, (DN_W, DN_W), 1)
    row_w = lax.broadcasted_iota(jnp.int32, (DN_W, DN_W), 0)
    head_ones = jnp.where((lane_w >> DH_SHIFT) == (row_w >> DH_SHIFT), 1.0, 0.0).astype(BF16)

    def head_sums(x):
        hi = x.astype(BF16)
        lo = (x - hi.astype(F32)).astype(BF16)
        return (jnp.dot(hi, head_ones, preferred_element_type=F32)
                + jnp.dot(lo, head_ones, preferred_element_type=F32))

    def l2n(x):
        return x * lax.rsqrt(head_sums(x * x) + EPS)

    qn = l2n(qkv[:, 0:DN_W]) * (DN_DH ** -0.5)
    kn = l2n(qkv[:, DN_W:2 * DN_W])
    vv = qkv[:, 2 * DN_W:3 * DN_W]
    kn_t = kn.T

    ab = ab_ref[...]
    x = ab + dtb_ref[...]
    g = -jnp.exp(alog_ref[...]) * (jnp.maximum(x, 0.0) + jnp.log(1.0 + jnp.exp(-jnp.abs(x))))
    n_dh = 2 * DN_HEADS
    beta = _sigmoid(pltpu.roll(ab, LANES - n_dh, 1))
    r = lax.broadcasted_iota(jnp.int32, (ROW_TILE, ROW_TILE), 0)
    c = lax.broadcasted_iota(jnp.int32, (ROW_TILE, ROW_TILE), 1)
    same_chunk = (r >> CHUNK_SHIFT) == (c >> CHUNK_SHIFT)
    tri_f = jnp.where(jnp.logical_and(same_chunk, c <= r), 1.0, 0.0)
    tri_b = jnp.where(jnp.logical_and(same_chunk, c >= r), 1.0, 0.0)
    tri = jnp.concatenate([tri_f, tri_b], axis=0).astype(BF16)
    both, rest = None, g
    for _ in range(3):
        piece = rest.astype(BF16)
        rest = rest - piece.astype(F32)
        term = jnp.dot(tri, piece, preferred_element_type=F32)
        both = term if both is None else both + term
    lane = lax.broadcasted_iota(jnp.int32, g.shape, 1)
    gcum = jnp.where(lane < DN_HEADS, both[0:ROW_TILE, :], both[ROW_TILE:2 * ROW_TILE, :])
    gcum_t = gcum.T

    n_grp = 2 * DN_PAIRS
    sel_r = lax.broadcasted_iota(jnp.int32, (LANES, n_grp * LANES), 0)
    sel_c = lax.broadcasted_iota(jnp.int32, (LANES, n_grp * LANES), 1)
    grp = sel_c >> LANE_SHIFT
    sel = jnp.where(sel_r == (grp >> 1) * DN_HEADS + (grp & 1) * 2 + ((sel_c >> DH_SHIFT) & 1),
                    1.0, 0.0).astype(BF16)

    def spread(x, pieces):
        out, rest = None, x
        for _ in range(pieces):
            part = rest.astype(BF16)
            rest = rest - part.astype(F32)
            term = jnp.dot(part, sel, preferred_element_type=F32)
            out = term if out is None else out + term
        return out

    g_lanes = spread(gcum, 3)
    b_lanes = spread(beta, 3)

    ri = lax.broadcasted_iota(jnp.int32, (DN_CHUNK, LANES), 0)
    ci = lax.broadcasted_iota(jnp.int32, (DN_CHUNK, LANES), 1) & (DN_DH - 1)
    left = lax.broadcasted_iota(jnp.int32, (DN_CHUNK, LANES), 1) < DN_DH
    eye2 = ri == ci
    same_sub = (ri >> SUB_SHIFT) == (ci >> SUB_SHIFT)
    n_chunks = ROW_TILE // DN_CHUNK

    def bdiag(x):
        return jnp.concatenate([jnp.where(left, x, 0.0), jnp.where(left, 0.0, x)], axis=-2)

    def pmm(a, b):
        return _bmm(a, bdiag(b))

    m_list, xu_list, xw_list = [], [], []
    for ch in range(n_chunks):
        rows = slice(ch * DN_CHUNK, (ch + 1) * DN_CHUNK)
        gram, qkm = [], []
        for p in range(DN_PAIRS):
            cols = slice(p * LANES, (p + 1) * LANES)
            kp = kn[rows, cols]
            kbd = bdiag(kp)
            gram.append(_dot_nt(kp, kbd))
            qkm.append(_dot_nt(qn[rows, cols], kbd))
        for grp_i in range(n_grp):
            d, p = grp_i // DN_PAIRS, grp_i % DN_PAIRS
            cols = slice(p * LANES, (p + 1) * LANES)
            gcols = slice(grp_i * LANES, (grp_i + 1) * LANES)
            incl = (ci <= ri) if d == 0 else (ci >= ri)
            strict = (ci < ri) if d == 0 else (ci > ri)
            last = DN_CHUNK - 1 if d == 0 else 0
            g_col = g_lanes[rows, gcols]
            b_col = b_lanes[rows, gcols]
            g_row = jnp.sum(jnp.where(eye2, g_col, 0.0), axis=0, keepdims=True)
            decay = jnp.where(incl, jnp.exp(jnp.minimum(g_col - g_row, 0.0)), 0.0)
            eg = jnp.exp(g_col)
            kp = kn[rows, cols]
            m_list.append(jnp.where(strict, gram[p] * decay, 0.0) * b_col)
            xu_list.append(vv[rows, cols] * b_col)
            xw_list.append(kp * (b_col * eg))
            qg_ref[grp_i, rows, :] = (qn[rows, cols] * eg).astype(qg_ref.dtype)
            qk_ref[grp_i, rows, :] = (qkm[p] * decay).astype(qk_ref.dtype)
            fac = []
            for hh in range(2):
                idx = d * DN_HEADS + 2 * p + hh
                gr = gcum_t[idx:idx + 1, rows]
                fac.append(jnp.broadcast_to(jnp.exp(gr[:, last:last + 1] - gr), (DN_DH, DN_CHUNK)))
            kt_ref[grp_i, ch] = (kn_t[cols, rows] * jnp.concatenate(fac, axis=0)).astype(kt_ref.dtype)
            gl_ref[grp_i, ch * HALO:(ch + 1) * HALO, :] = jnp.broadcast_to(eg[last:last + 1, :], (HALO, LANES))
    m = jnp.stack(m_list)
    n_diag = jnp.where(same_sub, m, 0.0)
    m_off = m - n_diag
    xp = -n_diag
    dinv = jnp.where(eye2, 1.0, 0.0) + xp
    for _ in range(3):
        xp = pmm(xp, xp)
        dinv = dinv + pmm(dinv, xp)
    f = pmm(dinv, m_off)
    f2 = pmm(f, f)

    def bdiag2(y):
        return jnp.concatenate([bdiag(y[..., 0:LANES]), bdiag(y[..., LANES:2 * LANES])], axis=-1)

    y = _bmm(dinv, jnp.concatenate([bdiag(jnp.stack(xu_list)), bdiag(jnp.stack(xw_list))], axis=-1))
    y = y - _bmm(f, bdiag2(y))
    sol = y + _bmm(f2, bdiag2(y))
    for ch in range(n_chunks):
        rows = slice(ch * DN_CHUNK, (ch + 1) * DN_CHUNK)
        for grp_i in range(n_grp):
            i = ch * n_grp + grp_i
            u_ref[grp_i, rows, :] = sol[i, :, 0:LANES]
            w_ref[grp_i, rows, :] = sol[i, :, LANES:2 * LANES].astype(w_ref.dtype)


def _dn_prep(dn, ab, conv_w, a_log, dt_bias, tiles_per_batch):
    n_batch, t, dn_cols = dn.shape
    n_dh = 2 * DN_HEADS
    n_grp = 2 * DN_PAIRS
    pad = jnp.zeros((1, LANES - n_dh), F32)
    alog_row = jnp.concatenate([a_log.reshape(1, n_dh).astype(F32), pad], axis=-1)
    dtb_row = jnp.concatenate([dt_bias.reshape(1, n_dh).astype(F32), pad], axis=-1)
    prev_spec, next_spec = _halo_specs(dn_cols, tiles_per_batch)
    tok = lambda b, j: (b, 0, j, 0)
    out_specs = [pl.BlockSpec((None, n_grp, ROW_TILE, LANES), tok)] * 4
    out_specs += [pl.BlockSpec((None, n_grp, ROW_TILE // DN_CHUNK, LANES, DN_CHUNK), lambda b, j: (b, 0, j, 0, 0)),
                  pl.BlockSpec((None, n_grp, HALO * ROW_TILE // DN_CHUNK, LANES), tok)]
    tok_shape = (n_batch, n_grp, t, LANES)
    out_shape = [jax.ShapeDtypeStruct(tok_shape, F32),
                 jax.ShapeDtypeStruct(tok_shape, BF16),
                 jax.ShapeDtypeStruct(tok_shape, BF16),
                 jax.ShapeDtypeStruct(tok_shape, BF16),
                 jax.ShapeDtypeStruct((n_batch, n_grp, t // DN_CHUNK, LANES, DN_CHUNK), BF16),
                 jax.ShapeDtypeStruct((n_batch, n_grp, HALO * t // DN_CHUNK, LANES), F32)]
    return pl.pallas_call(
        functools.partial(_dn_prep_kernel, tiles_per_batch=tiles_per_batch),
        grid=(n_batch, tiles_per_batch),
        in_specs=[pl.BlockSpec((None, ROW_TILE, dn_cols), lambda b, j: (b, j, 0)),
                  prev_spec, next_spec,
                  pl.BlockSpec((None, ROW_TILE, LANES), lambda b, j: (b, j, 0)),
                  _const_spec(conv_w.shape), _const_spec((1, LANES)), _const_spec((1, LANES))],
        out_specs=out_specs,
        out_shape=out_shape,
        compiler_params=_cparams(("parallel", "parallel")),
        name="dn_prep",
    )(dn, dn, dn, ab, conv_w, alog_row, dtb_row)


def _dn_scan_kernel(*refs):
    ins, (of_ref, ob_ref, s_ref) = refs[:12], refs[12:]

    @pl.when(pl.program_id(0) == 0)
    def _():
        s_ref[...] = jnp.zeros_like(s_ref)

    left = lax.broadcasted_iota(jnp.int32, (DN_DH, LANES), 1) < DN_DH

    def bdiag(x):
        return jnp.concatenate([jnp.where(left, x, 0.0), jnp.where(left, 0.0, x)], axis=-2)

    for d, o_ref in enumerate((of_ref, ob_ref)):
        u_ref, w_ref, qg_ref, qk_ref, kt_ref, gl_ref = ins[6 * d:6 * d + 6]
        flat = lambda r: r[...].reshape((-1,) + r.shape[2:])
        s = s_ref[d]
        s_bd = bdiag(s)
        both = _bmm(jnp.concatenate([flat(w_ref), flat(qg_ref)], axis=1), s_bd)
        v_new = flat(u_ref) - both[:, 0:DN_CHUNK, :]
        o = both[:, DN_CHUNK:2 * DN_CHUNK, :] + _bmm(flat(qk_ref), bdiag(v_new))
        o_ref[...] = o.reshape(o_ref.shape)
        full = _bmm(flat(kt_ref), v_new)
        s_ref[d] = s * flat(gl_ref)[:, 0:1, :] + jnp.where(left, full[:, 0:DN_DH, :], full[:, DN_DH:2 * DN_DH, :])


def _dn_scan(prep, n_ctx):
    u, w, qg, qk, kt, gl = prep
    n_batch, _, t, _ = u.shape
    n_steps = t // DN_CHUNK
    ctx_chunks = n_ctx // DN_CHUNK

    def chunk_of(d, i):
        if d == 0:
            return i
        return jnp.where(i < ctx_chunks, ctx_chunks - 1 - i, n_steps - 1 + ctx_chunks - i)

    in_specs = []
    for d in range(2):
        tok = functools.partial(lambda i, d: (0, d, chunk_of(d, i), 0), d=d)
        in_specs += [pl.BlockSpec((n_batch, DN_PAIRS, DN_CHUNK, LANES), tok)] * 4
        in_specs += [pl.BlockSpec((n_batch, DN_PAIRS, None, LANES, DN_CHUNK),
                                  functools.partial(lambda i, d: (0, d, chunk_of(d, i), 0, 0), d=d)),
                     pl.BlockSpec((n_batch, DN_PAIRS, HALO, LANES), tok)]
    out_specs = [pl.BlockSpec((n_batch, DN_PAIRS, DN_CHUNK, LANES),
                              functools.partial(lambda i, d: (0, 0, chunk_of(d, i), 0), d=d))
                 for d in range(2)]
    out_shape = [jax.ShapeDtypeStruct((n_batch, DN_PAIRS, t, LANES), F32)] * 2
    return pl.pallas_call(
        _dn_scan_kernel,
        grid=(n_steps,),
        in_specs=in_specs,
        out_specs=out_specs,
        out_shape=out_shape,
        scratch_shapes=[pltpu.VMEM((2, n_batch * DN_PAIRS, DN_DH, LANES), F32)],
        compiler_params=_cparams(("arbitrary",)),
        name="dn_scan",
    )(u, w, qg, qk, kt, gl, u, w, qg, qk, kt, gl)


def _merge_kernel(h_ref, mod_ref, att_ref, attc_ref, of_ref, ob_ref, z_ref, dng_ref,
                  pool_ref, pprev_ref, pnext_ref, pw_ref, ps_ref, gate_ref,
                  wba_ref, wbd_ref, wbp_ref, wo_ref, o_ref, *, tiles_per_batch, first_tile, seq, n_ctx):
    j = pl.program_id(1) + first_tile
    d = h_ref.shape[-1]
    o = jnp.concatenate([of_ref[p] + ob_ref[p] for p in range(DN_PAIRS)], axis=-1)
    lane_w = lax.broadcasted_iota(jnp.int32, (DN_W, DN_W), 1)
    row_w = lax.broadcasted_iota(jnp.int32, (DN_W, DN_W), 0)
    head_ones = jnp.where((lane_w >> DH_SHIFT) == (row_w >> DH_SHIFT), 1.0, 0.0).astype(BF16)
    sq = o * o
    sq_hi = sq.astype(BF16)
    sq_lo = (sq - sq_hi.astype(F32)).astype(BF16)
    ssq = (jnp.dot(sq_hi, head_ones, preferred_element_type=F32)
           + jnp.dot(sq_lo, head_ones, preferred_element_type=F32))
    dn = o * lax.rsqrt(ssq * (1.0 / DN_DH) + EPS) * dng_ref[...] * _silu(z_ref[...])
    cur = pool_ref[...]
    xe = _with_halo(pprev_ref, cur, pnext_ref, POOL_W, tiles_per_batch, j)
    length = jnp.where(j == 0, n_ctx, seq)
    t0 = jnp.where(j == 0, 0, (j - 1) * ROW_TILE)
    tpos = t0 + lax.broadcasted_iota(jnp.int32, (ROW_TILE, POOL_W), 0)
    lane = lax.broadcasted_iota(jnp.int32, (ROW_TILE, POOL_W), 1)
    n_ext = ROW_TILE + 2 * HALO
    sums = {1: xe}
    win = 1
    while win < max(POOL_WINDOWS):
        prev = sums[win]
        sums[2 * win] = prev + pltpu.roll(prev, n_ext - win, 0)
        win *= 2
    mean = jnp.zeros((ROW_TILE, POOL_W), F32)
    for gi, win in enumerate(POOL_WINDOWS):
        wsum = pltpu.roll(sums[win], win // 2, 0)[HALO:HALO + ROW_TILE, :]
        lo = jnp.clip(tpos - win // 2, 0, length)
        hi = jnp.clip(tpos - win // 2 + win, 0, length)
        mean = jnp.where((lane >> DH_SHIFT) == gi, wsum / (hi - lo).astype(F32), mean)
    pooled = _dot(mean - cur, pw_ref[...]) * ps_ref[...]
    ga = gate_ref[:, 0:d].astype(F32)
    gd = gate_ref[:, d:2 * d].astype(F32)
    gp = gate_ref[:, 2 * d:3 * d].astype(F32)
    att = jnp.concatenate([jnp.where(j == 0, attc_ref[hd], att_ref[hd])
                           for hd in range(ATT_HEADS)], axis=-1)
    mix = (ga * jnp.dot(att, wba_ref[...], preferred_element_type=F32)
           + gd * _dot(dn, wbd_ref[...]) + gp * _dot(pooled, wbp_ref[...]))
    y = _dot(mix, wo_ref[...])
    o_ref[...] = h_ref[...] + mod_ref[0:1, :] * y


def _merge(h, mod_gate, att, att_ctx, o_f, o_b, dn, dn_norm_g, pool, pool_w_bd, pool_scale, gates,
           wb_att, wb_dn, wb_pool, w_out, first_tile, n_ctx):
    n_batch, t, d = h.shape
    tpb = t // ROW_TILE
    n_rows = tpb - first_tile
    seq = t - n_ctx
    ft = first_tile
    prev_spec, next_spec = _halo_specs(POOL_W, tpb)
    shift = lambda f: (lambda b, j: f(b, j + ft))
    prev_spec = pl.BlockSpec(prev_spec.block_shape, shift(prev_spec.index_map))
    next_spec = pl.BlockSpec(next_spec.block_shape, shift(next_spec.index_map))
    row = lambda b, j: (b, j + ft, 0)

    def mod_idx(b, j):
        return (jnp.where(j + ft == 0, n_batch, b), 0, 0)

    weights = [wb_att, wb_dn, wb_pool, w_out]
    kern = functools.partial(_merge_kernel, tiles_per_batch=tpb, first_tile=ft, seq=seq, n_ctx=n_ctx)
    return pl.pallas_call(
        kern,
        grid=(n_batch, n_rows),
        in_specs=[pl.BlockSpec((None, ROW_TILE, d), row),
                  pl.BlockSpec((None, 1, d), mod_idx),
                  pl.BlockSpec((ATT_HEADS, None, ROW_TILE, ATT_VD),
                               lambda b, j: (0, b, jnp.maximum(j + ft - 1, 0), 0)),
                  pl.BlockSpec((ATT_HEADS, None, ROW_TILE, ATT_VD), lambda b, j: (0, b, 0, 0)),
                  pl.BlockSpec((None, DN_PAIRS, ROW_TILE, LANES), lambda b, j: (b, 0, j + ft, 0)),
                  pl.BlockSpec((None, DN_PAIRS, ROW_TILE, LANES), lambda b, j: (b, 0, j + ft, 0)),
                  pl.BlockSpec((None, ROW_TILE, DN_W), lambda b, j: (b, j + ft, 3)),
                  _const_spec((1, DN_W)),
                  pl.BlockSpec((None, ROW_TILE, POOL_W), row),
                  prev_spec, next_spec,
                  _const_spec(pool_w_bd.shape), _const_spec((1, POOL_W)),
                  pl.BlockSpec((None, ROW_TILE, 3 * d), row)]
                 + [_const_spec(w.shape) for w in weights],
        out_specs=pl.BlockSpec((None, ROW_TILE, d), lambda b, j: (b, j, 0)),
        out_shape=jax.ShapeDtypeStruct((n_batch, n_rows * ROW_TILE, d), F32),
        compiler_params=_cparams(("parallel", "parallel")),
        name="merge",
    )(h, mod_gate, att, att if att_ctx is None else att_ctx, o_f, o_b, dn,
      jnp.tile(dn_norm_g.reshape(1, DN_DH), (1, DN_HEADS)),
      pool, pool, pool,
      pool_w_bd, pool_scale.reshape(1, POOL_W), gates, *weights)


def _ffn_kernel(*refs, n_experts, n_fchunks, final_norm):
    if n_experts > 1:
        h_ref, mod_ref, g_ref, fg_ref, rw_ref, w1_ref, w3_ref, w2_ref, o_ref, u_ref, acc_ref, comb_ref = refs
    else:
        h_ref, mod_ref, g_ref, fg_ref, w1_ref, w3_ref, w2_ref, o_ref, u_ref, acc_ref = refs
    step = pl.program_id(2)
    e = step // n_fchunks

    @pl.when(step == 0)
    def _():
        u = _norm_mod(h_ref[...], g_ref[...], mod_ref[0:1, :], mod_ref[1:2, :])
        u_ref[...] = u.astype(BF16)
        acc_ref[...] = jnp.zeros_like(acc_ref)
        if n_experts > 1:
            logits = _dot(u, rw_ref[...])
            lane = lax.broadcasted_iota(jnp.int32, logits.shape, 1).astype(F32)
            neg = jnp.float32(-jnp.inf)
            lg = jnp.where(lane < n_experts, logits, neg)
            v1 = jnp.max(lg, axis=-1, keepdims=True)
            i1 = jnp.min(jnp.where(lg == v1, lane, float(LANES)), axis=-1, keepdims=True)
            lg2 = jnp.where(lane == i1, neg, lg)
            v2 = jnp.max(lg2, axis=-1, keepdims=True)
            i2 = jnp.min(jnp.where(lg2 == v2, lane, float(LANES)), axis=-1, keepdims=True)
            e2 = jnp.exp(v2 - v1)
            den = 1.0 + e2
            comb_ref[...] = jnp.where(lane == i1, 1.0 / den, 0.0) + jnp.where(lane == i2, e2 / den, 0.0)

    u = u_ref[...]
    hid = _silu(jnp.dot(u, w1_ref[...], preferred_element_type=F32)) * jnp.dot(
        u, w3_ref[...], preferred_element_type=F32)
    y = _dot(hid, w2_ref[...])
    if n_experts > 1:
        lane = lax.broadcasted_iota(jnp.int32, comb_ref.shape, 1)
        y = y * jnp.sum(jnp.where(lane == e, comb_ref[...], 0.0), axis=-1, keepdims=True)
    acc_ref[...] += y

    @pl.when(step == n_experts * n_fchunks - 1)
    def _():
        out = h_ref[...] + mod_ref[2:3, :] * acc_ref[...]
        if final_norm:
            out = out * lax.rsqrt(jnp.mean(out * out, axis=-1, keepdims=True) + EPS) * fg_ref[...]
        o_ref[...] = out


def _ffn(h, mod, norm_g, final_g, router_w, w1, w3, w2, has_ctx, final_norm):
    n_batch, t, d = h.shape
    n_exp, _, f = w1.shape
    n_rows = t // ROW_TILE
    row = lambda b, j, e: (b, j, 0)

    def mod_idx(b, j, e):
        return (jnp.where(jnp.logical_and(has_ctx, j == 0), n_batch, b), 0, 0)

    in_specs = [pl.BlockSpec((None, ROW_TILE, d), row),
                pl.BlockSpec((None, 3, d), mod_idx),
                _const_spec((1, d)), _const_spec((1, d))]
    args = [h, mod, norm_g.reshape(1, d), final_g.reshape(1, d)]
    scratch = [pltpu.VMEM((ROW_TILE, d), BF16), pltpu.VMEM((ROW_TILE, d), F32)]
    if n_exp > 1:
        in_specs.append(_const_spec(router_w.shape))
        args.append(router_w)
        scratch.append(pltpu.VMEM((ROW_TILE, LANES), F32))
    nfc = 1 if n_exp == 1 else 2
    fc = f // nfc
    assert fc * nfc == f and fc % LANES == 0
    in_specs += [pl.BlockSpec((None, d, fc), lambda b, j, s: (s // nfc, 0, s % nfc)),
                 pl.BlockSpec((None, d, fc), lambda b, j, s: (s // nfc, 0, s % nfc)),
                 pl.BlockSpec((None, fc, d), lambda b, j, s: (s // nfc, s % nfc, 0))]
    args += [w1, w3, w2]
    return pl.pallas_call(
        functools.partial(_ffn_kernel, n_experts=n_exp, n_fchunks=nfc, final_norm=final_norm),
        grid=(n_batch, n_rows, n_exp * nfc),
        in_specs=in_specs,
        out_specs=pl.BlockSpec((None, ROW_TILE, d), row),
        out_shape=jax.ShapeDtypeStruct((n_batch, t, d), F32),
        scratch_shapes=scratch,
        input_output_aliases={0: 0},
        compiler_params=_cparams(("parallel", "parallel", "arbitrary")),
        name="ffn_all_experts" if n_exp > 1 else "ffn_dense",
    )(*args)


MOE_BLOCK = 2048
MOE_SUB = 512
MOE_TILE = 288
MOE_ROW_ALIGN = 32
MOE_ROWS = -(-MOE_BLOCK // MOE_TILE) * MOE_TILE
MOE_PUT_TILE = 256
MOE_FCHUNKS = 4


def _top2_routing(logits, n_experts):
    lane = lax.broadcasted_iota(jnp.int32, logits.shape, 1).astype(F32)
    neg = jnp.float32(-jnp.inf)
    lg = jnp.where(lane < n_experts, logits, neg)
    v1 = jnp.max(lg, axis=-1, keepdims=True)
    i1 = jnp.min(jnp.where(lg == v1, lane, float(LANES)), axis=-1, keepdims=True)
    lg2 = jnp.where(lane == i1, neg, lg)
    v2 = jnp.max(lg2, axis=-1, keepdims=True)
    i2 = jnp.min(jnp.where(lg2 == v2, lane, float(LANES)), axis=-1, keepdims=True)
    e2 = jnp.exp(v2 - v1)
    den = 1.0 + e2
    first = lane == i1
    second = lane == i2
    sel = jnp.where(jnp.logical_or(first, second), 1.0, 0.0)
    comb = jnp.where(first, 1.0 / den, 0.0) + jnp.where(second, e2 / den, 0.0)
    return sel, comb


def _moe_kernel(h_ref, mod_ref, g_ref, fg_ref, rw_ref, w1_ref, w3_ref, w2_ref, o_ref,
                u_ref, x_ref, y_ref, rank_ref, rank_t_ref, comb_ref, cnt_ref, span_ref, *, n_experts, final_norm):
    e = pl.program_id(2)
    f = pl.program_id(3)
    n_sub = MOE_BLOCK // MOE_SUB

    @pl.when(jnp.logical_and(e == 0, f == 0))
    def _route():
        r = lax.broadcasted_iota(jnp.int32, (ROW_TILE, ROW_TILE), 0)
        c = lax.broadcasted_iota(jnp.int32, (ROW_TILE, ROW_TILE), 1)
        before = jnp.where(c < r, 1.0, 0.0).astype(BF16)
        count = jnp.zeros((1, LANES), F32)
        for t in range(MOE_BLOCK // ROW_TILE):
            rows = slice(t * ROW_TILE, (t + 1) * ROW_TILE)
            u = _norm_mod(h_ref[rows, :], g_ref[...], mod_ref[0:1, :], mod_ref[1:2, :])
            u_ref[rows, :] = u.astype(BF16)
            sel, comb = _top2_routing(_dot(u, rw_ref[...]), n_experts)
            rank = jnp.dot(before, sel.astype(BF16), preferred_element_type=F32) + count
            rank_ref[rows, :] = jnp.where(sel > 0.0, rank, -1.0)
            comb_ref[rows, :] = comb
            count = count + jnp.sum(sel, axis=0, keepdims=True)
        lane = lax.broadcasted_iota(jnp.int32, (1, LANES), 1)
        for ee in range(n_experts):
            cnt_ref[ee] = jnp.sum(jnp.where(lane == ee, count, 0.0)).astype(jnp.int32)
        rank_t_ref[...] = rank_ref[...].T
        o_ref[...] = jnp.zeros_like(o_ref)
        for s in range(n_sub):
            rk = rank_ref[s * MOE_SUB:(s + 1) * MOE_SUB, :]
            lo_v = jnp.min(jnp.where(rk >= 0.0, rk, float(MOE_BLOCK)), axis=0, keepdims=True)
            hi_v = jnp.max(rk, axis=0, keepdims=True)
            for ee in range(n_experts):
                span_ref[2 * (s * n_experts + ee)] = jnp.sum(jnp.where(lane == ee, lo_v, 0.0)).astype(jnp.int32)
                span_ref[2 * (s * n_experts + ee) + 1] = jnp.sum(jnp.where(lane == ee, hi_v, 0.0)).astype(jnp.int32)

        @pl.when(jnp.logical_and(pl.program_id(0) == 0, pl.program_id(1) == 0))
        def _():
            y_ref[...] = jnp.zeros_like(y_ref)

    n_rows = cnt_ref[e]
    n_tiles = (n_rows + MOE_TILE - 1) // MOE_TILE
    rank_row = rank_t_ref[pl.ds(e, 1), :]

    def sub_hits(row0, s, tile=MOE_TILE):
        rk = rank_row[:, s * MOE_SUB:(s + 1) * MOE_SUB]
        lo = span_ref[2 * (s * n_experts + e)]
        hi = span_ref[2 * (s * n_experts + e) + 1]
        return rk, jnp.logical_and(hi >= row0, lo < row0 + tile)

    @pl.when(f == 0)
    def _compact():
        def body(i, carry):
            row0 = pl.multiple_of(i * MOE_TILE, MOE_ROW_ALIGN)
            ids = (row0 + lax.broadcasted_iota(jnp.int32, (MOE_TILE, 1), 0)).astype(F32)
            x_ref[pl.ds(row0, MOE_TILE), :] = jnp.zeros((MOE_TILE, x_ref.shape[1]), BF16)
            for s in range(n_sub):
                rk, hit = sub_hits(row0, s)

                @pl.when(hit)
                def _():
                    pick = jnp.where(rk == ids, 1.0, 0.0).astype(BF16)
                    got = jnp.dot(pick, u_ref[s * MOE_SUB:(s + 1) * MOE_SUB, :], preferred_element_type=F32)
                    x_ref[pl.ds(row0, MOE_TILE), :] += got.astype(BF16)
            return carry

        lax.fori_loop(0, n_tiles, body, 0)

    def expert_body(i, carry):
        row0 = pl.multiple_of(i * MOE_TILE, MOE_ROW_ALIGN)
        x = x_ref[pl.ds(row0, MOE_TILE), :]
        hid = _silu(jnp.dot(x, w1_ref[...], preferred_element_type=F32)) * jnp.dot(
            x, w3_ref[...], preferred_element_type=F32)
        y = _dot(hid, w2_ref[...])

        @pl.when(f == 0)
        def _():
            y_ref[pl.ds(row0, MOE_TILE), :] = y

        @pl.when(f > 0)
        def _():
            y_ref[pl.ds(row0, MOE_TILE), :] += y

        return carry

    lax.fori_loop(0, n_tiles, expert_body, 0)

    @pl.when(f == MOE_FCHUNKS - 1)
    def _expand():
        lanes = lax.broadcasted_iota(jnp.int32, (MOE_SUB, LANES), 1)

        def body(i, carry):
            row0 = pl.multiple_of(i * MOE_PUT_TILE, MOE_PUT_TILE)
            yt = y_ref[pl.ds(row0, MOE_PUT_TILE), :].astype(BF16)
            ids = (row0 + lax.broadcasted_iota(jnp.int32, (1, MOE_PUT_TILE), 1)).astype(F32)
            for s in range(n_sub):
                _, hit = sub_hits(row0, s, MOE_PUT_TILE)

                @pl.when(hit)
                def _():
                    rows = slice(s * MOE_SUB, (s + 1) * MOE_SUB)
                    rk_col = jnp.sum(jnp.where(lanes == e, rank_ref[rows, :], 0.0), axis=1, keepdims=True)
                    w_col = jnp.sum(jnp.where(lanes == e, comb_ref[rows, :], 0.0), axis=1, keepdims=True)
                    put = jnp.where(rk_col == ids, 1.0, 0.0).astype(BF16)
                    o_ref[rows, :] += w_col * jnp.dot(put, yt, preferred_element_type=F32)
            return carry

        lax.fori_loop(0, (n_rows + MOE_PUT_TILE - 1) // MOE_PUT_TILE, body, 0)

    @pl.when(jnp.logical_and(e == n_experts - 1, f == MOE_FCHUNKS - 1))
    def _finish():
        for t in range(MOE_BLOCK // ROW_TILE):
            rows = slice(t * ROW_TILE, (t + 1) * ROW_TILE)
            out = h_ref[rows, :] + mod_ref[2:3, :] * o_ref[rows, :]
            if final_norm:
                out = out * lax.rsqrt(jnp.mean(out * out, axis=-1, keepdims=True) + EPS) * fg_ref[...]
            o_ref[rows, :] = out


def _moe(h, mod, norm_g, final_g, router_w, w1, w3, w2, final_norm):
    n_batch, s, d = h.shape
    n_exp, _, f = w1.shape
    assert s % MOE_BLOCK == 0 and f % (MOE_FCHUNKS * LANES) == 0 and n_exp <= 8
    fc = f // MOE_FCHUNKS
    blk = lambda b, j, e, c: (b, j, 0)
    single = pl.Buffered(1)

    return pl.pallas_call(
        functools.partial(_moe_kernel, n_experts=n_exp, final_norm=final_norm),
        grid=(n_batch, s // MOE_BLOCK, n_exp, MOE_FCHUNKS),
        in_specs=[pl.BlockSpec((None, MOE_BLOCK, d), blk, pipeline_mode=single),
                  pl.BlockSpec((None, 3, d), lambda b, j, e, c: (b, 0, 0)),
                  _const_spec((1, d)), _const_spec((1, d)), _const_spec(router_w.shape),
                  pl.BlockSpec((None, d, fc), lambda b, j, e, c: (e, 0, c)),
                  pl.BlockSpec((None, d, fc), lambda b, j, e, c: (e, 0, c)),
                  pl.BlockSpec((None, fc, d), lambda b, j, e, c: (e, c, 0))],
        out_specs=pl.BlockSpec((None, MOE_BLOCK, d), blk, pipeline_mode=single),
        out_shape=jax.ShapeDtypeStruct((n_batch, s, d), F32),
        scratch_shapes=[pltpu.VMEM((MOE_BLOCK, d), BF16),
                        pltpu.VMEM((MOE_ROWS, d), BF16),
                        pltpu.VMEM((MOE_ROWS, d), F32),
                        pltpu.VMEM((MOE_BLOCK, LANES), F32),
                        pltpu.VMEM((LANES, MOE_BLOCK), F32),
                        pltpu.VMEM((MOE_BLOCK, LANES), F32),
                        pltpu.SMEM((HALO,), jnp.int32),
                        pltpu.SMEM((2 * (MOE_BLOCK // MOE_SUB) * HALO,), jnp.int32)],
        compiler_params=_cparams(("arbitrary", "arbitrary", "arbitrary", "arbitrary")),
        name="moe_top2",
    )(h, mod, norm_g.reshape(1, d), final_g.reshape(1, d), router_w, w1, w3, w2)


def _block_diag(w):
    g, n, _ = w.shape
    out = jnp.zeros((g * n, g * n), w.dtype)
    for i in range(g):
        out = out.at[i * n:(i + 1) * n, i * n:(i + 1) * n].set(w[i])
    return out


def kernel(x, c, ctx, c_ctx, ada_w, ada_b, norm1_g, norm2_g, w_in, attn_lambda, attn_subln_g, dn_conv_w,
           dn_a_log, dn_dt_bias, dn_norm_g, pool_w, pool_scale, w_branch, w_out, ffn_w1, ffn_w3, ffn_w2,
           router_w, moe_w1, moe_w3, moe_w2, final_norm_g):
    n_batch, seq, d = x.shape
    n_ctx = ctx.shape[1]
    depth = ada_w.shape[0]
    assert n_ctx == ROW_TILE and seq % ROW_TILE == 0 and seq % GRID_W == 0 and d == 1024
    t = n_ctx + seq
    tpb = t // ROW_TILE

    h = None
    c_rows = jnp.concatenate([c, c_ctx[None, :], jnp.zeros((8 - n_batch - 1, d), F32)], axis=0)
    mods = _ada_mod(c_rows, ada_w, ada_b).reshape(depth, 8, 6, d)
    rope_tabs = _rope_tables(n_ctx, seq)

    sizes = (ATT_W, ATT_W, ATT_W, DN_W, DN_W, DN_W, DN_W, 2 * DN_HEADS, 2 * DN_HEADS, POOL_W, 3 * d)
    offs = np.concatenate([[0], np.cumsum(sizes)]).tolist()

    for l in range(depth):
        last = l == depth - 1
        first_tile = 1 if last else 0
        lambda_init = 0.8 - 0.6 * math.exp(-0.3 * l)
        wl = w_in[l].astype(BF16)
        seg = lambda a, b: wl[:, offs[a]:offs[b]]
        w_ab = jnp.concatenate([seg(7, 9), jnp.zeros((d, LANES - 4 * DN_HEADS), BF16)], axis=1)
        weights = [seg(0, 1), seg(1, 2), seg(2, 3), seg(3, 7), w_ab, seg(9, 10), seg(10, 11)]
        mod = mods[l, :n_batch + 1]
        src = (ctx, x) if l == 0 else h.reshape(n_batch * t, d)
        outs = _in_projection(src, mod[:, 0:2], norm1_g[l], rope_tabs, weights, n_batch, tpb)
        q, k, v, dn, ab, pool, gates = outs[:7]
        if l == 0:
            h = outs[7].reshape(n_batch, t, d)
        q, k, v = (a.reshape(ATT_HEADS, n_batch, t, ATT_VD) for a in (q, k, v))
        att, att_ctx = _diff_attention(q, k, v, attn_lambda[l], attn_subln_g[l], lambda_init, n_ctx, not last)
        dn = dn.reshape(n_batch, t, 4 * DN_W)
        prep = _dn_prep(dn, ab.reshape(n_batch, t, LANES), dn_conv_w[l], dn_a_log[l], dn_dt_bias[l], tpb)
        o_f, o_b = _dn_scan(prep, n_ctx)
        wb = w_branch[l].astype(BF16)
        h = _merge(h, mod[:, 2:3], att, att_ctx, o_f, o_b, dn, dn_norm_g[l], pool.reshape(n_batch, t, POOL_W),
                   _block_diag(pool_w[l]).astype(BF16), pool_scale[l], gates.reshape(n_batch, t, 3 * d),
                   wb[:ATT_W], wb[ATT_W:ATT_W + DN_W], wb[ATT_W + DN_W:], w_out[l].astype(BF16),
                   first_tile, n_ctx)
        i = l // 2
        if l % 2 == 0:
            h = _ffn(h, mod[:, 3:6], norm2_g[l], final_norm_g, None, ffn_w1[i][None].astype(BF16),
                     ffn_w3[i][None].astype(BF16), ffn_w2[i][None].astype(BF16), not last, last)
        else:
            rw = jnp.concatenate([router_w[i], jnp.zeros((d, LANES - N_EXPERTS), F32)], axis=1).astype(BF16)
            w1, w3, w2 = moe_w1[i].astype(BF16), moe_w3[i].astype(BF16), moe_w2[i].astype(BF16)
            if last:
                h = _moe(h, mod[:n_batch, 3:6], norm2_g[l], final_norm_g, rw, w1, w3, w2, True)
            else:
                h = _ffn(h, mod[:, 3:6], norm2_g[l], final_norm_g, rw, w1, w3, w2, True, False)
    return h
```

```python
import functools
import math

import numpy as np
import jax
import jax.numpy as jnp
from jax import lax
from jax.experimental import pallas as pl
from jax.experimental.pallas import tpu as pltpu

F32 = jnp.float32
BF16 = jnp.bfloat16

EPS = 1e-6
GRID_W = 64
ROPE_BASE = 10000.0
ATT_HEADS = 4
ATT_DH = 64
ATT_VD = 2 * ATT_DH
ATT_W = ATT_HEADS * ATT_VD
DN_HEADS = 4
DN_DH = 64
DN_W = DN_HEADS * DN_DH
DN_PAIRS = DN_HEADS // 2
DN_CONV = 5
DN_CHUNK = 64
DN_SUB = 16
POOL_WINDOWS = (2, 4, 8, 16)
POOL_GD = 64
POOL_W = len(POOL_WINDOWS) * POOL_GD
N_EXPERTS = 8

ROW_TILE = 256
HALO = 8
LANES = 128
VMEM_LIMIT = 56 * 1024 * 1024
DH_SHIFT = DN_DH.bit_length() - 1
CHUNK_SHIFT = DN_CHUNK.bit_length() - 1
SUB_SHIFT = DN_SUB.bit_length() - 1
LANE_SHIFT = LANES.bit_length() - 1
assert POOL_GD == DN_DH == 1 << DH_SHIFT and DN_CHUNK == 1 << CHUNK_SHIFT and DN_SUB == 1 << SUB_SHIFT


def _cparams(sem):
    return pltpu.CompilerParams(dimension_semantics=sem, vmem_limit_bytes=VMEM_LIMIT)


def _dot(a, b):
    return jnp.dot(a.astype(BF16), b.astype(BF16), preferred_element_type=F32)


def _dot_nt(a, b):
    return lax.dot_general(a.astype(BF16), b.astype(BF16), (((1,), (1,)), ((), ())),
                           preferred_element_type=F32)


def _bmm(a, b):
    return lax.dot_general(a.astype(BF16), b.astype(BF16), (((2,), (1,)), ((0,), (0,))),
                           preferred_element_type=F32)


def _sigmoid(x):
    return 1.0 / (1.0 + jnp.exp(-x))


def _silu(x):
    return x * _sigmoid(x)


def _const_spec(shape):
    nd = len(shape)
    return pl.BlockSpec(shape, lambda *_: (0,) * nd)


def _ada_kernel(c_ref, w_ref, b_ref, o_ref):
    o_ref[...] = _dot(_silu(c_ref[...]), w_ref[...]) + b_ref[...]


def _ada_mod(c_rows, ada_w, ada_b):
    depth, d, six_d = ada_w.shape
    n = six_d // d
    rows = c_rows.shape[0]
    return pl.pallas_call(
        _ada_kernel,
        grid=(depth, n),
        in_specs=[pl.BlockSpec((rows, d), lambda l, j: (0, 0)),
                  pl.BlockSpec((None, d, d), lambda l, j: (l, 0, j)),
                  pl.BlockSpec((None, 1, d), lambda l, j: (l, 0, j))],
        out_specs=pl.BlockSpec((None, rows, d), lambda l, j: (l, 0, j)),
        out_shape=jax.ShapeDtypeStruct((depth, rows, six_d), F32),
        compiler_params=_cparams(("parallel", "parallel")),
        name="ada_mod",
    )(c_rows, ada_w, ada_b.reshape(depth, 1, six_d))


def _norm_mod(x, g, shift, scale):
    y = x * lax.rsqrt(jnp.mean(x * x, axis=-1, keepdims=True) + EPS) * g
    return y * (1.0 + scale) + shift


def _inproj_kernel(*refs, tiles_per_batch, split_in):
    if split_in:
        ctx_ref, x_ref = refs[:2]
        (mod_ref, g_ref, cos_ref, su_ref, sd_ref, wq_ref, wk_ref, wv_ref, wdn_ref, wab_ref, wpool_ref, wg_ref,
         q_ref, k_ref, v_ref, dn_ref, ab_ref, pool_ref, gate_ref, h_out_ref) = refs[2:]
        h = jnp.where(pl.program_id(0) % tiles_per_batch == 0, ctx_ref[...], x_ref[...])
        h_out_ref[...] = h
    else:
        (h_ref, mod_ref, g_ref, cos_ref, su_ref, sd_ref, wq_ref, wk_ref, wv_ref, wdn_ref, wab_ref, wpool_ref,
         wg_ref, q_ref, k_ref, v_ref, dn_ref, ab_ref, pool_ref, gate_ref) = refs
        h = h_ref[...]
    u = _norm_mod(h, g_ref[...], mod_ref[0:1, :], mod_ref[1:2, :]).astype(BF16)
    reps = ATT_W // LANES
    cos = jnp.tile(cos_ref[...], (1, reps))
    s_up = jnp.tile(su_ref[...], (1, reps))
    s_dn = jnp.tile(sd_ref[...], (1, reps))
    quarter = ATT_DH // 4

    def rope(t):
        return (t * cos + pltpu.roll(t, ATT_W - quarter, 1) * s_up + pltpu.roll(t, quarter, 1) * s_dn)

    q = rope(jnp.dot(u, wq_ref[...], preferred_element_type=F32))
    k = rope(jnp.dot(u, wk_ref[...], preferred_element_type=F32))
    v = jnp.dot(u, wv_ref[...], preferred_element_type=F32)
    for hd in range(ATT_HEADS):
        cols = slice(hd * ATT_VD, (hd + 1) * ATT_VD)
        q_ref[hd] = (q[:, cols] * (ATT_DH ** -0.5 * LOG2E)).astype(q_ref.dtype)
        k_ref[hd] = k[:, cols].astype(k_ref.dtype)
        v_ref[hd] = v[:, cols].astype(v_ref.dtype)
    dn_ref[...] = jnp.dot(u, wdn_ref[...], preferred_element_type=F32)
    ab_ref[...] = jnp.dot(u, wab_ref[...], preferred_element_type=F32)
    pool_ref[...] = jnp.dot(u, wpool_ref[...], preferred_element_type=F32)
    gate_ref[...] = _sigmoid(jnp.dot(u, wg_ref[...], preferred_element_type=F32)).astype(gate_ref.dtype)


def _in_projection(h, mod, norm_g, rope_tabs, weights, n_batch, tiles_per_batch):
    split_in = isinstance(h, tuple)
    tpb = tiles_per_batch
    d = mod.shape[-1]
    n_tiles = n_batch * tpb
    rows = n_tiles * ROW_TILE
    widths = [w.shape[1] for w in weights]
    out_dtypes = [BF16, BF16, BF16, F32, F32, F32, BF16]

    def mod_idx(i):
        return (jnp.where(i % tpb == 0, n_batch, i // tpb), 0, 0)

    if split_in:
        h_args = list(h)
        in_specs = [pl.BlockSpec((None, ROW_TILE, d), lambda i: (i // tpb, 0, 0)),
                    pl.BlockSpec((None, ROW_TILE, d), lambda i: (i // tpb, jnp.maximum(i % tpb - 1, 0), 0))]
        widths.append(d)
        out_dtypes.append(F32)
    else:
        h_args = [h]
        in_specs = [pl.BlockSpec((ROW_TILE, d), lambda i: (i, 0))]
    in_specs += [pl.BlockSpec((None, 2, d), mod_idx), _const_spec((1, d))]
    in_specs += [pl.BlockSpec((ROW_TILE, LANES), lambda i: (i % tpb, 0)) for _ in range(3)]
    in_specs += [pl.BlockSpec(w.shape, lambda i: (0, 0), pipeline_mode=pl.Buffered(1)) for w in weights]
    out_specs = [pl.BlockSpec((ROW_TILE, w), lambda i: (i, 0)) for w in widths]
    out_shape = [jax.ShapeDtypeStruct((rows, w), dt) for w, dt in zip(widths, out_dtypes)]
    for o in range(3):
        out_specs[o] = pl.BlockSpec((ATT_HEADS, ROW_TILE, ATT_VD), lambda i: (0, i, 0))
        out_shape[o] = jax.ShapeDtypeStruct((ATT_HEADS, rows, ATT_VD), BF16)
    return pl.pallas_call(
        functools.partial(_inproj_kernel, tiles_per_batch=tpb, split_in=split_in),
        grid=(n_tiles,),
        in_specs=in_specs,
        out_specs=out_specs,
        out_shape=out_shape,
        compiler_params=_cparams(("parallel",)),
        name="in_projection",
    )(*h_args, mod, norm_g.reshape(1, d), *rope_tabs, *weights)


def _rope_tables(n_ctx, seq):
    rows = seq // GRID_W
    row = jnp.repeat(jnp.arange(rows), GRID_W).astype(F32)
    col = jnp.tile(jnp.arange(GRID_W), rows).astype(F32)
    n_freq = ATT_DH // 4
    inv = ROPE_BASE ** (-jnp.arange(n_freq, dtype=F32) / n_freq)
    ar = row[:, None] * inv
    ac = col[:, None] * inv
    cr, sr, cc, sc = jnp.cos(ar), jnp.sin(ar), jnp.cos(ac), jnp.sin(ac)
    z = jnp.zeros_like(sr)
    cos = jnp.concatenate([cr, cr, cc, cc], axis=-1)
    s_up = jnp.concatenate([-sr, z, -sc, z], axis=-1)
    s_dn = jnp.concatenate([z, sr, z, sc], axis=-1)

    def full(t, ctx_val):
        t = jnp.tile(t, (1, LANES // ATT_DH))
        return jnp.concatenate([jnp.full((n_ctx, LANES), ctx_val, F32), t], axis=0)

    return full(cos, 1.0), full(s_up, 0.0), full(s_dn, 0.0)


ATT_KEY_TILE = 1024
LOG2E = math.log2(math.e)


ATT_STREAMS = 2


class _AttnStream:
    def __init__(self, q, e_ref, k_ref, v_ref):
        lane = lax.broadcasted_iota(jnp.int32, q.shape, 1)
        zero = jnp.zeros_like(q)
        self.qm = (jnp.where(lane < ATT_DH, q, zero), jnp.where(lane >= ATT_DH, q, zero))
        self.e_ref, self.k_ref, self.v_ref = e_ref, k_ref, v_ref
        tq = q.shape[0]
        self.m_part = [jnp.full((tq, LANES), -jnp.inf, F32)] * 2
        self.l_part = [jnp.zeros((tq, LANES), F32)] * 2
        self.acc = jnp.zeros((tq, ATT_VD), F32)

    @staticmethod
    def _lane_fold(x, op):
        out = x[:, 0:LANES]
        for c in range(1, x.shape[1] // LANES):
            out = op(out, x[:, c * LANES:(c + 1) * LANES])
        return out

    def scores(self, k0, w):
        kt = self.k_ref[k0:k0 + w, :]
        for m in range(2):
            s = _dot_nt(self.qm[m], kt)
            self.e_ref[m, :, k0:k0 + w] = s
            self.m_part[m] = jnp.maximum(self.m_part[m], self._lane_fold(s, jnp.maximum))

    def end_scores(self):
        self.mx = [jnp.max(p, axis=-1, keepdims=True) for p in self.m_part]

    def exps(self, k0, w):
        for m in range(2):
            e = jnp.exp2(self.e_ref[m, :, k0:k0 + w] - self.mx[m])
            self.e_ref[m, :, k0:k0 + w] = e
            self.l_part[m] = self.l_part[m] + self._lane_fold(e, jnp.add)

    def end_exps(self, lam):
        self.inv0 = 1.0 / jnp.sum(self.l_part[0], axis=-1, keepdims=True)
        self.inv1 = lam / jnp.sum(self.l_part[1], axis=-1, keepdims=True)

    def values(self, k0, w):
        a = self.e_ref[0, :, k0:k0 + w] * self.inv0 - self.e_ref[1, :, k0:k0 + w] * self.inv1
        self.acc = self.acc + jnp.dot(a.astype(BF16), self.v_ref[k0:k0 + w, :], preferred_element_type=F32)

    def result(self, g, lambda_init):
        y = self.acc * lax.rsqrt(jnp.mean(self.acc * self.acc, axis=-1, keepdims=True) + EPS) * g
        return y * (1.0 - lambda_init)


def _attn_run(streams, tiles, lam, g, lambda_init, outs):
    passes = ("scores", "exps", "values")
    for step in range(len(passes) + len(streams) - 1):
        active = [(st, passes[step - i]) for i, st in enumerate(streams) if 0 <= step - i < len(passes)]
        for k0, w in tiles:
            for st, name in active:
                getattr(st, name)(k0, w)
        for st, name in active:
            if name == "scores":
                st.end_scores()
            elif name == "exps":
                st.end_exps(lam)
    for st, (o_ref, rows) in zip(streams, outs):
        o_ref[rows, :] = st.result(g, lambda_init).astype(o_ref.dtype)


def _attn_kernel(*refs, lambda_init, n_ctx, ctx_out):
    if ctx_out:
        lam_ref, g_ref, q_ref, qc_ref, k_ref, v_ref, o_ref, oc_ref, e_ref = refs
    else:
        lam_ref, g_ref, q_ref, k_ref, v_ref, o_ref, e_ref = refs
    n_keys = k_ref.shape[0]
    ctx_tiles = ((0, n_ctx),)
    all_tiles = ctx_tiles + tuple((k0, ATT_KEY_TILE) for k0 in range(n_ctx, n_keys, ATT_KEY_TILE))
    lp = lam_ref[...]
    lam = (jnp.exp(jnp.sum(lp[0:1] * lp[1:2], keepdims=True))
           - jnp.exp(jnp.sum(lp[2:3] * lp[3:4], keepdims=True)) + lambda_init)
    g = g_ref[...]

    if ctx_out:
        @pl.when(pl.program_id(2) == 0)
        def _():
            st = _AttnStream(qc_ref[...], e_ref.at[0], k_ref, v_ref)
            _attn_run([st], ctx_tiles, lam, g, lambda_init, [(oc_ref, slice(None))])

    streams, outs = [], []
    for i in range(ATT_STREAMS):
        rows = slice(i * ROW_TILE, (i + 1) * ROW_TILE)
        streams.append(_AttnStream(q_ref[0, 0, rows, :], e_ref.at[i], k_ref, v_ref))
        outs.append((o_ref, rows))
    _attn_run(streams, all_tiles, lam, g, lambda_init, outs)


def _diff_attention(q, k, v, lam_params, subln_g, lambda_init, n_ctx, ctx_out):
    _, n_batch, t, _ = q.shape
    seq = t - n_ctx
    tq = ATT_STREAMS * ROW_TILE
    assert seq % ATT_KEY_TILE == 0 and seq % tq == 0
    kv_spec = pl.BlockSpec((None, None, t, ATT_VD), lambda b, h, j: (h, b, 0, 0))
    ctx_spec = pl.BlockSpec((None, None, n_ctx, ATT_VD), lambda b, h, j: (h, b, 0, 0))
    in_specs = [_const_spec(lam_params.shape), _const_spec((1, ATT_VD)),
                pl.BlockSpec((pl.Element(1), pl.Element(1), pl.Element(tq), pl.Element(ATT_VD)),
                             lambda b, h, j: (h, b, pl.multiple_of(n_ctx + j * tq, ROW_TILE), 0))]
    out_specs = [pl.BlockSpec((None, None, tq, ATT_VD), lambda b, h, j: (h, b, j, 0))]
    out_shape = [jax.ShapeDtypeStruct((ATT_HEADS, n_batch, seq, ATT_VD), BF16)]
    args = [lam_params, subln_g.reshape(1, ATT_VD), q]
    if ctx_out:
        in_specs.append(ctx_spec)
        args.append(q)
        out_specs.append(ctx_spec)
        out_shape.append(jax.ShapeDtypeStruct((ATT_HEADS, n_batch, n_ctx, ATT_VD), BF16))
    kern = functools.partial(_attn_kernel, lambda_init=lambda_init, n_ctx=n_ctx, ctx_out=ctx_out)
    outs = pl.pallas_call(
        kern,
        grid=(n_batch, ATT_HEADS, seq // tq),
        in_specs=in_specs + [kv_spec, kv_spec],
        out_specs=out_specs,
        out_shape=out_shape,
        scratch_shapes=[pltpu.VMEM((ATT_STREAMS, 2, ROW_TILE, t), F32)],
        compiler_params=_cparams(("parallel", "parallel", "arbitrary")),
        name="diff_attention",
    )(*args, k, v)
    return outs[0], (outs[1] if ctx_out else None)


def _halo_specs(width, tiles_per_batch):
    per = ROW_TILE // HALO

    def prev_idx(b, j):
        return (b, jnp.maximum(j * per - 1, 0), 0)

    def next_idx(b, j):
        return (b, jnp.minimum((j + 1) * per, tiles_per_batch * per - 1), 0)

    return (pl.BlockSpec((None, HALO, width), prev_idx), pl.BlockSpec((None, HALO, width), next_idx))


def _with_halo(prev_ref, cur, next_ref, width, tiles_per_batch, j):
    left_ok = jnp.where(j > 1, 1.0, 0.0)
    right_ok = jnp.where(jnp.logical_and(j > 0, j < tiles_per_batch - 1), 1.0, 0.0)
    return jnp.concatenate([prev_ref[:, 0:width] * left_ok, cur, next_ref[:, 0:width] * right_ok], axis=0)


def _dn_prep_kernel(dn_ref, prev_ref, next_ref, ab_ref, cw_ref, alog_ref, dtb_ref,
                    u_ref, w_ref, qg_ref, qk_ref, kt_ref, gl_ref, *, tiles_per_batch):
    j = pl.program_id(1)
    width = 3 * DN_W
    xe = _with_halo(prev_ref, dn_ref[:, 0:width], next_ref, width, tiles_per_batch, j)
    half = DN_CONV // 2
    acc = None
    n_ext = ROW_TILE + 2 * HALO
    for tap in range(DN_CONV):
        shifted = xe if tap == half else pltpu.roll(xe, (half - tap) % n_ext, 0)
        term = shifted[HALO:HALO + ROW_TILE, :] * cw_ref[tap:tap + 1, :]
        acc = term if acc is None else acc + term
    qkv = _silu(acc)

    lane_w = lax.broadcasted_iota(jnp.int32, (DN_W, DN_W), 1)
    row_w = lax.broadcasted_iota(jnp.int32, (DN_W, DN_W), 0)
    head_ones = jnp.where((lane_w >> DH_SHIFT) == (row_w >> DH_SHIFT), 1.0, 0.0).astype(BF16)

    def head_sums(x):
        hi = x.astype(BF16)
        lo = (x - hi.astype(F32)).astype(BF16)
        return (jnp.dot(hi, head_ones, preferred_element_type=F32)
                + jnp.dot(lo, head_ones, preferred_element_type=F32))

    def l2n(x):
        return x * lax.rsqrt(head_sums(x * x) + EPS)

    qn = l2n(qkv[:, 0:DN_W]) * (DN_DH ** -0.5)
    kn = l2n(qkv[:, DN_W:2 * DN_W])
    vv = qkv[:, 2 * DN_W:3 * DN_W]
    kn_t = kn.T

    ab = ab_ref[...]
    x = ab + dtb_ref[...]
    g = -jnp.exp(alog_ref[...]) * (jnp.maximum(x, 0.0) + jnp.log(1.0 + jnp.exp(-jnp.abs(x))))
    n_dh = 2 * DN_HEADS
    beta = _sigmoid(pltpu.roll(ab, LANES - n_dh, 1))
    r = lax.broadcasted_iota(jnp.int32, (ROW_TILE, ROW_TILE), 0)
    c = lax.broadcasted_iota(jnp.int32, (ROW_TILE, ROW_TILE), 1)
    same_chunk = (r >> CHUNK_SHIFT) == (c >> CHUNK_SHIFT)
    tri_f = jnp.where(jnp.logical_and(same_chunk, c <= r), 1.0, 0.0)
    tri_b = jnp.where(jnp.logical_and(same_chunk, c >= r), 1.0, 0.0)
    tri = jnp.concatenate([tri_f, tri_b], axis=0).astype(BF16)
    both, rest = None, g
    for _ in range(3):
        piece = rest.astype(BF16)
        rest = rest - piece.astype(F32)
        term = jnp.dot(tri, piece, preferred_element_type=F32)
        both = term if both is None else both + term
    lane = lax.broadcasted_iota(jnp.int32, g.shape, 1)
    gcum = jnp.where(lane < DN_HEADS, both[0:ROW_TILE, :], both[ROW_TILE:2 * ROW_TILE, :])
    gcum_t = gcum.T

    n_grp = 2 * DN_PAIRS
    sel_r = lax.broadcasted_iota(jnp.int32, (LANES, n_grp * LANES), 0)
    sel_c = lax.broadcasted_iota(jnp.int32, (LANES, n_grp * LANES), 1)
    grp = sel_c >> LANE_SHIFT
    sel = jnp.where(sel_r == (grp >> 1) * DN_HEADS + (grp & 1) * 2 + ((sel_c >> DH_SHIFT) & 1),
                    1.0, 0.0).astype(BF16)

    def spread(x, pieces):
        out, rest = None, x
        for _ in range(pieces):
            part = rest.astype(BF16)
            rest = rest - part.astype(F32)
            term = jnp.dot(part, sel, preferred_element_type=F32)
            out = term if out is None else out + term
        return out

    g_lanes = spread(gcum, 3)
    b_lanes = spread(beta, 3)

    ri = lax.broadcasted_iota(jnp.int32, (DN_CHUNK, LANES), 0)
    ci = lax.broadcasted_iota(jnp.int32, (DN_CHUNK, LANES), 1) & (DN_DH - 1)
    left = lax.broadcasted_iota(jnp.int32, (DN_CHUNK, LANES), 1) < DN_DH
    eye2 = ri == ci
    same_sub = (ri >> SUB_SHIFT) == (ci >> SUB_SHIFT)
    n_chunks = ROW_TILE // DN_CHUNK

    def bdiag(x):
        return jnp.concatenate([jnp.where(left, x, 0.0), jnp.where(left, 0.0, x)], axis=-2)

    def pmm(a, b):
        return _bmm(a, bdiag(b))

    m_list, xu_list, xw_list = [], [], []
    for ch in range(n_chunks):
        rows = slice(ch * DN_CHUNK, (ch + 1) * DN_CHUNK)
        gram, qkm = [], []
        for p in range(DN_PAIRS):
            cols = slice(p * LANES, (p + 1) * LANES)
            kp = kn[rows, cols]
            kbd = bdiag(kp)
            gram.append(_dot_nt(kp, kbd))
            qkm.append(_dot_nt(qn[rows, cols], kbd))
        for grp_i in range(n_grp):
            d, p = grp_i // DN_PAIRS, grp_i % DN_PAIRS
            cols = slice(p * LANES, (p + 1) * LANES)
            gcols = slice(grp_i * LANES, (grp_i + 1) * LANES)
            incl = (ci <= ri) if d == 0 else (ci >= ri)
            strict = (ci < ri) if d == 0 else (ci > ri)
            last = DN_CHUNK - 1 if d == 0 else 0
            g_col = g_lanes[rows, gcols]
            b_col = b_lanes[rows, gcols]
            g_row = jnp.sum(jnp.where(eye2, g_col, 0.0), axis=0, keepdims=True)
            decay = jnp.where(incl, jnp.exp(jnp.minimum(g_col - g_row, 0.0)), 0.0)
            eg = jnp.exp(g_col)
            kp = kn[rows, cols]
            m_list.append(jnp.where(strict, gram[p] * decay, 0.0) * b_col)
            xu_list.append(vv[rows, cols] * b_col)
            xw_list.append(kp * (b_col * eg))
            qg_ref[grp_i, rows, :] = (qn[rows, cols] * eg).astype(qg_ref.dtype)
            qk_ref[grp_i, rows, :] = (qkm[p] * decay).astype(qk_ref.dtype)
            fac = []
            for hh in range(2):
                idx = d * DN_HEADS + 2 * p + hh
                gr = gcum_t[idx:idx + 1, rows]
                fac.append(jnp.broadcast_to(jnp.exp(gr[:, last:last + 1] - gr), (DN_DH, DN_CHUNK)))
            kt_ref[grp_i, ch] = (kn_t[cols, rows] * jnp.concatenate(fac, axis=0)).astype(kt_ref.dtype)
            gl_ref[grp_i, ch * HALO:(ch + 1) * HALO, :] = jnp.broadcast_to(eg[last:last + 1, :], (HALO, LANES))
    m = jnp.stack(m_list)
    n_diag = jnp.where(same_sub, m, 0.0)
    m_off = m - n_diag
    xp = -n_diag
    dinv = jnp.where(eye2, 1.0, 0.0) + xp
    for _ in range(3):
        xp = pmm(xp, xp)
        dinv = dinv + pmm(dinv, xp)
    f = pmm(dinv, m_off)
    f2 = pmm(f, f)

    def bdiag2(y):
        return jnp.concatenate([bdiag(y[..., 0:LANES]), bdiag(y[..., LANES:2 * LANES])], axis=-1)

    y = _bmm(dinv, jnp.concatenate([bdiag(jnp.stack(xu_list)), bdiag(jnp.stack(xw_list))], axis=-1))
    y = y - _bmm(f, bdiag2(y))
    sol = y + _bmm(f2, bdiag2(y))
    for ch in range(n_chunks):
        rows = slice(ch * DN_CHUNK, (ch + 1) * DN_CHUNK)
        for grp_i in range(n_grp):
            i = ch * n_grp + grp_i
            u_ref[grp_i, rows, :] = sol[i, :, 0:LANES]
            w_ref[grp_i, rows, :] = sol[i, :, LANES:2 * LANES].astype(w_ref.dtype)


def _dn_prep(dn, ab, conv_w, a_log, dt_bias, tiles_per_batch):
    n_batch, t, dn_cols = dn.shape
    n_dh = 2 * DN_HEADS
    n_grp = 2 * DN_PAIRS
    pad = jnp.zeros((1, LANES - n_dh), F32)
    alog_row = jnp.concatenate([a_log.reshape(1, n_dh).astype(F32), pad], axis=-1)
    dtb_row = jnp.concatenate([dt_bias.reshape(1, n_dh).astype(F32), pad], axis=-1)
    prev_spec, next_spec = _halo_specs(dn_cols, tiles_per_batch)
    tok = lambda b, j: (b, 0, j, 0)
    out_specs = [pl.BlockSpec((None, n_grp, ROW_TILE, LANES), tok)] * 4
    out_specs += [pl.BlockSpec((None, n_grp, ROW_TILE // DN_CHUNK, LANES, DN_CHUNK), lambda b, j: (b, 0, j, 0, 0)),
                  pl.BlockSpec((None, n_grp, HALO * ROW_TILE // DN_CHUNK, LANES), tok)]
    tok_shape = (n_batch, n_grp, t, LANES)
    out_shape = [jax.ShapeDtypeStruct(tok_shape, F32),
                 jax.ShapeDtypeStruct(tok_shape, BF16),
                 jax.ShapeDtypeStruct(tok_shape, BF16),
                 jax.ShapeDtypeStruct(tok_shape, BF16),
                 jax.ShapeDtypeStruct((n_batch, n_grp, t // DN_CHUNK, LANES, DN_CHUNK), BF16),
                 jax.ShapeDtypeStruct((n_batch, n_grp, HALO * t // DN_CHUNK, LANES), F32)]
    return pl.pallas_call(
        functools.partial(_dn_prep_kernel, tiles_per_batch=tiles_per_batch),
        grid=(n_batch, tiles_per_batch),
        in_specs=[pl.BlockSpec((None, ROW_TILE, dn_cols), lambda b, j: (b, j, 0)),
                  prev_spec, next_spec,
                  pl.BlockSpec((None, ROW_TILE, LANES), lambda b, j: (b, j, 0)),
                  _const_spec(conv_w.shape), _const_spec((1, LANES)), _const_spec((1, LANES))],
        out_specs=out_specs,
        out_shape=out_shape,
        compiler_params=_cparams(("parallel", "parallel")),
        name="dn_prep",
    )(dn, dn, dn, ab, conv_w, alog_row, dtb_row)


def _dn_scan_kernel(*refs):
    ins, (of_ref, ob_ref, s_ref) = refs[:12], refs[12:]

    @pl.when(pl.program_id(0) == 0)
    def _():
        s_ref[...] = jnp.zeros_like(s_ref)

    left = lax.broadcasted_iota(jnp.int32, (DN_DH, LANES), 1) < DN_DH

    def bdiag(x):
        return jnp.concatenate([jnp.where(left, x, 0.0), jnp.where(left, 0.0, x)], axis=-2)

    for d, o_ref in enumerate((of_ref, ob_ref)):
        u_ref, w_ref, qg_ref, qk_ref, kt_ref, gl_ref = ins[6 * d:6 * d + 6]
        flat = lambda r: r[...].reshape((-1,) + r.shape[2:])
        s = s_ref[d]
        s_bd = bdiag(s)
        both = _bmm(jnp.concatenate([flat(w_ref), flat(qg_ref)], axis=1), s_bd)
        v_new = flat(u_ref) - both[:, 0:DN_CHUNK, :]
        o = both[:, DN_CHUNK:2 * DN_CHUNK, :] + _bmm(flat(qk_ref), bdiag(v_new))
        o_ref[...] = o.reshape(o_ref.shape)
        full = _bmm(flat(kt_ref), v_new)
        s_ref[d] = s * flat(gl_ref)[:, 0:1, :] + jnp.where(left, full[:, 0:DN_DH, :], full[:, DN_DH:2 * DN_DH, :])


def _dn_scan(prep, n_ctx):
    u, w, qg, qk, kt, gl = prep
    n_batch, _, t, _ = u.shape
    n_steps = t // DN_CHUNK
    ctx_chunks = n_ctx // DN_CHUNK

    def chunk_of(d, i):
        if d == 0:
            return i
        return jnp.where(i < ctx_chunks, ctx_chunks - 1 - i, n_steps - 1 + ctx_chunks - i)

    in_specs = []
    for d in range(2):
        tok = functools.partial(lambda i, d: (0, d, chunk_of(d, i), 0), d=d)
        in_specs += [pl.BlockSpec((n_batch, DN_PAIRS, DN_CHUNK, LANES), tok)] * 4
        in_specs += [pl.BlockSpec((n_batch, DN_PAIRS, None, LANES, DN_CHUNK),
                                  functools.partial(lambda i, d: (0, d, chunk_of(d, i), 0, 0), d=d)),
                     pl.BlockSpec((n_batch, DN_PAIRS, HALO, LANES), tok)]
    out_specs = [pl.BlockSpec((n_batch, DN_PAIRS, DN_CHUNK, LANES),
                              functools.partial(lambda i, d: (0, 0, chunk_of(d, i), 0), d=d))
                 for d in range(2)]
    out_shape = [jax.ShapeDtypeStruct((n_batch, DN_PAIRS, t, LANES), F32)] * 2
    return pl.pallas_call(
        _dn_scan_kernel,
        grid=(n_steps,),
        in_specs=in_specs,
        out_specs=out_specs,
        out_shape=out_shape,
        scratch_shapes=[pltpu.VMEM((2, n_batch * DN_PAIRS, DN_DH, LANES), F32)],
        compiler_params=_cparams(("arbitrary",)),
        name="dn_scan",
    )(u, w, qg, qk, kt, gl, u, w, qg, qk, kt, gl)


def _merge_kernel(h_ref, mod_ref, att_ref, attc_ref, of_ref, ob_ref, z_ref, dng_ref,
                  pool_ref, pprev_ref, pnext_ref, pw_ref, ps_ref, gate_ref,
                  wba_ref, wbd_ref, wbp_ref, wo_ref, o_ref, *, tiles_per_batch, first_tile, seq, n_ctx):
    j = pl.program_id(1) + first_tile
    d = h_ref.shape[-1]
    o = jnp.concatenate([of_ref[p] + ob_ref[p] for p in range(DN_PAIRS)], axis=-1)
    lane_w = lax.broadcasted_iota(jnp.int32, (DN_W, DN_W), 1)
    row_w = lax.broadcasted_iota(jnp.int32, (DN_W, DN_W), 0)
    head_ones = jnp.where((lane_w >> DH_SHIFT) == (row_w >> DH_SHIFT), 1.0, 0.0).astype(BF16)
    sq = o * o
    sq_hi = sq.astype(BF16)
    sq_lo = (sq - sq_hi.astype(F32)).astype(BF16)
    ssq = (jnp.dot(sq_hi, head_ones, preferred_element_type=F32)
           + jnp.dot(sq_lo, head_ones, preferred_element_type=F32))
    dn = o * lax.rsqrt(ssq * (1.0 / DN_DH) + EPS) * dng_ref[...] * _silu(z_ref[...])
    cur = pool_ref[...]
    xe = _with_halo(pprev_ref, cur, pnext_ref, POOL_W, tiles_per_batch, j)
    length = jnp.where(j == 0, n_ctx, seq)
    t0 = jnp.where(j == 0, 0, (j - 1) * ROW_TILE)
    tpos = t0 + lax.broadcasted_iota(jnp.int32, (ROW_TILE, POOL_W), 0)
    lane = lax.broadcasted_iota(jnp.int32, (ROW_TILE, POOL_W), 1)
    n_ext = ROW_TILE + 2 * HALO
    sums = {1: xe}
    win = 1
    while win < max(POOL_WINDOWS):
        prev = sums[win]
        sums[2 * win] = prev + pltpu.roll(prev, n_ext - win, 0)
        win *= 2
    mean = jnp.zeros((ROW_TILE, POOL_W), F32)
    for gi, win in enumerate(POOL_WINDOWS):
        wsum = pltpu.roll(sums[win], win // 2, 0)[HALO:HALO + ROW_TILE, :]
        lo = jnp.clip(tpos - win // 2, 0, length)
        hi = jnp.clip(tpos - win // 2 + win, 0, length)
        mean = jnp.where((lane >> DH_SHIFT) == gi, wsum / (hi - lo).astype(F32), mean)
    pooled = _dot(mean - cur, pw_ref[...]) * ps_ref[...]
    ga = gate_ref[:, 0:d].astype(F32)
    gd = gate_ref[:, d:2 * d].astype(F32)
    gp = gate_ref[:, 2 * d:3 * d].astype(F32)
    att = jnp.concatenate([jnp.where(j == 0, attc_ref[hd], att_ref[hd])
                           for hd in range(ATT_HEADS)], axis=-1)
    mix = (ga * jnp.dot(att, wba_ref[...], preferred_element_type=F32)
           + gd * _dot(dn, wbd_ref[...]) + gp * _dot(pooled, wbp_ref[...]))
    y = _dot(mix, wo_ref[...])
    o_ref[...] = h_ref[...] + mod_ref[0:1, :] * y


def _merge(h, mod_gate, att, att_ctx, o_f, o_b, dn, dn_norm_g, pool, pool_w_bd, pool_scale, gates,
           wb_att, wb_dn, wb_pool, w_out, first_tile, n_ctx):
    n_batch, t, d = h.shape
    tpb = t // ROW_TILE
    n_rows = tpb - first_tile
    seq = t - n_ctx
    ft = first_tile
    prev_spec, next_spec = _halo_specs(POOL_W, tpb)
    shift = lambda f: (lambda b, j: f(b, j + ft))
    prev_spec = pl.BlockSpec(prev_spec.block_shape, shift(prev_spec.index_map))
    next_spec = pl.BlockSpec(next_spec.block_shape, shift(next_spec.index_map))
    row = lambda b, j: (b, j + ft, 0)

    def mod_idx(b, j):
        return (jnp.where(j + ft == 0, n_batch, b), 0, 0)

    weights = [wb_att, wb_dn, wb_pool, w_out]
    kern = functools.partial(_merge_kernel, tiles_per_batch=tpb, first_tile=ft, seq=seq, n_ctx=n_ctx)
    return pl.pallas_call(
        kern,
        grid=(n_batch, n_rows),
        in_specs=[pl.BlockSpec((None, ROW_TILE, d), row),
                  pl.BlockSpec((None, 1, d), mod_idx),
                  pl.BlockSpec((ATT_HEADS, None, ROW_TILE, ATT_VD),
                               lambda b, j: (0, b, jnp.maximum(j + ft - 1, 0), 0)),
                  pl.BlockSpec((ATT_HEADS, None, ROW_TILE, ATT_VD), lambda b, j: (0, b, 0, 0)),
                  pl.BlockSpec((None, DN_PAIRS, ROW_TILE, LANES), lambda b, j: (b, 0, j + ft, 0)),
                  pl.BlockSpec((None, DN_PAIRS, ROW_TILE, LANES), lambda b, j: (b, 0, j + ft, 0)),
                  pl.BlockSpec((None, ROW_TILE, DN_W), lambda b, j: (b, j + ft, 3)),
                  _const_spec((1, DN_W)),
                  pl.BlockSpec((None, ROW_TILE, POOL_W), row),
                  prev_spec, next_spec,
                  _const_spec(pool_w_bd.shape), _const_spec((1, POOL_W)),
                  pl.BlockSpec((None, ROW_TILE, 3 * d), row)]
                 + [_const_spec(w.shape) for w in weights],
        out_specs=pl.BlockSpec((None, ROW_TILE, d), lambda b, j: (b, j, 0)),
        out_shape=jax.ShapeDtypeStruct((n_batch, n_rows * ROW_TILE, d), F32),
        compiler_params=_cparams(("parallel", "parallel")),
        name="merge",
    )(h, mod_gate, att, att if att_ctx is None else att_ctx, o_f, o_b, dn,
      jnp.tile(dn_norm_g.reshape(1, DN_DH), (1, DN_HEADS)),
      pool, pool, pool,
      pool_w_bd, pool_scale.reshape(1, POOL_W), gates, *weights)


def _ffn_kernel(*refs, n_experts, n_fchunks, final_norm):
    if n_experts > 1:
        h_ref, mod_ref, g_ref, fg_ref, rw_ref, w1_ref, w3_ref, w2_ref, o_ref, u_ref, acc_ref, comb_ref = refs
    else:
        h_ref, mod_ref, g_ref, fg_ref, w1_ref, w3_ref, w2_ref, o_ref, u_ref, acc_ref = refs
    step = pl.program_id(2)
    e = step // n_fchunks

    @pl.when(step == 0)
    def _():
        u = _norm_mod(h_ref[...], g_ref[...], mod_ref[0:1, :], mod_ref[1:2, :])
        u_ref[...] = u.astype(BF16)
        acc_ref[...] = jnp.zeros_like(acc_ref)
        if n_experts > 1:
            logits = _dot(u, rw_ref[...])
            lane = lax.broadcasted_iota(jnp.int32, logits.shape, 1).astype(F32)
            neg = jnp.float32(-jnp.inf)
            lg = jnp.where(lane < n_experts, logits, neg)
            v1 = jnp.max(lg, axis=-1, keepdims=True)
            i1 = jnp.min(jnp.where(lg == v1, lane, float(LANES)), axis=-1, keepdims=True)
            lg2 = jnp.where(lane == i1, neg, lg)
            v2 = jnp.max(lg2, axis=-1, keepdims=True)
            i2 = jnp.min(jnp.where(lg2 == v2, lane, float(LANES)), axis=-1, keepdims=True)
            e2 = jnp.exp(v2 - v1)
            den = 1.0 + e2
            comb_ref[...] = jnp.where(lane == i1, 1.0 / den, 0.0) + jnp.where(lane == i2, e2 / den, 0.0)

    u = u_ref[...]
    hid = _silu(jnp.dot(u, w1_ref[...], preferred_element_type=F32)) * jnp.dot(
        u, w3_ref[...], preferred_element_type=F32)
    y = _dot(hid, w2_ref[...])
    if n_experts > 1:
        lane = lax.broadcasted_iota(jnp.int32, comb_ref.shape, 1)
        y = y * jnp.sum(jnp.where(lane == e, comb_ref[...], 0.0), axis=-1, keepdims=True)
    acc_ref[...] += y

    @pl.when(step == n_experts * n_fchunks - 1)
    def _():
        out = h_ref[...] + mod_ref[2:3, :] * acc_ref[...]
        if final_norm:
            out = out * lax.rsqrt(jnp.mean(out * out, axis=-1, keepdims=True) + EPS) * fg_ref[...]
        o_ref[...] = out


def _ffn(h, mod, norm_g, final_g, router_w, w1, w3, w2, has_ctx, final_norm):
    n_batch, t, d = h.shape
    n_exp, _, f = w1.shape
    n_rows = t // ROW_TILE
    row = lambda b, j, e: (b, j, 0)

    def mod_idx(b, j, e):
        return (jnp.where(jnp.logical_and(has_ctx, j == 0), n_batch, b), 0, 0)

    in_specs = [pl.BlockSpec((None, ROW_TILE, d), row),
                pl.BlockSpec((None, 3, d), mod_idx),
                _const_spec((1, d)), _const_spec((1, d))]
    args = [h, mod, norm_g.reshape(1, d), final_g.reshape(1, d)]
    scratch = [pltpu.VMEM((ROW_TILE, d), BF16), pltpu.VMEM((ROW_TILE, d), F32)]
    if n_exp > 1:
        in_specs.append(_const_spec(router_w.shape))
        args.append(router_w)
        scratch.append(pltpu.VMEM((ROW_TILE, LANES), F32))
    nfc = 1 if n_exp == 1 else 2
    fc = f // nfc
    assert fc * nfc == f and fc % LANES == 0
    in_specs += [pl.BlockSpec((None, d, fc), lambda b, j, s: (s // nfc, 0, s % nfc)),
                 pl.BlockSpec((None, d, fc), lambda b, j, s: (s // nfc, 0, s % nfc)),
                 pl.BlockSpec((None, fc, d), lambda b, j, s: (s // nfc, s % nfc, 0))]
    args += [w1, w3, w2]
    return pl.pallas_call(
        functools.partial(_ffn_kernel, n_experts=n_exp, n_fchunks=nfc, final_norm=final_norm),
        grid=(n_batch, n_rows, n_exp * nfc),
        in_specs=in_specs,
        out_specs=pl.BlockSpec((None, ROW_TILE, d), row),
        out_shape=jax.ShapeDtypeStruct((n_batch, t, d), F32),
        scratch_shapes=scratch,
        input_output_aliases={0: 0},
        compiler_params=_cparams(("parallel", "parallel", "arbitrary")),
        name="ffn_all_experts" if n_exp > 1 else "ffn_dense",
    )(*args)


MOE_BLOCK = 2048
MOE_SUB = 512
MOE_TILE = 272
MOE_ROW_ALIGN = 16
MOE_ROWS = -(-MOE_BLOCK // MOE_TILE) * MOE_TILE
MOE_PUT_TILE = 256
MOE_FCHUNKS = 4


def _top2_routing(logits, n_experts):
    lane = lax.broadcasted_iota(jnp.int32, logits.shape, 1).astype(F32)
    neg = jnp.float32(-jnp.inf)
    lg = jnp.where(lane < n_experts, logits, neg)
    v1 = jnp.max(lg, axis=-1, keepdims=True)
    i1 = jnp.min(jnp.where(lg == v1, lane, float(LANES)), axis=-1, keepdims=True)
    lg2 = jnp.where(lane == i1, neg, lg)
    v2 = jnp.max(lg2, axis=-1, keepdims=True)
    i2 = jnp.min(jnp.where(lg2 == v2, lane, float(LANES)), axis=-1, keepdims=True)
    e2 = jnp.exp(v2 - v1)
    den = 1.0 + e2
    first = lane == i1
    second = lane == i2
    sel = jnp.where(jnp.logical_or(first, second), 1.0, 0.0)
    comb = jnp.where(first, 1.0 / den, 0.0) + jnp.where(second, e2 / den, 0.0)
    return sel, comb


def _moe_kernel(h_ref, mod_ref, g_ref, fg_ref, rw_ref, w1_ref, w3_ref, w2_ref, o_ref,
                u_ref, x_ref, y_ref, rank_ref, rank_t_ref, comb_ref, cnt_ref, span_ref, *, n_experts, final_norm):
    e = pl.program_id(2)
    f = pl.program_id(3)
    n_sub = MOE_BLOCK // MOE_SUB

    @pl.when(jnp.logical_and(e == 0, f == 0))
    def _route():
        r = lax.broadcasted_iota(jnp.int32, (ROW_TILE, ROW_TILE), 0)
        c = lax.broadcasted_iota(jnp.int32, (ROW_TILE, ROW_TILE), 1)
        before = jnp.where(c < r, 1.0, 0.0).astype(BF16)
        count = jnp.zeros((1, LANES), F32)
        for t in range(MOE_BLOCK // ROW_TILE):
            rows = slice(t * ROW_TILE, (t + 1) * ROW_TILE)
            u = _norm_mod(h_ref[rows, :], g_ref[...], mod_ref[0:1, :], mod_ref[1:2, :])
            u_ref[rows, :] = u.astype(BF16)
            sel, comb = _top2_routing(_dot(u, rw_ref[...]), n_experts)
            rank = jnp.dot(before, sel.astype(BF16), preferred_element_type=F32) + count
            rank_ref[rows, :] = jnp.where(sel > 0.0, rank, -1.0)
            comb_ref[rows, :] = comb
            count = count + jnp.sum(sel, axis=0, keepdims=True)
        lane = lax.broadcasted_iota(jnp.int32, (1, LANES), 1)
        for ee in range(n_experts):
            cnt_ref[ee] = jnp.sum(jnp.where(lane == ee, count, 0.0)).astype(jnp.int32)
        rank_t_ref[...] = rank_ref[...].T
        o_ref[...] = jnp.zeros_like(o_ref)
        for s in range(n_sub):
            rk = rank_ref[s * MOE_SUB:(s + 1) * MOE_SUB, :]
            lo_v = jnp.min(jnp.where(rk >= 0.0, rk, float(MOE_BLOCK)), axis=0, keepdims=True)
            hi_v = jnp.max(rk, axis=0, keepdims=True)
            for ee in range(n_experts):
                span_ref[2 * (s * n_experts + ee)] = jnp.sum(jnp.where(lane == ee, lo_v, 0.0)).astype(jnp.int32)
                span_ref[2 * (s * n_experts + ee) + 1] = jnp.sum(jnp.where(lane == ee, hi_v, 0.0)).astype(jnp.int32)

        @pl.when(jnp.logical_and(pl.program_id(0) == 0, pl.program_id(1) == 0))
        def _():
            y_ref[...] = jnp.zeros_like(y_ref)

    n_rows = cnt_ref[e]
    n_tiles = (n_rows + MOE_TILE - 1) // MOE_TILE
    rank_row = rank_t_ref[pl.ds(e, 1), :]

    def sub_hits(row0, s, tile=MOE_TILE):
        rk = rank_row[:, s * MOE_SUB:(s + 1) * MOE_SUB]
        lo = span_ref[2 * (s * n_experts + e)]
        hi = span_ref[2 * (s * n_experts + e) + 1]
        return rk, jnp.logical_and(hi >= row0, lo < row0 + tile)

    @pl.when(f == 0)
    def _compact():
        def body(i, carry):
            row0 = pl.multiple_of(i * MOE_TILE, MOE_ROW_ALIGN)
            ids = (row0 + lax.broadcasted_iota(jnp.int32, (MOE_TILE, 1), 0)).astype(F32)
            x_ref[pl.ds(row0, MOE_TILE), :] = jnp.zeros((MOE_TILE, x_ref.shape[1]), BF16)
            for s in range(n_sub):
                rk, hit = sub_hits(row0, s)

                @pl.when(hit)
                def _():
                    pick = jnp.where(rk == ids, 1.0, 0.0).astype(BF16)
                    got = jnp.dot(pick, u_ref[s * MOE_SUB:(s + 1) * MOE_SUB, :], preferred_element_type=F32)
                    x_ref[pl.ds(row0, MOE_TILE), :] += got.astype(BF16)
            return carry

        lax.fori_loop(0, n_tiles, body, 0)

    def expert_body(i, carry):
        row0 = pl.multiple_of(i * MOE_TILE, MOE_ROW_ALIGN)
        x = x_ref[pl.ds(row0, MOE_TILE), :]
        hid = _silu(jnp.dot(x, w1_ref[...], preferred_element_type=F32)) * jnp.dot(
            x, w3_ref[...], preferred_element_type=F32)
        y = _dot(hid, w2_ref[...])

        @pl.when(f == 0)
        def _():
            y_ref[pl.ds(row0, MOE_TILE), :] = y

        @pl.when(f > 0)
        def _():
            y_ref[pl.ds(row0, MOE_TILE), :] += y

        return carry

    lax.fori_loop(0, n_tiles, expert_body, 0)

    @pl.when(f == MOE_FCHUNKS - 1)
    def _expand():
        lanes = lax.broadcasted_iota(jnp.int32, (MOE_SUB, LANES), 1)

        def body(i, carry):
            row0 = pl.multiple_of(i * MOE_PUT_TILE, MOE_PUT_TILE)
            yt = y_ref[pl.ds(row0, MOE_PUT_TILE), :].astype(BF16)
            ids = (row0 + lax.broadcasted_iota(jnp.int32, (1, MOE_PUT_TILE), 1)).astype(F32)
            for s in range(n_sub):
                _, hit = sub_hits(row0, s, MOE_PUT_TILE)

                @pl.when(hit)
                def _():
                    rows = slice(s * MOE_SUB, (s + 1) * MOE_SUB)
                    rk_col = jnp.sum(jnp.where(lanes == e, rank_ref[rows, :], 0.0), axis=1, keepdims=True)
                    w_col = jnp.sum(jnp.where(lanes == e, comb_ref[rows, :], 0.0), axis=1, keepdims=True)
                    put = jnp.where(rk_col == ids, 1.0, 0.0).astype(BF16)
                    o_ref[rows, :] += w_col * jnp.dot(put, yt, preferred_element_type=F32)
            return carry

        lax.fori_loop(0, (n_rows + MOE_PUT_TILE - 1) // MOE_PUT_TILE, body, 0)

    @pl.when(jnp.logical_and(e == n_experts - 1, f == MOE_FCHUNKS - 1))
    def _finish():
        for t in range(MOE_BLOCK // ROW_TILE):
            rows = slice(t * ROW_TILE, (t + 1) * ROW_TILE)
            out = h_ref[rows, :] + mod_ref[2:3, :] * o_ref[rows, :]
            if final_norm:
                out = out * lax.rsqrt(jnp.mean(out * out, axis=-1, keepdims=True) + EPS) * fg_ref[...]
            o_ref[rows, :] = out


def _moe(h, mod, norm_g, final_g, router_w, w1, w3, w2, final_norm):
    n_batch, s, d = h.shape
    n_exp, _, f = w1.shape
    assert s % MOE_BLOCK == 0 and f % (MOE_FCHUNKS * LANES) == 0 and n_exp <= 8
    fc = f // MOE_FCHUNKS
    blk = lambda b, j, e, c: (b, j, 0)
    single = pl.Buffered(1)

    return pl.pallas_call(
        functools.partial(_moe_kernel, n_experts=n_exp, final_norm=final_norm),
        grid=(n_batch, s // MOE_BLOCK, n_exp, MOE_FCHUNKS),
        in_specs=[pl.BlockSpec((None, MOE_BLOCK, d), blk, pipeline_mode=single),
                  pl.BlockSpec((None, 3, d), lambda b, j, e, c: (b, 0, 0)),
                  _const_spec((1, d)), _const_spec((1, d)), _const_spec(router_w.shape),
                  pl.BlockSpec((None, d, fc), lambda b, j, e, c: (e, 0, c)),
                  pl.BlockSpec((None, d, fc), lambda b, j, e, c: (e, 0, c)),
                  pl.BlockSpec((None, fc, d), lambda b, j, e, c: (e, c, 0))],
        out_specs=pl.BlockSpec((None, MOE_BLOCK, d), blk, pipeline_mode=single),
        out_shape=jax.ShapeDtypeStruct((n_batch, s, d), F32),
        scratch_shapes=[pltpu.VMEM((MOE_BLOCK, d), BF16),
                        pltpu.VMEM((MOE_ROWS, d), BF16),
                        pltpu.VMEM((MOE_ROWS, d), F32),
                        pltpu.VMEM((MOE_BLOCK, LANES), F32),
                        pltpu.VMEM((LANES, MOE_BLOCK), F32),
                        pltpu.VMEM((MOE_BLOCK, LANES), F32),
                        pltpu.SMEM((HALO,), jnp.int32),
                        pltpu.SMEM((2 * (MOE_BLOCK // MOE_SUB) * HALO,), jnp.int32)],
        compiler_params=_cparams(("arbitrary", "arbitrary", "arbitrary", "arbitrary")),
        name="moe_top2",
    )(h, mod, norm_g.reshape(1, d), final_g.reshape(1, d), router_w, w1, w3, w2)


def _block_diag(w):
    g, n, _ = w.shape
    out = jnp.zeros((g * n, g * n), w.dtype)
    for i in range(g):
        out = out.at[i * n:(i + 1) * n, i * n:(i + 1) * n].set(w[i])
    return out


def kernel(x, c, ctx, c_ctx, ada_w, ada_b, norm1_g, norm2_g, w_in, attn_lambda, attn_subln_g, dn_conv_w,
           dn_a_log, dn_dt_bias, dn_norm_g, pool_w, pool_scale, w_branch, w_out, ffn_w1, ffn_w3, ffn_w2,
           router_w, moe_w1, moe_w3, moe_w2, final_norm_g):
    n_batch, seq, d = x.shape
    n_ctx = ctx.shape[1]
    depth = ada_w.shape[0]
    assert n_ctx == ROW_TILE and seq % ROW_TILE == 0 and seq % GRID_W == 0 and d == 1024
    t = n_ctx + seq
    tpb = t // ROW_TILE

    h = None
    c_rows = jnp.concatenate([c, c_ctx[None, :], jnp.zeros((8 - n_batch - 1, d), F32)], axis=0)
    mods = _ada_mod(c_rows, ada_w, ada_b).reshape(depth, 8, 6, d)
    rope_tabs = _rope_tables(n_ctx, seq)

    sizes = (ATT_W, ATT_W, ATT_W, DN_W, DN_W, DN_W, DN_W, 2 * DN_HEADS, 2 * DN_HEADS, POOL_W, 3 * d)
    offs = np.concatenate([[0], np.cumsum(sizes)]).tolist()

    for l in range(depth):
        last = l == depth - 1
        first_tile = 1 if last else 0
        lambda_init = 0.8 - 0.6 * math.exp(-0.3 * l)
        wl = w_in[l].astype(BF16)
        seg = lambda a, b: wl[:, offs[a]:offs[b]]
        w_ab = jnp.concatenate([seg(7, 9), jnp.zeros((d, LANES - 4 * DN_HEADS), BF16)], axis=1)
        weights = [seg(0, 1), seg(1, 2), seg(2, 3), seg(3, 7), w_ab, seg(9, 10), seg(10, 11)]
        mod = mods[l, :n_batch + 1]
        src = (ctx, x) if l == 0 else h.reshape(n_batch * t, d)
        outs = _in_projection(src, mod[:, 0:2], norm1_g[l], rope_tabs, weights, n_batch, tpb)
        q, k, v, dn, ab, pool, gates = outs[:7]
        if l == 0:
            h = outs[7].reshape(n_batch, t, d)
        q, k, v = (a.reshape(ATT_HEADS, n_batch, t, ATT_VD) for a in (q, k, v))
        att, att_ctx = _diff_attention(q, k, v, attn_lambda[l], attn_subln_g[l], lambda_init, n_ctx, not last)
        dn = dn.reshape(n_batch, t, 4 * DN_W)
        prep = _dn_prep(dn, ab.reshape(n_batch, t, LANES), dn_conv_w[l], dn_a_log[l], dn_dt_bias[l], tpb)
        o_f, o_b = _dn_scan(prep, n_ctx)
        wb = w_branch[l].astype(BF16)
        h = _merge(h, mod[:, 2:3], att, att_ctx, o_f, o_b, dn, dn_norm_g[l], pool.reshape(n_batch, t, POOL_W),
                   _block_diag(pool_w[l]).astype(BF16), pool_scale[l], gates.reshape(n_batch, t, 3 * d),
                   wb[:ATT_W], wb[ATT_W:ATT_W + DN_W], wb[ATT_W + DN_W:], w_out[l].astype(BF16),
                   first_tile, n_ctx)
        i = l // 2
        if l % 2 == 0:
            h = _ffn(h, mod[:, 3:6], norm2_g[l], final_norm_g, None, ffn_w1[i][None].astype(BF16),
                     ffn_w3[i][None].astype(BF16), ffn_w2[i][None].astype(BF16), not last, last)
        else:
            rw = jnp.concatenate([router_w[i], jnp.zeros((d, LANES - N_EXPERTS), F32)], axis=1).astype(BF16)
            w1, w3, w2 = moe_w1[i].astype(BF16), moe_w3[i].astype(BF16), moe_w2[i].astype(BF16)
            if last:
                h = _moe(h, mod[:n_batch, 3:6], norm2_g[l], final_norm_g, rw, w1, w3, w2, True)
            else:
                h = _ffn(h, mod[:, 3:6], norm2_g[l], final_norm_g, rw, w1, w3, w2, True, False)
    return h
```
